```python
import math
import jax, jax.numpy as jnp
from jax import lax
import numpy as np

D_MODEL = 1024
BATCH = 4
SEQ = 4096
DEPTH = 1

N_ATTN_HEADS = 8
SUB_HEAD_DIM = 64
V_HEAD_DIM = 2 * SUB_HEAD_DIM
ATTN_WIDTH = N_ATTN_HEADS * V_HEAD_DIM
ROPE_THETA = 10000.0
Q_BLOCK = 128
LAMBDA_BASE = 0.8
LAMBDA_AMP = 0.6
LAMBDA_RATE = 0.3

LRU_WIDTH = 1024
LRU_BLOCKS = 8
LRU_BLOCK_DIM = LRU_WIDTH // LRU_BLOCKS
CONV_WIDTH = 4
CONV_PAD_LEFT = (CONV_WIDTH - 1) // 2
CONV_PAD_RIGHT = CONV_WIDTH - 1 - CONV_PAD_LEFT
LRU_C = 8.0
N_DIRECTIONS = 2

N_GATED_BRANCHES = 2
PROJ_WIDTH = 3 * ATTN_WIDTH + 2 * LRU_WIDTH + N_GATED_BRANCHES * D_MODEL
SPLIT_POINTS = (ATTN_WIDTH, 2 * ATTN_WIDTH, 3 * ATTN_WIDTH,
                3 * ATTN_WIDTH + LRU_WIDTH, 3 * ATTN_WIDTH + 2 * LRU_WIDTH)

N_GROUPS = 4
EXPERTS_PER_GROUP = 4
N_EXPERTS = N_GROUPS * EXPERTS_PER_GROUP
TOP_K_IN_GROUP = 2
EXPERT_HIDDEN = 512
TOKEN_BLOCK = 128

RMS_EPS = 1e-6

kernel_name = "hybrid_diffattn_rglru_hmoe_encoder"


def rms_norm(x, gain):
    xf = x.astype(jnp.float32)
    y = xf * lax.rsqrt(jnp.mean(xf * xf, axis=-1, keepdims=True) + RMS_EPS)
    return (y * gain.astype(jnp.float32)).astype(x.dtype)


def lambda_init_fn(layer_idx):
    return LAMBDA_BASE - LAMBDA_AMP * math.exp(-LAMBDA_RATE * layer_idx)


def rope_tables(seq, dim):
    pos = jnp.arange(seq, dtype=jnp.float32)
    inv_freq = ROPE_THETA ** (-jnp.arange(0, dim, 2, dtype=jnp.float32) / dim)
    ang = pos[:, None] * inv_freq[None, :]
    return jnp.cos(ang), jnp.sin(ang)


def apply_rope(x, cos, sin):
    xf = x.astype(jnp.float32)
    x1, x2 = jnp.split(xf, 2, axis=-1)
    c = cos[None, :, None, None, :]
    s = sin[None, :, None, None, :]
    return jnp.concatenate([x1 * c - x2 * s, x1 * s + x2 * c], axis=-1).astype(x.dtype)


def diff_attention(q, k, v, lam):
    B, S, H, _, dh = q.shape
    scale = dh ** -0.5
    n_blk = S // Q_BLOCK
    qb = q.reshape(B, n_blk, Q_BLOCK, H, 2, dh).transpose(1, 0, 2, 3, 4, 5)

    def one_block(q_blk):
        s = jnp.einsum('bqhcd,bkhcd->bhcqk', q_blk, k,
                       preferred_element_type=jnp.float32) * scale
        p = jax.nn.softmax(s, axis=-1)
        p_diff = p[:, :, 0] - lam * p[:, :, 1]
        return jnp.einsum('bhqk,bkhd->bqhd', p_diff.astype(v.dtype), v)

    out = lax.map(one_block, qb)
    return out.transpose(1, 0, 2, 3, 4).reshape(B, S, H, v.shape[-1])


def centred_depthwise_conv(x, w, b):
    y = lax.conv_general_dilated(
        x, w[:, None, :], window_strides=(1,),
        padding=[(CONV_PAD_LEFT, CONV_PAD_RIGHT)],
        dimension_numbers=('NWC', 'WIO', 'NWC'),
        feature_group_count=x.shape[-1])
    return y + b


def block_diag_linear(x, w, b):
    B, S, C = x.shape
    xb = x.reshape(B, S, LRU_BLOCKS, LRU_BLOCK_DIM)
    y = jnp.einsum('bsnd,nde->bsne', xb, w)
    return y.reshape(B, S, C) + b


def rg_lru(x, w_a, b_a, w_i, b_i, lam_param, reverse):
    r = jax.nn.sigmoid(block_diag_linear(x, w_a, b_a).astype(jnp.float32))
    i = jax.nn.sigmoid(block_diag_linear(x, w_i, b_i).astype(jnp.float32))
    log_a = -LRU_C * r * jax.nn.softplus(-lam_param.astype(jnp.float32))
    a = jnp.exp(log_a)
    mult = jnp.sqrt(-jnp.expm1(2.0 * log_a))
    u = mult * (i * x.astype(jnp.float32))

    def combine(c1, c2):
        a1, b1 = c1
        a2, b2 = c2
        return a1 * a2, a2 * b1 + b2

    _, h = lax.associative_scan(combine, (a, u), axis=1, reverse=reverse)
    return h.astype(x.dtype)


def hierarchical_moe(h, w_group, b_group, w_er, b_er, w_gate, w_up, w_down):
    B, S, D = h.shape
    N = B * S
    xt = h.reshape(N, D)
    g_logits = (xt @ w_group + b_group).astype(jnp.float32)
    g_prob = jax.nn.softmax(g_logits, axis=-1)
    g_sel = jnp.argmax(g_logits, axis=-1)
    g_w = jnp.take_along_axis(g_prob, g_sel[:, None], axis=-1)
    e_logits = (xt @ w_er + b_er).astype(jnp.float32).reshape(N, N_GROUPS, EXPERTS_PER_GROUP)
    e_sel_logits = jnp.take_along_axis(e_logits, g_sel[:, None, None], axis=1)[:, 0]
    top_val, top_idx = lax.top_k(e_sel_logits, TOP_K_IN_GROUP)
    top_w = jax.nn.softmax(top_val, axis=-1) * g_w
    expert_id = g_sel[:, None] * EXPERTS_PER_GROUP + top_idx
    combine_w = jnp.sum(jax.nn.one_hot(expert_id, N_EXPERTS, dtype=jnp.float32)
                        * top_w[..., None], axis=1)

    n_blk = S // TOKEN_BLOCK
    xb = h.reshape(B, n_blk, TOKEN_BLOCK, D).transpose(1, 0, 2, 3).reshape(n_blk, B * TOKEN_BLOCK, D)
    cb = combine_w.reshape(B, n_blk, TOKEN_BLOCK, N_EXPERTS).transpose(1, 0, 2, 3)
    cb = cb.reshape(n_blk, B * TOKEN_BLOCK, N_EXPERTS).astype(h.dtype)

    def expert_block(args):
        xs, cs = args
        a = jnp.einsum('nd,edf->nef', xs, w_gate)
        u = jnp.einsum('nd,edf->nef', xs, w_up)
        hid = jax.nn.silu(a) * u * cs[:, :, None]
        return jnp.einsum('nef,efd->nd', hid, w_down)

    y = lax.map(expert_block, (xb, cb))
    y = y.reshape(n_blk, B, TOKEN_BLOCK, D).transpose(1, 0, 2, 3)
    return y.reshape(B, S, D)


def setup_inputs(seed: int = 0) -> dict:
    key = jax.random.key(seed)
    ks = jax.random.split(key, 32)
    f32 = jnp.float32
    nrm = lambda k, shape, scale: jax.random.normal(k, shape, f32) * scale
    gain = lambda k, shape: jnp.ones(shape, f32) + 0.02 * jax.random.normal(k, shape, f32)

    a_c = jax.random.uniform(ks[16], (DEPTH, N_DIRECTIONS, LRU_WIDTH), f32, 0.9, 0.999)
    a_base = a_c ** (1.0 / LRU_C)
    lru_lambda = jnp.log(a_base) - jnp.log1p(-a_base)

    return {
        "x": jax.random.normal(ks[0], (BATCH, SEQ, D_MODEL), f32),
        "norm1_gain": gain(ks[1], (DEPTH, D_MODEL)),
        "w_in": nrm(ks[2], (DEPTH, D_MODEL, PROJ_WIDTH), D_MODEL ** -0.5),
        "b_gates": nrm(ks[3], (DEPTH, N_GATED_BRANCHES * D_MODEL), 0.02),
        "q_norm_gain": gain(ks[4], (DEPTH, SUB_HEAD_DIM)),
        "k_norm_gain": gain(ks[5], (DEPTH, SUB_HEAD_DIM)),
        "lambda_q1": nrm(ks[6], (DEPTH, SUB_HEAD_DIM), 0.1),
        "lambda_k1": nrm(ks[7], (DEPTH, SUB_HEAD_DIM), 0.1),
        "lambda_q2": nrm(ks[8], (DEPTH, SUB_HEAD_DIM), 0.1),
        "lambda_k2": nrm(ks[9], (DEPTH, SUB_HEAD_DIM), 0.1),
        "attn_subln_gain": gain(ks[10], (DEPTH, V_HEAD_DIM)),
        "w_attn_o": nrm(ks[11], (DEPTH, ATTN_WIDTH, D_MODEL), ATTN_WIDTH ** -0.5),
        "conv_w": nrm(ks[12], (DEPTH, CONV_WIDTH, LRU_WIDTH), CONV_WIDTH ** -0.5),
        "conv_b": nrm(ks[13], (DEPTH, LRU_WIDTH), 0.02),
        "lru_wa": nrm(ks[14], (DEPTH, N_DIRECTIONS, LRU_BLOCKS, LRU_BLOCK_DIM, LRU_BLOCK_DIM), LRU_BLOCK_DIM ** -0.5),
        "lru_ba": nrm(ks[15], (DEPTH, N_DIRECTIONS, LRU_WIDTH), 0.02),
        "lru_wi": nrm(ks[17], (DEPTH, N_DIRECTIONS, LRU_BLOCKS, LRU_BLOCK_DIM, LRU_BLOCK_DIM), LRU_BLOCK_DIM ** -0.5),
        "lru_bi": nrm(ks[18], (DEPTH, N_DIRECTIONS, LRU_WIDTH), 0.02),
        "lru_lambda": lru_lambda,
        "w_lru_o": nrm(ks[19], (DEPTH, LRU_WIDTH, D_MODEL), LRU_WIDTH ** -0.5),
        "w_out": nrm(ks[20], (DEPTH, D_MODEL, D_MODEL), D_MODEL ** -0.5),
        "norm2_gain": gain(ks[21], (DEPTH, D_MODEL)),
        "w_group_router": nrm(ks[22], (DEPTH, D_MODEL, N_GROUPS), D_MODEL ** -0.5),
        "b_group_router": nrm(ks[23], (DEPTH, N_GROUPS), 0.01),
        "w_expert_router": nrm(ks[24], (DEPTH, D_MODEL, N_EXPERTS), D_MODEL ** -0.5),
        "b_expert_router": nrm(ks[25], (DEPTH, N_EXPERTS), 0.01),
        "w_expert_gate": nrm(ks[26], (DEPTH, N_EXPERTS, D_MODEL, EXPERT_HIDDEN), D_MODEL ** -0.5),
        "w_expert_up": nrm(ks[27], (DEPTH, N_EXPERTS, D_MODEL, EXPERT_HIDDEN), D_MODEL ** -0.5),
        "w_expert_down": nrm(ks[28], (DEPTH, N_EXPERTS, EXPERT_HIDDEN, D_MODEL), EXPERT_HIDDEN ** -0.5),
    }


def reference(x, norm1_gain, w_in, b_gates, q_norm_gain, k_norm_gain,
              lambda_q1, lambda_k1, lambda_q2, lambda_k2, attn_subln_gain, w_attn_o,
              conv_w, conv_b, lru_wa, lru_ba, lru_wi, lru_bi, lru_lambda, w_lru_o,
              w_out, norm2_gain, w_group_router, b_group_router,
              w_expert_router, b_expert_router, w_expert_gate, w_expert_up, w_expert_down):
    B, S, D = x.shape
    cos, sin = rope_tables(S, SUB_HEAD_DIM)
    for l in range(DEPTH):
        lam_init = lambda_init_fn(l)
        h = rms_norm(x, norm1_gain[l])
        proj = jnp.einsum('bsd,dc->bsc', h, w_in[l])
        q, k, v, lru_x, lru_g, gate_pre = jnp.split(proj, SPLIT_POINTS, axis=-1)

        q = q.reshape(B, S, N_ATTN_HEADS, 2, SUB_HEAD_DIM)
        k = k.reshape(B, S, N_ATTN_HEADS, 2, SUB_HEAD_DIM)
        v = v.reshape(B, S, N_ATTN_HEADS, V_HEAD_DIM)
        q = apply_rope(rms_norm(q, q_norm_gain[l]), cos, sin)
        k = apply_rope(rms_norm(k, k_norm_gain[l]), cos, sin)
        lam = (jnp.exp(jnp.sum(lambda_q1[l].astype(jnp.float32) * lambda_k1[l].astype(jnp.float32)))
               - jnp.exp(jnp.sum(lambda_q2[l].astype(jnp.float32) * lambda_k2[l].astype(jnp.float32)))
               + lam_init)
        o = diff_attention(q, k, v, lam)
        o = rms_norm(o, attn_subln_gain[l]) * (1.0 - lam_init)
        attn_d = jnp.einsum('bsc,cd->bsd', o.reshape(B, S, ATTN_WIDTH), w_attn_o[l])

        xr = centred_depthwise_conv(lru_x, conv_w[l], conv_b[l])
        h_fwd = rg_lru(xr, lru_wa[l, 0], lru_ba[l, 0], lru_wi[l, 0], lru_bi[l, 0], lru_lambda[l, 0], False)
        h_bwd = rg_lru(xr, lru_wa[l, 1], lru_ba[l, 1], lru_wi[l, 1], lru_bi[l, 1], lru_lambda[l, 1], True)
        y_lru = (h_fwd + h_bwd) * jax.nn.gelu(lru_g)
        lru_d = jnp.einsum('bsc,cd->bsd', y_lru, w_lru_o[l])

        g_attn, g_lru = jnp.split(jax.nn.sigmoid(gate_pre + b_gates[l]), N_GATED_BRANCHES, axis=-1)
        merged = g_attn * attn_d + g_lru * lru_d
        x = x + jnp.einsum('bsd,de->bse', merged, w_out[l])

        h2 = rms_norm(x, norm2_gain[l])
        x = x + hierarchical_moe(h2, w_group_router[l], b_group_router[l],
                                 w_expert_router[l], b_expert_router[l],
                                 w_expert_gate[l], w_expert_up[l], w_expert_down[l])
    return x
```

```python
import functools
import math

import jax
import jax.numpy as jnp
from jax import lax
from jax.experimental import pallas as pl
from jax.experimental.pallas import tpu as pltpu

F32 = jnp.float32
BF16 = jnp.bfloat16

D_MODEL = 1024
N_HEADS = 8
SUB_HEAD = 64
V_HEAD = 128
ATTN_WIDTH = 1024
LRU_WIDTH = 1024
LRU_BLOCKS = 8
LRU_BLOCK_DIM = 128
LRU_C = 8.0
PROJ_WIDTH = 7168
N_GROUPS = 4
EXPERTS_PER_GROUP = 4
N_EXPERTS = 16
EXPERT_HIDDEN = 512
ROPE_THETA = 10000.0
RMS_EPS = 1e-6
LAMBDA_INIT = 0.8 - 0.6 * math.exp(-0.3 * 0)

LANES = 128
SUBLANES = 8
VMEM_LIMIT = 56 * 1024 * 1024

Q_COL, K_COL, V_COL, LRUX_COL, LRUG_COL = 0, 8, 16, 24, 32
GATE_COL_1024 = 5

ROUTER_LANES = 128
EXPERT_LANE0 = N_GROUPS


def _rms(x, gain):
    ms = jnp.mean(x * x, axis=-1, keepdims=True)
    return x * lax.rsqrt(ms + RMS_EPS) * gain


def _inproj_kernel(x_ref, g_ref, w_ref, o_ref):
    h = _rms(x_ref[...], g_ref[...]).astype(BF16)
    for j in range(PROJ_WIDTH // 1024):
        cols = slice(j * 1024, (j + 1) * 1024)
        o_ref[:, cols] = jnp.dot(h, w_ref[:, cols], preferred_element_type=F32).astype(BF16)


def _inproj(x2, gain, w_bf16, tm=256):
    n = x2.shape[0]
    return pl.pallas_call(
        _inproj_kernel,
        out_shape=jax.ShapeDtypeStruct((n, PROJ_WIDTH), BF16),
        grid=(n // tm,),
        in_specs=[
            pl.BlockSpec((tm, D_MODEL), lambda i: (i, 0)),
            pl.BlockSpec((1, D_MODEL), lambda i: (0, 0)),
            pl.BlockSpec((D_MODEL, PROJ_WIDTH), lambda i: (0, 0)),
        ],
        out_specs=pl.BlockSpec((tm, PROJ_WIDTH), lambda i: (i, 0)),
        compiler_params=pltpu.CompilerParams(
            dimension_semantics=("arbitrary",), vmem_limit_bytes=VMEM_LIMIT),
        name="inproj",
    )(x2, gain, w_bf16)


def _subhead_norm_rope(x, gain, cos, sin_signed):
    r = lax.broadcasted_iota(jnp.int32, (LANES, LANES), 0) // SUB_HEAD
    c = lax.broadcasted_iota(jnp.int32, (LANES, LANES), 1) // SUB_HEAD
    group_mean = jnp.where(r == c, 1.0 / SUB_HEAD, 0.0).astype(F32)
    ms = jnp.dot(x * x, group_mean, preferred_element_type=F32, precision=lax.Precision.HIGHEST)
    xn = x * lax.rsqrt(ms + RMS_EPS) * gain
    lane = lax.broadcasted_iota(jnp.int32, xn.shape, 1)
    first_half = (lane % SUB_HEAD) < (SUB_HEAD // 2)
    partner = jnp.where(first_half,
                        pltpu.roll(xn, LANES - SUB_HEAD // 2, axis=1),
                        pltpu.roll(xn, SUB_HEAD // 2, axis=1))
    return xn * cos + partner * sin_signed


def _attn_kernel(lam_ref, qg_ref, kg_ref, sg_ref, cq_ref, sq_ref, ck_ref, sk_ref,
                 q_ref, k_ref, v_ref, o_ref, kt_ref, s_ref, m_ref, l_ref, acc_ref,
                 *, tq, tk, seq):
    i = pl.program_id(2)
    n_k = seq // tk

    @pl.when(i == 0)
    def _prep_keys():
        for c in range(n_k):
            rows = slice(c * tk, (c + 1) * tk)
            kr = _subhead_norm_rope(k_ref[0, rows, :].astype(F32), kg_ref[...],
                                    ck_ref[rows, :], sk_ref[rows, :])
            kt_ref[:, rows] = kr.T.astype(BF16)

    lp = lam_ref[...]
    lam = (jnp.exp(jnp.sum(lp[0:1] * lp[1:2], axis=-1, keepdims=True))
           - jnp.exp(jnp.sum(lp[2:3] * lp[3:4], axis=-1, keepdims=True)) + LAMBDA_INIT)

    qr = _subhead_norm_rope(q_ref[0].astype(F32), qg_ref[...], cq_ref[...], sq_ref[...])
    qr = qr * (SUB_HEAD ** -0.5)
    lane = lax.broadcasted_iota(jnp.int32, qr.shape, 1)
    zero = jnp.zeros_like(qr)
    qq = jnp.concatenate([jnp.where(lane < SUB_HEAD, qr, zero),
                          jnp.where(lane >= SUB_HEAD, qr, zero)], axis=0).astype(BF16)

    def fold_lanes(x, op):
        out = x[:, 0:LANES]
        for t in range(1, tk // LANES):
            out = op(out, x[:, t * LANES:(t + 1) * LANES])
        return out

    m_ref[...] = jnp.full(m_ref.shape, -jnp.inf, F32)

    def scores(c, carry):
        off = pl.multiple_of(c * tk, tk)
        s = jnp.dot(qq, kt_ref[:, pl.ds(off, tk)], preferred_element_type=F32)
        s_ref[:, pl.ds(off, tk)] = s
        m_ref[...] = jnp.maximum(m_ref[...], fold_lanes(s, jnp.maximum))
        return carry

    lax.fori_loop(0, n_k, scores, 0)
    m_ref[...] = jnp.broadcast_to(jnp.max(m_ref[...], axis=-1, keepdims=True), m_ref.shape)
    l_ref[...] = jnp.zeros(l_ref.shape, F32)
    acc_ref[...] = jnp.zeros(acc_ref.shape, F32)

    def weighted(c, carry):
        off = pl.multiple_of(c * tk, tk)
        s = s_ref[:, pl.ds(off, tk)]
        m = m_ref[...]
        p = jnp.exp(s - jnp.concatenate([m] * (tk // LANES), axis=1))
        l_ref[...] += fold_lanes(p, jnp.add)
        acc_ref[...] += jnp.dot(p.astype(BF16), v_ref[0, pl.ds(off, tk), :],
                                preferred_element_type=F32)
        return carry

    lax.fori_loop(0, n_k, weighted, 0)
    l = jnp.sum(l_ref[...], axis=-1, keepdims=True)
    o = acc_ref[...] / l
    o = o[0:tq] - lam * o[tq:2 * tq]
    o = _rms(o, sg_ref[...]) * (1.0 - LAMBDA_INIT)
    o_ref[0] = o.astype(BF16)


def _attention(proj3, lam_params, q_gain, k_gain, subln_gain, cos, sin_signed, tq=128, tk=512):
    b, s, _ = proj3.shape
    kernel = functools.partial(_attn_kernel, tq=tq, tk=tk, seq=s)
    const = lambda shape: pl.BlockSpec(shape, lambda bi, hi, qi: (0, 0))
    return pl.pallas_call(
        kernel,
        out_shape=jax.ShapeDtypeStruct((b, s, ATTN_WIDTH), BF16),
        grid=(b, N_HEADS, s // tq),
        in_specs=[
            const((4, SUB_HEAD)),
            const((1, LANES)), const((1, LANES)), const((1, V_HEAD)),
            pl.BlockSpec((tq, LANES), lambda bi, hi, qi: (qi, 0)),
            pl.BlockSpec((tq, LANES), lambda bi, hi, qi: (qi, 0)),
            const((s, LANES)), const((s, LANES)),
            pl.BlockSpec((1, tq, LANES), lambda bi, hi, qi: (bi, qi, Q_COL + hi)),
            pl.BlockSpec((1, s, LANES), lambda bi, hi, qi: (bi, 0, K_COL + hi)),
            pl.BlockSpec((1, s, V_HEAD), lambda bi, hi, qi: (bi, 0, V_COL + hi)),
        ],
        out_specs=pl.BlockSpec((1, tq, V_HEAD), lambda bi, hi, qi: (bi, qi, hi)),
        scratch_shapes=[
            pltpu.VMEM((LANES, s), BF16),
            pltpu.VMEM((2 * tq, s), F32),
            pltpu.VMEM((2 * tq, LANES), F32),
            pltpu.VMEM((2 * tq, LANES), F32),
            pltpu.VMEM((2 * tq, V_HEAD), F32),
        ],
        compiler_params=pltpu.CompilerParams(
            dimension_semantics=("arbitrary", "arbitrary", "arbitrary"),
            vmem_limit_bytes=VMEM_LIMIT),
        name="diff_attn",
    )(lam_params, q_gain, k_gain, subln_gain, cos, sin_signed, cos, sin_signed,
      proj3, proj3, proj3)


def _softplus(x):
    return jnp.maximum(x, 0.0) + jnp.log1p(jnp.exp(-jnp.abs(x)))


def _gelu_tanh(x):
    return 0.5 * x * (1.0 + jnp.tanh(math.sqrt(2.0 / math.pi) * (x + 0.044715 * (x * x * x))))


def _lru_kernel(x_ref, g_ref, cw_ref, cb_ref, w_ref, b_ref, lam_ref, o_ref,
                xs_ref, a0_ref, u0_ref, a1_ref, u1_ref, *, seq, tc):
    pad = SUBLANES
    zeros_pad = jnp.zeros((pad, LANES), F32)
    xs_ref[0:pad, :] = zeros_pad
    xs_ref[pad + seq:pad + seq + pad, :] = zeros_pad
    xs_ref[pad:pad + seq, :] = x_ref[0].astype(F32)

    decay = -LRU_C * _softplus(-lam_ref[...])
    cw = cw_ref[...]
    cb = cb_ref[...]
    bias = b_ref[0]

    for c in range(seq // tc):
        base = pad + c * tc
        xr = cb + sum(cw[j:j + 1] * xs_ref[base + j - 1:base + j - 1 + tc, :] for j in range(4))
        z = jnp.dot(xr.astype(BF16), w_ref[0], preferred_element_type=F32) + bias
        rows = slice(c * tc, (c + 1) * tc)
        for d, (a_ref, u_ref) in enumerate(((a0_ref, u0_ref), (a1_ref, u1_ref))):
            r = jax.nn.sigmoid(z[:, (2 * d) * LANES:(2 * d + 1) * LANES])
            gate_i = jax.nn.sigmoid(z[:, (2 * d + 1) * LANES:(2 * d + 2) * LANES])
            a = jnp.exp(decay[d:d + 1] * r)
            a_ref[rows, :] = a
            u_ref[rows, :] = jnp.sqrt(1.0 - a * a) * (gate_i * xr)

    row = lax.broadcasted_iota(jnp.int32, (SUBLANES, LANES), 0)
    n_grp = seq // SUBLANES

    def step(t, carry):
        hf, hb = carry
        of = pl.multiple_of(t * SUBLANES, SUBLANES)
        a = a0_ref[pl.ds(of, SUBLANES), :]
        u = u0_ref[pl.ds(of, SUBLANES), :]
        for d in (1, 2, 4):
            valid = row >= d
            u = jnp.where(valid, a * pltpu.roll(u, d, axis=0) + u, u)
            a = jnp.where(valid, a * pltpu.roll(a, d, axis=0), a)
        h = a * hf + u
        u0_ref[pl.ds(of, SUBLANES), :] = h
        hf = jnp.broadcast_to(h[SUBLANES - 1:SUBLANES, :], h.shape)
        ob = pl.multiple_of((n_grp - 1 - t) * SUBLANES, SUBLANES)
        a = a1_ref[pl.ds(ob, SUBLANES), :]
        u = u1_ref[pl.ds(ob, SUBLANES), :]
        for d in (1, 2, 4):
            valid = row < SUBLANES - d
            u = jnp.where(valid, a * pltpu.roll(u, SUBLANES - d, axis=0) + u, u)
            a = jnp.where(valid, a * pltpu.roll(a, SUBLANES - d, axis=0), a)
        h = a * hb + u
        u1_ref[pl.ds(ob, SUBLANES), :] = h
        hb = jnp.broadcast_to(h[0:1, :], h.shape)
        return hf, hb

    zero = jnp.zeros((SUBLANES, LANES), F32)
    lax.fori_loop(0, n_grp, step, (zero, zero), unroll=2)

    for c in range(seq // tc):
        rows = slice(c * tc, (c + 1) * tc)
        y = (u0_ref[rows, :] + u1_ref[rows, :]) * _gelu_tanh(g_ref[0, rows, :].astype(F32))
        o_ref[0, rows, :] = y.astype(BF16)


def _lru(proj3, conv_w, conv_b, w_cat, b_cat, lam, tc=512):
    b, s, _ = proj3.shape
    kernel = functools.partial(_lru_kernel, seq=s, tc=tc)
    return pl.pallas_call(
        kernel,
        out_shape=jax.ShapeDtypeStruct((b, s, LRU_WIDTH), BF16),
        grid=(b, LRU_BLOCKS),
        in_specs=[
            pl.BlockSpec((1, s, LANES), lambda bi, ni: (bi, 0, LRUX_COL + ni)),
            pl.BlockSpec((1, s, LANES), lambda bi, ni: (bi, 0, LRUG_COL + ni)),
            pl.BlockSpec((4, LANES), lambda bi, ni: (0, ni)),
            pl.BlockSpec((1, LANES), lambda bi, ni: (0, ni)),
            pl.BlockSpec((1, LRU_BLOCK_DIM, 4 * LRU_BLOCK_DIM), lambda bi, ni: (ni, 0, 0)),
            pl.BlockSpec((1, 1, 4 * LRU_BLOCK_DIM), lambda bi, ni: (ni, 0, 0)),
            pl.BlockSpec((2, LANES), lambda bi, ni: (0, ni)),
        ],
        out_specs=pl.BlockSpec((1, s, LANES), lambda bi, ni: (bi, 0, ni)),
        scratch_shapes=[pltpu.VMEM((s + 2 * SUBLANES, LANES), F32)]
        + [pltpu.VMEM((s, LANES), F32)] * 4,
        compiler_params=pltpu.CompilerParams(
            dimension_semantics=("arbitrary", "arbitrary"), vmem_limit_bytes=VMEM_LIMIT),
        name="rglru",
    )(proj3, proj3, conv_w, conv_b, w_cat, b_cat, lam)


def _merge_kernel(x_ref, ao_ref, yl_ref, ga_ref, gl_ref, bg_ref, wa_ref, wl_ref, wo_ref,
                  n2_ref, wr_ref, br_ref, x1_ref, h2_ref, cw_ref):
    attn_d = jnp.dot(ao_ref[...], wa_ref[...], preferred_element_type=F32)
    lru_d = jnp.dot(yl_ref[...], wl_ref[...], preferred_element_type=F32)
    bg = bg_ref[...]
    g_attn = jax.nn.sigmoid(ga_ref[...].astype(F32) + bg[:, 0:D_MODEL])
    g_lru = jax.nn.sigmoid(gl_ref[...].astype(F32) + bg[:, D_MODEL:2 * D_MODEL])
    merged = g_attn * attn_d + g_lru * lru_d
    x1 = x_ref[...] + jnp.dot(merged.astype(BF16), wo_ref[...], preferred_element_type=F32)
    x1_ref[...] = x1
    h2 = _rms(x1, n2_ref[...])
    h2_ref[...] = h2.astype(BF16)

    logits = jnp.dot(h2, wr_ref[...], preferred_element_type=F32,
                     precision=lax.Precision.HIGHEST) + br_ref[...]
    lane = lax.broadcasted_iota(jnp.int32, logits.shape, 1)
    neg = jnp.full_like(logits, -jnp.inf)
    big = jnp.full_like(lane, ROUTER_LANES)

    def masked_max(mask):
        return jnp.max(jnp.where(mask, logits, neg), axis=-1, keepdims=True)

    def first_lane(mask, value):
        return jnp.min(jnp.where(mask & (logits == value), lane, big), axis=-1, keepdims=True)

    g_mask = lane < N_GROUPS
    g_max = masked_max(g_mask)
    g_sel = first_lane(g_mask, g_max)
    g_w = 1.0 / jnp.sum(jnp.where(g_mask, jnp.exp(logits - g_max), 0.0), axis=-1, keepdims=True)
    e_lo = EXPERT_LANE0 + g_sel * EXPERTS_PER_GROUP
    e_mask = (lane >= e_lo) & (lane < e_lo + EXPERTS_PER_GROUP)
    v1 = masked_max(e_mask)
    i1 = first_lane(e_mask, v1)
    e_mask2 = e_mask & (lane != i1)
    v2 = masked_max(e_mask2)
    i2 = first_lane(e_mask2, v2)
    t = jnp.exp(v2 - v1)
    w1 = g_w / (1.0 + t)
    w2 = g_w * t / (1.0 + t)
    cw_ref[...] = jnp.where(lane == i1, w1, 0.0) + jnp.where(lane == i2, w2, 0.0)


def _merge(x2, attn_o, y_lru, proj, b_gates, wa, wl, wo, n2_gain, w_router, b_router, tm=256):
    n = x2.shape[0]
    row = lambda cols, col_blk=0: pl.BlockSpec((tm, cols), lambda i: (i, col_blk))
    const = lambda shape: pl.BlockSpec(shape, lambda i: (0, 0))
    return pl.pallas_call(
        _merge_kernel,
        out_shape=(jax.ShapeDtypeStruct((n, D_MODEL), F32),
                   jax.ShapeDtypeStruct((n, D_MODEL), BF16),
                   jax.ShapeDtypeStruct((n, ROUTER_LANES), F32)),
        grid=(n // tm,),
        in_specs=[
            row(D_MODEL), row(ATTN_WIDTH), row(LRU_WIDTH),
            row(D_MODEL, GATE_COL_1024), row(D_MODEL, GATE_COL_1024 + 1),
            const((1, 2 * D_MODEL)),
            const((ATTN_WIDTH, D_MODEL)), const((LRU_WIDTH, D_MODEL)), const((D_MODEL, D_MODEL)),
            const((1, D_MODEL)), const((D_MODEL, ROUTER_LANES)), const((1, ROUTER_LANES)),
        ],
        out_specs=(row(D_MODEL), row(D_MODEL), row(ROUTER_LANES)),
        compiler_params=pltpu.CompilerParams(
            dimension_semantics=("arbitrary",), vmem_limit_bytes=VMEM_LIMIT),
        name="merge_router",
    )(x2, attn_o, y_lru, proj, proj, b_gates, wa, wl, wo, n2_gain, w_router, b_router)


def _moe_kernel(h_ref, cw_ref, x1_ref, wg_ref, wu_ref, wd_ref, o_ref):
    e = pl.program_id(1)

    @pl.when(e == 0)
    def _init():
        o_ref[...] = x1_ref[...]

    h = h_ref[...]
    a = jnp.dot(h, wg_ref[0], preferred_element_type=F32)
    u = jnp.dot(h, wu_ref[0], preferred_element_type=F32)
    cw = cw_ref[...]
    lane = lax.broadcasted_iota(jnp.int32, cw.shape, 1)
    c = jnp.sum(jnp.where(lane == EXPERT_LANE0 + e, cw, 0.0), axis=-1, keepdims=True)
    hid = a * jax.nn.sigmoid(a) * u * c
    o_ref[...] += jnp.dot(hid.astype(BF16), wd_ref[0], preferred_element_type=F32)


def _moe(h2, cw, x1, wg, wu, wd, tm=512):
    n = h2.shape[0]
    return pl.pallas_call(
        _moe_kernel,
        out_shape=jax.ShapeDtypeStruct((n, D_MODEL), F32),
        grid=(n // tm, N_EXPERTS),
        in_specs=[
            pl.BlockSpec((tm, D_MODEL), lambda i, e: (i, 0)),
            pl.BlockSpec((tm, ROUTER_LANES), lambda i, e: (i, 0)),
            pl.BlockSpec((tm, D_MODEL), lambda i, e: (i, 0)),
            pl.BlockSpec((1, D_MODEL, EXPERT_HIDDEN), lambda i, e: (e, 0, 0)),
            pl.BlockSpec((1, D_MODEL, EXPERT_HIDDEN), lambda i, e: (e, 0, 0)),
            pl.BlockSpec((1, EXPERT_HIDDEN, D_MODEL), lambda i, e: (e, 0, 0)),
        ],
        out_specs=pl.BlockSpec((tm, D_MODEL), lambda i, e: (i, 0)),
        compiler_params=pltpu.CompilerParams(
            dimension_semantics=("arbitrary", "arbitrary"), vmem_limit_bytes=VMEM_LIMIT),
        name="moe",
    )(h2, cw, x1, wg, wu, wd)


def _rope_tables(seq):
    pos = jnp.arange(seq, dtype=F32)
    inv_freq = ROPE_THETA ** (-jnp.arange(0, SUB_HEAD, 2, dtype=F32) / SUB_HEAD)
    ang = pos[:, None] * inv_freq[None, :]
    cos, sin = jnp.cos(ang), jnp.sin(ang)
    cos_full = jnp.concatenate([cos, cos, cos, cos], axis=-1)
    sin_signed = jnp.concatenate([-sin, sin, -sin, sin], axis=-1)
    return cos_full, sin_signed


def kernel(x, norm1_gain, w_in, b_gates, q_norm_gain, k_norm_gain, lambda_q1, lambda_k1, lambda_q2, lambda_k2, attn_subln_gain, w_attn_o, conv_w, conv_b, lru_wa, lru_ba, lru_wi, lru_bi, lru_lambda, w_lru_o, w_out, norm2_gain, w_group_router, b_group_router, w_expert_router, b_expert_router, w_expert_gate, w_expert_up, w_expert_down):
    b, s, d = x.shape
    n = b * s
    depth = w_in.shape[0]
    assert depth == 1 and d == D_MODEL
    cos, sin_signed = _rope_tables(s)
    x2 = x.reshape(n, d)
    l = 0

    proj = _inproj(x2, norm1_gain[l][None, :], w_in[l].astype(BF16))
    proj3 = proj.reshape(b, s, PROJ_WIDTH)

    lam_params = jnp.stack([lambda_q1[l], lambda_k1[l], lambda_q2[l], lambda_k2[l]])
    attn_o = _attention(
        proj3, lam_params,
        jnp.tile(q_norm_gain[l], 2)[None, :], jnp.tile(k_norm_gain[l], 2)[None, :],
        attn_subln_gain[l][None, :], cos, sin_signed)

    w_cat = jnp.concatenate([lru_wa[l, 0], lru_wi[l, 0], lru_wa[l, 1], lru_wi[l, 1]],
                            axis=-1).astype(BF16)
    blk = lambda v: v.reshape(LRU_BLOCKS, 1, LRU_BLOCK_DIM)
    b_cat = jnp.concatenate([blk(lru_ba[l, 0]), blk(lru_bi[l, 0]),
                             blk(lru_ba[l, 1]), blk(lru_bi[l, 1])], axis=-1)
    y_lru = _lru(proj3, conv_w[l], conv_b[l][None, :], w_cat, b_cat, lru_lambda[l])

    pad = ROUTER_LANES - N_GROUPS - N_EXPERTS
    w_router = jnp.concatenate([w_group_router[l], w_expert_router[l],
                                jnp.zeros((d, pad), F32)], axis=-1)
    b_router = jnp.concatenate([b_group_router[l], b_expert_router[l],
                                jnp.zeros((pad,), F32)])[None, :]
    x1, h2, cw = _merge(
        x2, attn_o.reshape(n, ATTN_WIDTH), y_lru.reshape(n, LRU_WIDTH), proj,
        b_gates[l][None, :], w_attn_o[l].astype(BF16), w_lru_o[l].astype(BF16),
        w_out[l].astype(BF16), norm2_gain[l][None, :], w_router, b_router)

    out = _moe(h2, cw, x1, w_expert_gate[l].astype(BF16), w_expert_up[l].astype(BF16),
               w_expert_down[l].astype(BF16))
    return out.reshape(b, s, d)
```

```python
import functools
import math

import jax
import jax.numpy as jnp
from jax import lax
from jax.experimental import pallas as pl
from jax.experimental.pallas import tpu as pltpu

F32 = jnp.float32
BF16 = jnp.bfloat16

D_MODEL = 1024
N_HEADS = 8
SUB_HEAD = 64
V_HEAD = 128
ATTN_WIDTH = 1024
LRU_WIDTH = 1024
LRU_BLOCKS = 8
LRU_BLOCK_DIM = 128
LRU_C = 8.0
PROJ_WIDTH = 7168
N_GROUPS = 4
EXPERTS_PER_GROUP = 4
N_EXPERTS = 16
EXPERT_HIDDEN = 512
ROPE_THETA = 10000.0
RMS_EPS = 1e-6
LAMBDA_INIT = 0.8 - 0.6 * math.exp(-0.3 * 0)

LOG2_E = math.log2(math.e)
LANES = 128
SUBLANES = 8
VMEM_LIMIT = 56 * 1024 * 1024

Q_COL, K_COL, V_COL, LRUX_COL, LRUG_COL = 0, 8, 16, 24, 32
GATE_COL_1024 = 5

ROUTER_LANES = 128
EXPERT_LANE0 = N_GROUPS


def _rms(x, gain):
    ms = jnp.mean(x * x, axis=-1, keepdims=True)
    return x * lax.rsqrt(ms + RMS_EPS) * gain


def _inproj_kernel(x_ref, g_ref, w_ref, o_ref):
    h = _rms(x_ref[...], g_ref[...]).astype(BF16)
    for j in range(PROJ_WIDTH // 1024):
        cols = slice(j * 1024, (j + 1) * 1024)
        o_ref[:, cols] = jnp.dot(h, w_ref[:, cols], preferred_element_type=F32).astype(BF16)


def _inproj(x2, gain, w_bf16, tm=256):
    n = x2.shape[0]
    return pl.pallas_call(
        _inproj_kernel,
        out_shape=jax.ShapeDtypeStruct((n, PROJ_WIDTH), BF16),
        grid=(n // tm,),
        in_specs=[
            pl.BlockSpec((tm, D_MODEL), lambda i: (i, 0)),
            pl.BlockSpec((1, D_MODEL), lambda i: (0, 0)),
            pl.BlockSpec((D_MODEL, PROJ_WIDTH), lambda i: (0, 0)),
        ],
        out_specs=pl.BlockSpec((tm, PROJ_WIDTH), lambda i: (i, 0)),
        compiler_params=pltpu.CompilerParams(
            dimension_semantics=("arbitrary",), vmem_limit_bytes=VMEM_LIMIT),
        name="inproj",
    )(x2, gain, w_bf16)


def _subhead_norm_rope(x, gain, cos, sin_signed):
    r = lax.broadcasted_iota(jnp.int32, (LANES, LANES), 0) // SUB_HEAD
    c = lax.broadcasted_iota(jnp.int32, (LANES, LANES), 1) // SUB_HEAD
    group_mean = jnp.where(r == c, 1.0 / SUB_HEAD, 0.0).astype(F32)
    ms = jnp.dot(x * x, group_mean, preferred_element_type=F32, precision=lax.Precision.HIGHEST)
    xn = x * lax.rsqrt(ms + RMS_EPS) * gain
    lane = lax.broadcasted_iota(jnp.int32, xn.shape, 1)
    first_half = (lane % SUB_HEAD) < (SUB_HEAD // 2)
    partner = jnp.where(first_half,
                        pltpu.roll(xn, LANES - SUB_HEAD // 2, axis=1),
                        pltpu.roll(xn, SUB_HEAD // 2, axis=1))
    return xn * cos + partner * sin_signed


def _attn_kernel(lam_ref, qg_ref, kg_ref, sg_ref, cq_ref, sq_ref, ck_ref, sk_ref,
                 q_ref, k_ref, v_ref, o_ref, kt_ref, *, tq, tk, seq):
    i = pl.program_id(2)
    n_k = seq // tk

    @pl.when(i == 0)
    def _prep_keys():
        for c in range(n_k):
            rows = slice(c * tk, (c + 1) * tk)
            kr = _subhead_norm_rope(k_ref[0, rows, :].astype(F32), kg_ref[...],
                                    ck_ref[rows, :], sk_ref[rows, :])
            kt_ref[:, rows] = kr.T.astype(BF16)

    lp = lam_ref[...]
    lam = (jnp.exp(jnp.sum(lp[0:1] * lp[1:2], axis=-1, keepdims=True))
           - jnp.exp(jnp.sum(lp[2:3] * lp[3:4], axis=-1, keepdims=True)) + LAMBDA_INIT)

    qr = _subhead_norm_rope(q_ref[0].astype(F32), qg_ref[...], cq_ref[...], sq_ref[...])
    qr = qr * (SUB_HEAD ** -0.5 * LOG2_E)
    lane = lax.broadcasted_iota(jnp.int32, qr.shape, 1)
    zero = jnp.zeros_like(qr)
    qq = jnp.concatenate([jnp.where(lane < SUB_HEAD, qr, zero),
                          jnp.where(lane >= SUB_HEAD, qr, zero)], axis=0).astype(BF16)

    s = jnp.dot(qq, kt_ref[...], preferred_element_type=F32)
    m = jnp.max(s, axis=-1, keepdims=True)
    p = jnp.exp2(s - m)
    l = jnp.sum(p, axis=-1, keepdims=True)
    acc = jnp.dot(p.astype(BF16), v_ref[0], preferred_element_type=F32)
    o = acc / l
    o = o[0:tq] - lam * o[tq:2 * tq]
    o = _rms(o, sg_ref[...]) * (1.0 - LAMBDA_INIT)
    o_ref[0] = o.astype(BF16)


def _attention(proj3, lam_params, q_gain, k_gain, subln_gain, cos, sin_signed, tq=128, tk=512):
    b, s, _ = proj3.shape
    kernel = functools.partial(_attn_kernel, tq=tq, tk=tk, seq=s)
    const = lambda shape: pl.BlockSpec(shape, lambda bi, hi, qi: (0, 0))
    return pl.pallas_call(
        kernel,
        out_shape=jax.ShapeDtypeStruct((b, s, ATTN_WIDTH), BF16),
        grid=(b, N_HEADS, s // tq),
        in_specs=[
            const((4, SUB_HEAD)),
            const((1, LANES)), const((1, LANES)), const((1, V_HEAD)),
            pl.BlockSpec((tq, LANES), lambda bi, hi, qi: (qi, 0)),
            pl.BlockSpec((tq, LANES), lambda bi, hi, qi: (qi, 0)),
            const((s, LANES)), const((s, LANES)),
            pl.BlockSpec((1, tq, LANES), lambda bi, hi, qi: (bi, qi, Q_COL + hi)),
            pl.BlockSpec((1, s, LANES), lambda bi, hi, qi: (bi, 0, K_COL + hi)),
            pl.BlockSpec((1, s, V_HEAD), lambda bi, hi, qi: (bi, 0, V_COL + hi)),
        ],
        out_specs=pl.BlockSpec((1, tq, V_HEAD), lambda bi, hi, qi: (bi, qi, hi)),
        scratch_shapes=[
            pltpu.VMEM((LANES, s), BF16),
        ],
        compiler_params=pltpu.CompilerParams(
            dimension_semantics=("arbitrary", "arbitrary", "arbitrary"),
            vmem_limit_bytes=VMEM_LIMIT),
        name="diff_attn",
    )(lam_params, q_gain, k_gain, subln_gain, cos, sin_signed, cos, sin_signed,
      proj3, proj3, proj3)


def _softplus(x):
    return jnp.maximum(x, 0.0) + jnp.log1p(jnp.exp(-jnp.abs(x)))


def _gelu_tanh(x):
    return 0.5 * x * (1.0 + jnp.tanh(math.sqrt(2.0 / math.pi) * (x + 0.044715 * (x * x * x))))


def _lru_kernel(x_ref, g_ref, cw_ref, cb_ref, w_ref, b_ref, lam_ref, o_ref,
                xs_ref, a0_ref, u0_ref, a1_ref, u1_ref, *, seq, tc):
    pad = SUBLANES
    zeros_pad = jnp.zeros((pad, LANES), F32)
    xs_ref[0:pad, :] = zeros_pad
    xs_ref[pad + seq:pad + seq + pad, :] = zeros_pad
    xs_ref[pad:pad + seq, :] = x_ref[0].astype(F32)

    decay = -LRU_C * _softplus(-lam_ref[...])
    cw = cw_ref[...]
    cb = cb_ref[...]
    bias = b_ref[0]

    for c in range(seq // tc):
        base = pad + c * tc
        xr = cb + sum(cw[j:j + 1] * xs_ref[base + j - 1:base + j - 1 + tc, :] for j in range(4))
        z = jnp.dot(xr.astype(BF16), w_ref[0], preferred_element_type=F32) + bias
        rows = slice(c * tc, (c + 1) * tc)
        for d, (a_ref, u_ref) in enumerate(((a0_ref, u0_ref), (a1_ref, u1_ref))):
            r = jax.nn.sigmoid(z[:, (2 * d) * LANES:(2 * d + 1) * LANES])
            gate_i = jax.nn.sigmoid(z[:, (2 * d + 1) * LANES:(2 * d + 2) * LANES])
            a = jnp.exp(decay[d:d + 1] * r)
            a_ref[rows, :] = a
            u_ref[rows, :] = jnp.sqrt(1.0 - a * a) * (gate_i * xr)

    row = lax.broadcasted_iota(jnp.int32, (SUBLANES, LANES), 0)
    n_grp = seq // SUBLANES

    def step(t, carry):
        hf, hb = carry
        of = pl.multiple_of(t * SUBLANES, SUBLANES)
        a = a0_ref[pl.ds(of, SUBLANES), :]
        u = u0_ref[pl.ds(of, SUBLANES), :]
        for d in (1, 2, 4):
            valid = row >= d
            u = jnp.where(valid, a * pltpu.roll(u, d, axis=0) + u, u)
            a = jnp.where(valid, a * pltpu.roll(a, d, axis=0), a)
        h = a * hf + u
        u0_ref[pl.ds(of, SUBLANES), :] = h
        hf = jnp.broadcast_to(h[SUBLANES - 1:SUBLANES, :], h.shape)
        ob = pl.multiple_of((n_grp - 1 - t) * SUBLANES, SUBLANES)
        a = a1_ref[pl.ds(ob, SUBLANES), :]
        u = u1_ref[pl.ds(ob, SUBLANES), :]
        for d in (1, 2, 4):
            valid = row < SUBLANES - d
            u = jnp.where(valid, a * pltpu.roll(u, SUBLANES - d, axis=0) + u, u)
            a = jnp.where(valid, a * pltpu.roll(a, SUBLANES - d, axis=0), a)
        h = a * hb + u
        u1_ref[pl.ds(ob, SUBLANES), :] = h
        hb = jnp.broadcast_to(h[0:1, :], h.shape)
        return hf, hb

    zero = jnp.zeros((SUBLANES, LANES), F32)
    lax.fori_loop(0, n_grp, step, (zero, zero), unroll=2)

    for c in range(seq // tc):
        rows = slice(c * tc, (c + 1) * tc)
        y = (u0_ref[rows, :] + u1_ref[rows, :]) * _gelu_tanh(g_ref[0, rows, :].astype(F32))
        o_ref[0, rows, :] = y.astype(BF16)


def _lru(proj3, conv_w, conv_b, w_cat, b_cat, lam, tc=512):
    b, s, _ = proj3.shape
    kernel = functools.partial(_lru_kernel, seq=s, tc=tc)
    return pl.pallas_call(
        kernel,
        out_shape=jax.ShapeDtypeStruct((b, s, LRU_WIDTH), BF16),
        grid=(b, LRU_BLOCKS),
        in_specs=[
            pl.BlockSpec((1, s, LANES), lambda bi, ni: (bi, 0, LRUX_COL + ni)),
            pl.BlockSpec((1, s, LANES), lambda bi, ni: (bi, 0, LRUG_COL + ni)),
            pl.BlockSpec((4, LANES), lambda bi, ni: (0, ni)),
            pl.BlockSpec((1, LANES), lambda bi, ni: (0, ni)),
            pl.BlockSpec((1, LRU_BLOCK_DIM, 4 * LRU_BLOCK_DIM), lambda bi, ni: (ni, 0, 0)),
            pl.BlockSpec((1, 1, 4 * LRU_BLOCK_DIM), lambda bi, ni: (ni, 0, 0)),
            pl.BlockSpec((2, LANES), lambda bi, ni: (0, ni)),
        ],
        out_specs=pl.BlockSpec((1, s, LANES), lambda bi, ni: (bi, 0, ni)),
        scratch_shapes=[pltpu.VMEM((s + 2 * SUBLANES, LANES), F32)]
        + [pltpu.VMEM((s, LANES), F32)] * 4,
        compiler_params=pltpu.CompilerParams(
            dimension_semantics=("arbitrary", "arbitrary"), vmem_limit_bytes=VMEM_LIMIT),
        name="rglru",
    )(proj3, proj3, conv_w, conv_b, w_cat, b_cat, lam)


def _merge_kernel(x_ref, ao_ref, yl_ref, ga_ref, gl_ref, bg_ref, wa_ref, wl_ref, wo_ref,
                  n2_ref, wr_ref, br_ref, x1_ref, h2_ref, cw_ref):
    attn_d = jnp.dot(ao_ref[...], wa_ref[...], preferred_element_type=F32)
    lru_d = jnp.dot(yl_ref[...], wl_ref[...], preferred_element_type=F32)
    bg = bg_ref[...]
    g_attn = jax.nn.sigmoid(ga_ref[...].astype(F32) + bg[:, 0:D_MODEL])
    g_lru = jax.nn.sigmoid(gl_ref[...].astype(F32) + bg[:, D_MODEL:2 * D_MODEL])
    merged = g_attn * attn_d + g_lru * lru_d
    x1 = x_ref[...] + jnp.dot(merged.astype(BF16), wo_ref[...], preferred_element_type=F32)
    x1_ref[...] = x1
    h2 = _rms(x1, n2_ref[...])
    h2_ref[...] = h2.astype(BF16)

    logits = jnp.dot(h2, wr_ref[...], preferred_element_type=F32,
                     precision=lax.Precision.HIGHEST) + br_ref[...]
    lane = lax.broadcasted_iota(jnp.int32, logits.shape, 1)
    neg = jnp.full_like(logits, -jnp.inf)
    big = jnp.full_like(lane, ROUTER_LANES)

    def masked_max(mask):
        return jnp.max(jnp.where(mask, logits, neg), axis=-1, keepdims=True)

    def first_lane(mask, value):
        return jnp.min(jnp.where(mask & (logits == value), lane, big), axis=-1, keepdims=True)

    g_mask = lane < N_GROUPS
    g_max = masked_max(g_mask)
    g_sel = first_lane(g_mask, g_max)
    g_w = 1.0 / jnp.sum(jnp.where(g_mask, jnp.exp(logits - g_max), 0.0), axis=-1, keepdims=True)
    e_lo = EXPERT_LANE0 + g_sel * EXPERTS_PER_GROUP
    e_mask = (lane >= e_lo) & (lane < e_lo + EXPERTS_PER_GROUP)
    v1 = masked_max(e_mask)
    i1 = first_lane(e_mask, v1)
    e_mask2 = e_mask & (lane != i1)
    v2 = masked_max(e_mask2)
    i2 = first_lane(e_mask2, v2)
    t = jnp.exp(v2 - v1)
    w1 = g_w / (1.0 + t)
    w2 = g_w * t / (1.0 + t)
    cw_ref[...] = jnp.where(lane == i1, w1, 0.0) + jnp.where(lane == i2, w2, 0.0)


def _merge(x2, attn_o, y_lru, proj, b_gates, wa, wl, wo, n2_gain, w_router, b_router, tm=256):
    n = x2.shape[0]
    row = lambda cols, col_blk=0: pl.BlockSpec((tm, cols), lambda i: (i, col_blk))
    const = lambda shape: pl.BlockSpec(shape, lambda i: (0, 0))
    return pl.pallas_call(
        _merge_kernel,
        out_shape=(jax.ShapeDtypeStruct((n, D_MODEL), F32),
                   jax.ShapeDtypeStruct((n, D_MODEL), BF16),
                   jax.ShapeDtypeStruct((n, ROUTER_LANES), F32)),
        grid=(n // tm,),
        in_specs=[
            row(D_MODEL), row(ATTN_WIDTH), row(LRU_WIDTH),
            row(D_MODEL, GATE_COL_1024), row(D_MODEL, GATE_COL_1024 + 1),
            const((1, 2 * D_MODEL)),
            const((ATTN_WIDTH, D_MODEL)), const((LRU_WIDTH, D_MODEL)), const((D_MODEL, D_MODEL)),
            const((1, D_MODEL)), const((D_MODEL, ROUTER_LANES)), const((1, ROUTER_LANES)),
        ],
        out_specs=(row(D_MODEL), row(D_MODEL), row(ROUTER_LANES)),
        compiler_params=pltpu.CompilerParams(
            dimension_semantics=("arbitrary",), vmem_limit_bytes=VMEM_LIMIT),
        name="merge_router",
    )(x2, attn_o, y_lru, proj, proj, b_gates, wa, wl, wo, n2_gain, w_router, b_router)


def _moe_kernel(h_ref, cw_ref, x1_ref, wg_ref, wu_ref, wd_ref, o_ref):
    e = pl.program_id(1)

    @pl.when(e == 0)
    def _init():
        o_ref[...] = x1_ref[...]

    h = h_ref[...]
    a = jnp.dot(h, wg_ref[0], preferred_element_type=F32)
    u = jnp.dot(h, wu_ref[0], preferred_element_type=F32)
    cw = cw_ref[...]
    lane = lax.broadcasted_iota(jnp.int32, cw.shape, 1)
    c = jnp.sum(jnp.where(lane == EXPERT_LANE0 + e, cw, 0.0), axis=-1, keepdims=True)
    hid = a * jax.nn.sigmoid(a) * u * c
    o_ref[...] += jnp.dot(hid.astype(BF16), wd_ref[0], preferred_element_type=F32)


def _moe(h2, cw, x1, wg, wu, wd, tm=512):
    n = h2.shape[0]
    return pl.pallas_call(
        _moe_kernel,
        out_shape=jax.ShapeDtypeStruct((n, D_MODEL), F32),
        grid=(n // tm, N_EXPERTS),
        in_specs=[
            pl.BlockSpec((tm, D_MODEL), lambda i, e: (i, 0)),
            pl.BlockSpec((tm, ROUTER_LANES), lambda i, e: (i, 0)),
            pl.BlockSpec((tm, D_MODEL), lambda i, e: (i, 0)),
            pl.BlockSpec((1, D_MODEL, EXPERT_HIDDEN), lambda i, e: (e, 0, 0)),
            pl.BlockSpec((1, D_MODEL, EXPERT_HIDDEN), lambda i, e: (e, 0, 0)),
            pl.BlockSpec((1, EXPERT_HIDDEN, D_MODEL), lambda i, e: (e, 0, 0)),
        ],
        out_specs=pl.BlockSpec((tm, D_MODEL), lambda i, e: (i, 0)),
        compiler_params=pltpu.CompilerParams(
            dimension_semantics=("arbitrary", "arbitrary"), vmem_limit_bytes=VMEM_LIMIT),
        name="moe",
    )(h2, cw, x1, wg, wu, wd)


def _rope_tables(seq):
    pos = jnp.arange(seq, dtype=F32)
    inv_freq = ROPE_THETA ** (-jnp.arange(0, SUB_HEAD, 2, dtype=F32) / SUB_HEAD)
    ang = pos[:, None] * inv_freq[None, :]
    cos, sin = jnp.cos(ang), jnp.sin(ang)
    cos_full = jnp.concatenate([cos, cos, cos, cos], axis=-1)
    sin_signed = jnp.concatenate([-sin, sin, -sin, sin], axis=-1)
    return cos_full, sin_signed


def kernel(x, norm1_gain, w_in, b_gates, q_norm_gain, k_norm_gain, lambda_q1, lambda_k1, lambda_q2, lambda_k2, attn_subln_gain, w_attn_o, conv_w, conv_b, lru_wa, lru_ba, lru_wi, lru_bi, lru_lambda, w_lru_o, w_out, norm2_gain, w_group_router, b_group_router, w_expert_router, b_expert_router, w_expert_gate, w_expert_up, w_expert_down):
    b, s, d = x.shape
    n = b * s
    depth = w_in.shape[0]
    assert depth == 1 and d == D_MODEL
    cos, sin_signed = _rope_tables(s)
    x2 = x.reshape(n, d)
    l = 0

    proj = _inproj(x2, norm1_gain[l][None, :], w_in[l].astype(BF16))
    proj3 = proj.reshape(b, s, PROJ_WIDTH)

    lam_params = jnp.stack([lambda_q1[l], lambda_k1[l], lambda_q2[l], lambda_k2[l]])
    attn_o = _attention(
        proj3, lam_params,
        jnp.tile(q_norm_gain[l], 2)[None, :], jnp.tile(k_norm_gain[l], 2)[None, :],
        attn_subln_gain[l][None, :], cos, sin_signed)

    w_cat = jnp.concatenate([lru_wa[l, 0], lru_wi[l, 0], lru_wa[l, 1], lru_wi[l, 1]],
                            axis=-1).astype(BF16)
    blk = lambda v: v.reshape(LRU_BLOCKS, 1, LRU_BLOCK_DIM)
    b_cat = jnp.concatenate([blk(lru_ba[l, 0]), blk(lru_bi[l, 0]),
                             blk(lru_ba[l, 1]), blk(lru_bi[l, 1])], axis=-1)
    y_lru = _lru(proj3, conv_w[l], conv_b[l][None, :], w_cat, b_cat, lru_lambda[l])

    pad = ROUTER_LANES - N_GROUPS - N_EXPERTS
    w_router = jnp.concatenate([w_group_router[l], w_expert_router[l],
                                jnp.zeros((d, pad), F32)], axis=-1)
    b_router = jnp.concatenate([b_group_router[l], b_expert_router[l],
                                jnp.zeros((pad,), F32)])[None, :]
    x1, h2, cw = _merge(
        x2, attn_o.reshape(n, ATTN_WIDTH), y_lru.reshape(n, LRU_WIDTH), proj,
        b_gates[l][None, :], w_attn_o[l].astype(BF16), w_lru_o[l].astype(BF16),
        w_out[l].astype(BF16), norm2_gain[l][None, :], w_router, b_router)

    out = _moe(h2, cw, x1, w_expert_gate[l].astype(BF16), w_expert_up[l].astype(BF16),
               w_expert_down[l].astype(BF16))
    return out.reshape(b, s, d)
```

```python
import functools
import math

import jax
import jax.numpy as jnp
from jax import lax
from jax.experimental import pallas as pl
from jax.experimental.pallas import tpu as pltpu

F32 = jnp.float32
BF16 = jnp.bfloat16

D_MODEL = 1024
N_HEADS = 8
SUB_HEAD = 64
V_HEAD = 128
ATTN_WIDTH = 1024
LRU_WIDTH = 1024
LRU_BLOCKS = 8
LRU_BLOCK_DIM = 128
LRU_C = 8.0
PROJ_WIDTH = 7168
N_GROUPS = 4
EXPERTS_PER_GROUP = 4
N_EXPERTS = 16
EXPERT_HIDDEN = 512
ROPE_THETA = 10000.0
RMS_EPS = 1e-6
LAMBDA_INIT = 0.8 - 0.6 * math.exp(-0.3 * 0)

LOG2_E = math.log2(math.e)
LANES = 128
SUBLANES = 8
VMEM_LIMIT = 56 * 1024 * 1024

Q_COL, K_COL, V_COL, LRUX_COL, LRUG_COL = 0, 8, 16, 24, 32
GATE_COL_1024 = 5

ROUTER_LANES = 128
EXPERT_LANE0 = N_GROUPS


def _rms(x, gain):
    ms = jnp.mean(x * x, axis=-1, keepdims=True)
    return x * lax.rsqrt(ms + RMS_EPS) * gain


def _inproj_kernel(x_ref, g_ref, w_ref, o_ref):
    h = _rms(x_ref[...], g_ref[...]).astype(BF16)
    for j in range(PROJ_WIDTH // 1024):
        cols = slice(j * 1024, (j + 1) * 1024)
        o_ref[:, cols] = jnp.dot(h, w_ref[:, cols], preferred_element_type=F32).astype(BF16)


def _inproj(x2, gain, w_bf16, tm=256):
    n = x2.shape[0]
    return pl.pallas_call(
        _inproj_kernel,
        out_shape=jax.ShapeDtypeStruct((n, PROJ_WIDTH), BF16),
        grid=(n // tm,),
        in_specs=[
            pl.BlockSpec((tm, D_MODEL), lambda i: (i, 0)),
            pl.BlockSpec((1, D_MODEL), lambda i: (0, 0)),
            pl.BlockSpec((D_MODEL, PROJ_WIDTH), lambda i: (0, 0)),
        ],
        out_specs=pl.BlockSpec((tm, PROJ_WIDTH), lambda i: (i, 0)),
        compiler_params=pltpu.CompilerParams(
            dimension_semantics=("arbitrary",), vmem_limit_bytes=VMEM_LIMIT),
        name="inproj",
    )(x2, gain, w_bf16)


def _subhead_norm_rope(x, gain, cos, sin_signed):
    r = lax.broadcasted_iota(jnp.int32, (LANES, LANES), 0) // SUB_HEAD
    c = lax.broadcasted_iota(jnp.int32, (LANES, LANES), 1) // SUB_HEAD
    group_mean = jnp.where(r == c, 1.0 / SUB_HEAD, 0.0).astype(F32)
    ms = jnp.dot(x * x, group_mean, preferred_element_type=F32, precision=lax.Precision.HIGHEST)
    xn = x * lax.rsqrt(ms + RMS_EPS) * gain
    lane = lax.broadcasted_iota(jnp.int32, xn.shape, 1)
    first_half = (lane % SUB_HEAD) < (SUB_HEAD // 2)
    partner = jnp.where(first_half,
                        pltpu.roll(xn, LANES - SUB_HEAD // 2, axis=1),
                        pltpu.roll(xn, SUB_HEAD // 2, axis=1))
    return xn * cos + partner * sin_signed


def _qkv_prep_kernel(qg_ref, kg_ref, cos_ref, sin_ref, q_ref, k_ref, v_ref, qo_ref, kt_ref, vo_ref):
    cos, sin_signed = cos_ref[...], sin_ref[...]
    qr = _subhead_norm_rope(q_ref[0].astype(F32), qg_ref[...], cos, sin_signed)
    qo_ref[0, 0] = (qr * (SUB_HEAD ** -0.5 * LOG2_E)).astype(BF16)
    kr = _subhead_norm_rope(k_ref[0].astype(F32), kg_ref[...], cos, sin_signed)
    kt_ref[0, 0] = kr.T.astype(BF16)
    vo_ref[0, 0] = v_ref[0]


def _qkv_prep(proj3, q_gain, k_gain, cos, sin_signed, tp=512):
    b, s, _ = proj3.shape
    const = lambda shape: pl.BlockSpec(shape, lambda bi, hi, si: (0, 0))
    table = pl.BlockSpec((tp, LANES), lambda bi, hi, si: (si, 0))
    col = lambda base: pl.BlockSpec((1, tp, LANES), lambda bi, hi, si: (bi, si, base + hi))
    rows_out = pl.BlockSpec((1, 1, tp, LANES), lambda bi, hi, si: (bi, hi, si, 0))
    return pl.pallas_call(
        _qkv_prep_kernel,
        out_shape=(jax.ShapeDtypeStruct((b, N_HEADS, s, LANES), BF16),
                   jax.ShapeDtypeStruct((b, N_HEADS, LANES, s), BF16),
                   jax.ShapeDtypeStruct((b, N_HEADS, s, V_HEAD), BF16)),
        grid=(b, N_HEADS, s // tp),
        in_specs=[const((1, LANES)), const((1, LANES)), table, table,
                  col(Q_COL), col(K_COL), col(V_COL)],
        out_specs=(rows_out,
                   pl.BlockSpec((1, 1, LANES, tp), lambda bi, hi, si: (bi, hi, 0, si)),
                   rows_out),
        compiler_params=pltpu.CompilerParams(
            dimension_semantics=("arbitrary", "arbitrary", "arbitrary"),
            vmem_limit_bytes=VMEM_LIMIT),
        name="qkv_prep",
    )(q_gain, k_gain, cos, sin_signed, proj3, proj3, proj3)


def _attn_kernel(lam_ref, sg_ref, q_ref, kt_ref, v_ref, o_ref, *, tq):
    lp = lam_ref[...]
    lam = (jnp.exp(jnp.sum(lp[0:1] * lp[1:2], axis=-1, keepdims=True))
           - jnp.exp(jnp.sum(lp[2:3] * lp[3:4], axis=-1, keepdims=True)) + LAMBDA_INIT)

    q = q_ref[0, 0]
    lane = lax.broadcasted_iota(jnp.int32, q.shape, 1)
    zero = jnp.zeros_like(q)
    qq = jnp.concatenate([jnp.where(lane < SUB_HEAD, q, zero),
                          jnp.where(lane >= SUB_HEAD, q, zero)], axis=0)

    s = jnp.dot(qq, kt_ref[0, 0], preferred_element_type=F32)
    m = jnp.max(s, axis=-1, keepdims=True)
    p = jnp.exp2(s - m)
    l = jnp.sum(p, axis=-1, keepdims=True)
    acc = jnp.dot(p.astype(BF16), v_ref[0, 0], preferred_element_type=F32)
    o = acc / l
    o = o[0:tq] - lam * o[tq:2 * tq]
    o = _rms(o, sg_ref[...]) * (1.0 - LAMBDA_INIT)
    o_ref[0, 0] = o.astype(BF16)


def _attention(q_hm, kt_hm, v_hm, lam_params, subln_gain, tq=128, name="diff_attn"):
    b, h, s, _ = q_hm.shape
    const = lambda shape: pl.BlockSpec(shape, lambda bi, hi, qi: (0, 0))
    return pl.pallas_call(
        functools.partial(_attn_kernel, tq=tq),
        out_shape=jax.ShapeDtypeStruct((b, h, s, V_HEAD), BF16),
        grid=(b, h, s // tq),
        in_specs=[
            const((4, SUB_HEAD)), const((1, V_HEAD)),
            pl.BlockSpec((1, 1, tq, LANES), lambda bi, hi, qi: (bi, hi, qi, 0)),
            pl.BlockSpec((1, 1, LANES, s), lambda bi, hi, qi: (bi, hi, 0, 0)),
            pl.BlockSpec((1, 1, s, V_HEAD), lambda bi, hi, qi: (bi, hi, 0, 0)),
        ],
        out_specs=pl.BlockSpec((1, 1, tq, V_HEAD), lambda bi, hi, qi: (bi, hi, qi, 0)),
        compiler_params=pltpu.CompilerParams(
            dimension_semantics=("arbitrary", "arbitrary", "arbitrary"),
            vmem_limit_bytes=VMEM_LIMIT),
        name=name,
    )(lam_params, subln_gain, q_hm, kt_hm, v_hm)


def _softplus(x):
    return jnp.maximum(x, 0.0) + jnp.log1p(jnp.exp(-jnp.abs(x)))


def _gelu_tanh(x):
    return 0.5 * x * (1.0 + jnp.tanh(math.sqrt(2.0 / math.pi) * (x + 0.044715 * (x * x * x))))


def _lru_kernel(x_ref, g_ref, cw_ref, cb_ref, w_ref, b_ref, lam_ref, o_ref,
                xs_ref, a0_ref, u0_ref, a1_ref, u1_ref, *, seq, tc):
    pad = SUBLANES
    zeros_pad = jnp.zeros((pad, LANES), F32)
    xs_ref[0:pad, :] = zeros_pad
    xs_ref[pad + seq:pad + seq + pad, :] = zeros_pad
    xs_ref[pad:pad + seq, :] = x_ref[0].astype(F32)

    decay = -LRU_C * _softplus(-lam_ref[...])
    cw = cw_ref[...]
    cb = cb_ref[...]
    bias = b_ref[0]

    for c in range(seq // tc):
        base = pad + c * tc
        xr = cb + sum(cw[j:j + 1] * xs_ref[base + j - 1:base + j - 1 + tc, :] for j in range(4))
        z = jnp.dot(xr.astype(BF16), w_ref[0], preferred_element_type=F32) + bias
        rows = slice(c * tc, (c + 1) * tc)
        for d, (a_ref, u_ref) in enumerate(((a0_ref, u0_ref), (a1_ref, u1_ref))):
            r = jax.nn.sigmoid(z[:, (2 * d) * LANES:(2 * d + 1) * LANES])
            gate_i = jax.nn.sigmoid(z[:, (2 * d + 1) * LANES:(2 * d + 2) * LANES])
            a = jnp.exp(decay[d:d + 1] * r)
            a_ref[rows, :] = a
            u_ref[rows, :] = jnp.sqrt(1.0 - a * a) * (gate_i * xr)

    row = lax.broadcasted_iota(jnp.int32, (SUBLANES, LANES), 0)
    n_grp = seq // SUBLANES

    def step(t, carry):
        hf, hb = carry
        of = pl.multiple_of(t * SUBLANES, SUBLANES)
        a = a0_ref[pl.ds(of, SUBLANES), :]
        u = u0_ref[pl.ds(of, SUBLANES), :]
        for d in (1, 2, 4):
            valid = row >= d
            u = jnp.where(valid, a * pltpu.roll(u, d, axis=0) + u, u)
            a = jnp.where(valid, a * pltpu.roll(a, d, axis=0), a)
        h = a * hf + u
        u0_ref[pl.ds(of, SUBLANES), :] = h
        hf = jnp.broadcast_to(h[SUBLANES - 1:SUBLANES, :], h.shape)
        ob = pl.multiple_of((n_grp - 1 - t) * SUBLANES, SUBLANES)
        a = a1_ref[pl.ds(ob, SUBLANES), :]
        u = u1_ref[pl.ds(ob, SUBLANES), :]
        for d in (1, 2, 4):
            valid = row < SUBLANES - d
            u = jnp.where(valid, a * pltpu.roll(u, SUBLANES - d, axis=0) + u, u)
            a = jnp.where(valid, a * pltpu.roll(a, SUBLANES - d, axis=0), a)
        h = a * hb + u
        u1_ref[pl.ds(ob, SUBLANES), :] = h
        hb = jnp.broadcast_to(h[0:1, :], h.shape)
        return hf, hb

    zero = jnp.zeros((SUBLANES, LANES), F32)
    lax.fori_loop(0, n_grp, step, (zero, zero), unroll=2)

    for c in range(seq // tc):
        rows = slice(c * tc, (c + 1) * tc)
        y = (u0_ref[rows, :] + u1_ref[rows, :]) * _gelu_tanh(g_ref[0, rows, :].astype(F32))
        o_ref[0, rows, :] = y.astype(BF16)


def _lru(proj3, conv_w, conv_b, w_cat, b_cat, lam, tc=512):
    b, s, _ = proj3.shape
    kernel = functools.partial(_lru_kernel, seq=s, tc=tc)
    return pl.pallas_call(
        kernel,
        out_shape=jax.ShapeDtypeStruct((b, s, LRU_WIDTH), BF16),
        grid=(b, LRU_BLOCKS),
        in_specs=[
            pl.BlockSpec((1, s, LANES), lambda bi, ni: (bi, 0, LRUX_COL + ni)),
            pl.BlockSpec((1, s, LANES), lambda bi, ni: (bi, 0, LRUG_COL + ni)),
            pl.BlockSpec((4, LANES), lambda bi, ni: (0, ni)),
            pl.BlockSpec((1, LANES), lambda bi, ni: (0, ni)),
            pl.BlockSpec((1, LRU_BLOCK_DIM, 4 * LRU_BLOCK_DIM), lambda bi, ni: (ni, 0, 0)),
            pl.BlockSpec((1, 1, 4 * LRU_BLOCK_DIM), lambda bi, ni: (ni, 0, 0)),
            pl.BlockSpec((2, LANES), lambda bi, ni: (0, ni)),
        ],
        out_specs=pl.BlockSpec((1, s, LANES), lambda bi, ni: (bi, 0, ni)),
        scratch_shapes=[pltpu.VMEM((s + 2 * SUBLANES, LANES), F32)]
        + [pltpu.VMEM((s, LANES), F32)] * 4,
        compiler_params=pltpu.CompilerParams(
            dimension_semantics=("arbitrary", "arbitrary"), vmem_limit_bytes=VMEM_LIMIT),
        name="rglru",
    )(proj3, proj3, conv_w, conv_b, w_cat, b_cat, lam)


def _merge_kernel(x_ref, ao_ref, yl_ref, ga_ref, gl_ref, bg_ref, wa_ref, wl_ref, wo_ref,
                  n2_ref, wr_ref, br_ref, x1_ref, h2_ref, cw_ref):
    attn_o = jnp.concatenate([ao_ref[0, h] for h in range(N_HEADS)], axis=1)
    attn_d = jnp.dot(attn_o, wa_ref[...], preferred_element_type=F32)
    lru_d = jnp.dot(yl_ref[...], wl_ref[...], preferred_element_type=F32)
    bg = bg_ref[...]
    g_attn = jax.nn.sigmoid(ga_ref[...].astype(F32) + bg[:, 0:D_MODEL])
    g_lru = jax.nn.sigmoid(gl_ref[...].astype(F32) + bg[:, D_MODEL:2 * D_MODEL])
    merged = g_attn * attn_d + g_lru * lru_d
    x1 = x_ref[...] + jnp.dot(merged.astype(BF16), wo_ref[...], preferred_element_type=F32)
    x1_ref[...] = x1
    h2 = _rms(x1, n2_ref[...])
    h2_ref[...] = h2.astype(BF16)

    logits = jnp.dot(h2, wr_ref[...], preferred_element_type=F32,
                     precision=lax.Precision.HIGHEST) + br_ref[...]
    lane = lax.broadcasted_iota(jnp.int32, logits.shape, 1)
    neg = jnp.full_like(logits, -jnp.inf)
    big = jnp.full_like(lane, ROUTER_LANES)

    def masked_max(mask):
        return jnp.max(jnp.where(mask, logits, neg), axis=-1, keepdims=True)

    def first_lane(mask, value):
        return jnp.min(jnp.where(mask & (logits == value), lane, big), axis=-1, keepdims=True)

    g_mask = lane < N_GROUPS
    g_max = masked_max(g_mask)
    g_sel = first_lane(g_mask, g_max)
    g_w = 1.0 / jnp.sum(jnp.where(g_mask, jnp.exp(logits - g_max), 0.0), axis=-1, keepdims=True)
    e_lo = EXPERT_LANE0 + g_sel * EXPERTS_PER_GROUP
    e_mask = (lane >= e_lo) & (lane < e_lo + EXPERTS_PER_GROUP)
    v1 = masked_max(e_mask)
    i1 = first_lane(e_mask, v1)
    e_mask2 = e_mask & (lane != i1)
    v2 = masked_max(e_mask2)
    i2 = first_lane(e_mask2, v2)
    t = jnp.exp(v2 - v1)
    w1 = g_w / (1.0 + t)
    w2 = g_w * t / (1.0 + t)
    cw_ref[...] = jnp.where(lane == i1, w1, 0.0) + jnp.where(lane == i2, w2, 0.0)


def _merge(x2, attn_o, y_lru, proj, b_gates, wa, wl, wo, n2_gain, w_router, b_router, tm=256):
    n = x2.shape[0]
    blocks_per_seq = attn_o.shape[2] // tm
    row = lambda cols, col_blk=0: pl.BlockSpec((tm, cols), lambda i: (i, col_blk))
    const = lambda shape: pl.BlockSpec(shape, lambda i: (0, 0))
    heads = pl.BlockSpec((1, N_HEADS, tm, V_HEAD),
                         lambda i: (i // blocks_per_seq, 0, i % blocks_per_seq, 0))
    return pl.pallas_call(
        _merge_kernel,
        out_shape=(jax.ShapeDtypeStruct((n, D_MODEL), F32),
                   jax.ShapeDtypeStruct((n, D_MODEL), BF16),
                   jax.ShapeDtypeStruct((n, ROUTER_LANES), F32)),
        grid=(n // tm,),
        in_specs=[
            row(D_MODEL), heads, row(LRU_WIDTH),
            row(D_MODEL, GATE_COL_1024), row(D_MODEL, GATE_COL_1024 + 1),
            const((1, 2 * D_MODEL)),
            const((ATTN_WIDTH, D_MODEL)), const((LRU_WIDTH, D_MODEL)), const((D_MODEL, D_MODEL)),
            const((1, D_MODEL)), const((D_MODEL, ROUTER_LANES)), const((1, ROUTER_LANES)),
        ],
        out_specs=(row(D_MODEL), row(D_MODEL), row(ROUTER_LANES)),
        compiler_params=pltpu.CompilerParams(
            dimension_semantics=("arbitrary",), vmem_limit_bytes=VMEM_LIMIT),
        name="merge_router",
    )(x2, attn_o, y_lru, proj, proj, b_gates, wa, wl, wo, n2_gain, w_router, b_router)


def _moe_kernel(h_ref, cw_ref, x1_ref, wg_ref, wu_ref, wd_ref, o_ref):
    e = pl.program_id(1)

    @pl.when(e == 0)
    def _init():
        o_ref[...] = x1_ref[...]

    h = h_ref[...]
    a = jnp.dot(h, wg_ref[0], preferred_element_type=F32)
    u = jnp.dot(h, wu_ref[0], preferred_element_type=F32)
    cw = cw_ref[...]
    lane = lax.broadcasted_iota(jnp.int32, cw.shape, 1)
    c = jnp.sum(jnp.where(lane == EXPERT_LANE0 + e, cw, 0.0), axis=-1, keepdims=True)
    hid = a * jax.nn.sigmoid(a) * u * c
    o_ref[...] += jnp.dot(hid.astype(BF16), wd_ref[0], preferred_element_type=F32)


def _moe(h2, cw, x1, wg, wu, wd, tm=512):
    n = h2.shape[0]
    return pl.pallas_call(
        _moe_kernel,
        out_shape=jax.ShapeDtypeStruct((n, D_MODEL), F32),
        grid=(n // tm, N_EXPERTS),
        in_specs=[
            pl.BlockSpec((tm, D_MODEL), lambda i, e: (i, 0)),
            pl.BlockSpec((tm, ROUTER_LANES), lambda i, e: (i, 0)),
            pl.BlockSpec((tm, D_MODEL), lambda i, e: (i, 0)),
            pl.BlockSpec((1, D_MODEL, EXPERT_HIDDEN), lambda i, e: (e, 0, 0)),
            pl.BlockSpec((1, D_MODEL, EXPERT_HIDDEN), lambda i, e: (e, 0, 0)),
            pl.BlockSpec((1, EXPERT_HIDDEN, D_MODEL), lambda i, e: (e, 0, 0)),
        ],
        out_specs=pl.BlockSpec((tm, D_MODEL), lambda i, e: (i, 0)),
        compiler_params=pltpu.CompilerParams(
            dimension_semantics=("arbitrary", "arbitrary"), vmem_limit_bytes=VMEM_LIMIT),
        name="moe",
    )(h2, cw, x1, wg, wu, wd)


def _rope_tables(seq):
    pos = jnp.arange(seq, dtype=F32)
    inv_freq = ROPE_THETA ** (-jnp.arange(0, SUB_HEAD, 2, dtype=F32) / SUB_HEAD)
    ang = pos[:, None] * inv_freq[None, :]
    cos, sin = jnp.cos(ang), jnp.sin(ang)
    cos_full = jnp.concatenate([cos, cos, cos, cos], axis=-1)
    sin_signed = jnp.concatenate([-sin, sin, -sin, sin], axis=-1)
    return cos_full, sin_signed


def kernel(x, norm1_gain, w_in, b_gates, q_norm_gain, k_norm_gain, lambda_q1, lambda_k1, lambda_q2, lambda_k2, attn_subln_gain, w_attn_o, conv_w, conv_b, lru_wa, lru_ba, lru_wi, lru_bi, lru_lambda, w_lru_o, w_out, norm2_gain, w_group_router, b_group_router, w_expert_router, b_expert_router, w_expert_gate, w_expert_up, w_expert_down):
    b, s, d = x.shape
    n = b * s
    depth = w_in.shape[0]
    assert depth == 1 and d == D_MODEL
    cos, sin_signed = _rope_tables(s)
    x2 = x.reshape(n, d)
    l = 0

    proj = _inproj(x2, norm1_gain[l][None, :], w_in[l].astype(BF16))
    proj3 = proj.reshape(b, s, PROJ_WIDTH)

    lam_params = jnp.stack([lambda_q1[l], lambda_k1[l], lambda_q2[l], lambda_k2[l]])
    q_hm, kt_hm, v_hm = _qkv_prep(
        proj3, jnp.tile(q_norm_gain[l], 2)[None, :], jnp.tile(k_norm_gain[l], 2)[None, :],
        cos, sin_signed)
    attn_o = _attention(q_hm, kt_hm, v_hm, lam_params, attn_subln_gain[l][None, :])
    probe = _attention(q_hm, kt_hm, v_hm, lam_params, attn_subln_gain[l][None, :],
                       tq=256, name="diff_attn_tq256")
    attn_o = attn_o + probe * jnp.zeros((), BF16)

    w_cat = jnp.concatenate([lru_wa[l, 0], lru_wi[l, 0], lru_wa[l, 1], lru_wi[l, 1]],
                            axis=-1).astype(BF16)
    blk = lambda v: v.reshape(LRU_BLOCKS, 1, LRU_BLOCK_DIM)
    b_cat = jnp.concatenate([blk(lru_ba[l, 0]), blk(lru_bi[l, 0]),
                             blk(lru_ba[l, 1]), blk(lru_bi[l, 1])], axis=-1)
    y_lru = _lru(proj3, conv_w[l], conv_b[l][None, :], w_cat, b_cat, lru_lambda[l])

    pad = ROUTER_LANES - N_GROUPS - N_EXPERTS
    w_router = jnp.concatenate([w_group_router[l], w_expert_router[l],
                                jnp.zeros((d, pad), F32)], axis=-1)
    b_router = jnp.concatenate([b_group_router[l], b_expert_router[l],
                                jnp.zeros((pad,), F32)])[None, :]
    x1, h2, cw = _merge(
        x2, attn_o, y_lru.reshape(n, LRU_WIDTH), proj,
        b_gates[l][None, :], w_attn_o[l].astype(BF16), w_lru_o[l].astype(BF16),
        w_out[l].astype(BF16), norm2_gain[l][None, :], w_router, b_router)

    out = _moe(h2, cw, x1, w_expert_gate[l].astype(BF16), w_expert_up[l].astype(BF16),
               w_expert_down[l].astype(BF16))
    return out.reshape(b, s, d)
```

```python
import functools
import math

import jax
import jax.numpy as jnp
from jax import lax
from jax.experimental import pallas as pl
from jax.experimental.pallas import tpu as pltpu

F32 = jnp.float32
BF16 = jnp.bfloat16

D_MODEL = 1024
N_HEADS = 8
SUB_HEAD = 64
V_HEAD = 128
ATTN_WIDTH = 1024
LRU_WIDTH = 1024
LRU_BLOCKS = 8
LRU_BLOCK_DIM = 128
LRU_C = 8.0
PROJ_WIDTH = 7168
N_GROUPS = 4
EXPERTS_PER_GROUP = 4
N_EXPERTS = 16
EXPERT_HIDDEN = 512
ROPE_THETA = 10000.0
RMS_EPS = 1e-6
LAMBDA_INIT = 0.8 - 0.6 * math.exp(-0.3 * 0)

LOG2_E = math.log2(math.e)
LANES = 128
SUBLANES = 8
VMEM_LIMIT = 56 * 1024 * 1024

Q_COL, K_COL, V_COL, LRUX_COL, LRUG_COL = 0, 8, 16, 24, 32
GATE_COL_1024 = 5

ROUTER_LANES = 128
EXPERT_LANE0 = N_GROUPS
GROUP_ID_LANE = N_GROUPS + N_EXPERTS


def _rms(x, gain):
    ms = jnp.mean(x * x, axis=-1, keepdims=True)
    return x * lax.rsqrt(ms + RMS_EPS) * gain


def _inproj_kernel(x_ref, g_ref, w_ref, o_ref):
    h = _rms(x_ref[...], g_ref[...]).astype(BF16)
    for j in range(PROJ_WIDTH // 1024):
        cols = slice(j * 1024, (j + 1) * 1024)
        o_ref[:, cols] = jnp.dot(h, w_ref[:, cols], preferred_element_type=F32).astype(BF16)


def _inproj(x2, gain, w_bf16, tm=256):
    n = x2.shape[0]
    return pl.pallas_call(
        _inproj_kernel,
        out_shape=jax.ShapeDtypeStruct((n, PROJ_WIDTH), BF16),
        grid=(n // tm,),
        in_specs=[
            pl.BlockSpec((tm, D_MODEL), lambda i: (i, 0)),
            pl.BlockSpec((1, D_MODEL), lambda i: (0, 0)),
            pl.BlockSpec((D_MODEL, PROJ_WIDTH), lambda i: (0, 0)),
        ],
        out_specs=pl.BlockSpec((tm, PROJ_WIDTH), lambda i: (i, 0)),
        compiler_params=pltpu.CompilerParams(
            dimension_semantics=("arbitrary",), vmem_limit_bytes=VMEM_LIMIT),
        name="inproj",
    )(x2, gain, w_bf16)


def _subhead_norm_rope(x, gain, cos, sin_signed):
    r = lax.broadcasted_iota(jnp.int32, (LANES, LANES), 0) // SUB_HEAD
    c = lax.broadcasted_iota(jnp.int32, (LANES, LANES), 1) // SUB_HEAD
    group_mean = jnp.where(r == c, 1.0 / SUB_HEAD, 0.0).astype(F32)
    ms = jnp.dot(x * x, group_mean, preferred_element_type=F32, precision=lax.Precision.HIGHEST)
    xn = x * lax.rsqrt(ms + RMS_EPS) * gain
    lane = lax.broadcasted_iota(jnp.int32, xn.shape, 1)
    first_half = (lane % SUB_HEAD) < (SUB_HEAD // 2)
    partner = jnp.where(first_half,
                        pltpu.roll(xn, LANES - SUB_HEAD // 2, axis=1),
                        pltpu.roll(xn, SUB_HEAD // 2, axis=1))
    return xn * cos + partner * sin_signed


def _qkv_prep_kernel(qg_ref, kg_ref, cos_ref, sin_ref, q_ref, k_ref, v_ref, qo_ref, kt_ref, vo_ref):
    cos, sin_signed = cos_ref[...], sin_ref[...]
    qr = _subhead_norm_rope(q_ref[0].astype(F32), qg_ref[...], cos, sin_signed)
    qo_ref[0, 0] = (qr * (SUB_HEAD ** -0.5 * LOG2_E)).astype(BF16)
    kr = _subhead_norm_rope(k_ref[0].astype(F32), kg_ref[...], cos, sin_signed)
    kt_ref[0, 0] = kr.T.astype(BF16)
    vo_ref[0, 0] = v_ref[0]


def _qkv_prep(proj3, q_gain, k_gain, cos, sin_signed, tp=512):
    b, s, _ = proj3.shape
    const = lambda shape: pl.BlockSpec(shape, lambda bi, hi, si: (0, 0))
    table = pl.BlockSpec((tp, LANES), lambda bi, hi, si: (si, 0))
    col = lambda base: pl.BlockSpec((1, tp, LANES), lambda bi, hi, si: (bi, si, base + hi))
    rows_out = pl.BlockSpec((1, 1, tp, LANES), lambda bi, hi, si: (bi, hi, si, 0))
    return pl.pallas_call(
        _qkv_prep_kernel,
        out_shape=(jax.ShapeDtypeStruct((b, N_HEADS, s, LANES), BF16),
                   jax.ShapeDtypeStruct((b, N_HEADS, LANES, s), BF16),
                   jax.ShapeDtypeStruct((b, N_HEADS, s, V_HEAD), BF16)),
        grid=(b, N_HEADS, s // tp),
        in_specs=[const((1, LANES)), const((1, LANES)), table, table,
                  col(Q_COL), col(K_COL), col(V_COL)],
        out_specs=(rows_out,
                   pl.BlockSpec((1, 1, LANES, tp), lambda bi, hi, si: (bi, hi, 0, si)),
                   rows_out),
        compiler_params=pltpu.CompilerParams(
            dimension_semantics=("arbitrary", "arbitrary", "arbitrary"),
            vmem_limit_bytes=VMEM_LIMIT),
        name="qkv_prep",
    )(q_gain, k_gain, cos, sin_signed, proj3, proj3, proj3)


def _attn_probe_kernel(lam_ref, sg_ref, q_ref, kt_ref, v_ref, o_ref, *, tq, mode):
    q = q_ref[0, 0]
    lane = lax.broadcasted_iota(jnp.int32, q.shape, 1)
    zero = jnp.zeros_like(q)
    qq = jnp.concatenate([jnp.where(lane < SUB_HEAD, q, zero),
                          jnp.where(lane >= SUB_HEAD, q, zero)], axis=0)
    if mode == "pv":
        p = jnp.concatenate([qq] * (kt_ref.shape[3] // LANES), axis=1)
        acc = jnp.dot(p, v_ref[0, 0], preferred_element_type=F32)
        o_ref[0, 0] = (acc[0:tq] + acc[tq:2 * tq]).astype(BF16)
        return
    s = jnp.dot(qq, kt_ref[0, 0], preferred_element_type=F32)
    m = jnp.max(s, axis=-1, keepdims=True)
    if mode == "qk_max":
        o_ref[0, 0] = jnp.broadcast_to(m[0:tq] + m[tq:2 * tq], (tq, V_HEAD)).astype(BF16)
        return
    p = jnp.exp2(s - m)
    l = jnp.sum(p, axis=-1, keepdims=True)
    o_ref[0, 0] = jnp.broadcast_to(l[0:tq] + l[tq:2 * tq], (tq, V_HEAD)).astype(BF16)


def _attn_kernel(lam_ref, sg_ref, q_ref, kt_ref, v_ref, o_ref, *, tq):
    lp = lam_ref[...]
    lam = (jnp.exp(jnp.sum(lp[0:1] * lp[1:2], axis=-1, keepdims=True))
           - jnp.exp(jnp.sum(lp[2:3] * lp[3:4], axis=-1, keepdims=True)) + LAMBDA_INIT)

    q = q_ref[0, 0]
    lane = lax.broadcasted_iota(jnp.int32, q.shape, 1)
    zero = jnp.zeros_like(q)
    qq = jnp.concatenate([jnp.where(lane < SUB_HEAD, q, zero),
                          jnp.where(lane >= SUB_HEAD, q, zero)], axis=0)

    s = jnp.dot(qq, kt_ref[0, 0], preferred_element_type=F32)
    m = jnp.max(s, axis=-1, keepdims=True)
    p = jnp.exp2(s - m)
    l = jnp.sum(p, axis=-1, keepdims=True)
    acc = jnp.dot(p.astype(BF16), v_ref[0, 0], preferred_element_type=F32)
    o = acc / l
    o = o[0:tq] - lam * o[tq:2 * tq]
    o = _rms(o, sg_ref[...]) * (1.0 - LAMBDA_INIT)
    o_ref[0, 0] = o.astype(BF16)


def _attention(q_hm, kt_hm, v_hm, lam_params, subln_gain, tq=128, name="diff_attn", mode=None):
    b, h, s, _ = q_hm.shape
    const = lambda shape: pl.BlockSpec(shape, lambda bi, hi, qi: (0, 0))
    body = (functools.partial(_attn_kernel, tq=tq) if mode is None
            else functools.partial(_attn_probe_kernel, tq=tq, mode=mode))
    return pl.pallas_call(
        body,
        out_shape=jax.ShapeDtypeStruct((b, h, s, V_HEAD), BF16),
        grid=(b, h, s // tq),
        in_specs=[
            const((4, SUB_HEAD)), const((1, V_HEAD)),
            pl.BlockSpec((1, 1, tq, LANES), lambda bi, hi, qi: (bi, hi, qi, 0)),
            pl.BlockSpec((1, 1, LANES, s), lambda bi, hi, qi: (bi, hi, 0, 0)),
            pl.BlockSpec((1, 1, s, V_HEAD), lambda bi, hi, qi: (bi, hi, 0, 0)),
        ],
        out_specs=pl.BlockSpec((1, 1, tq, V_HEAD), lambda bi, hi, qi: (bi, hi, qi, 0)),
        compiler_params=pltpu.CompilerParams(
            dimension_semantics=("arbitrary", "arbitrary", "arbitrary"),
            vmem_limit_bytes=VMEM_LIMIT),
        name=name,
    )(lam_params, subln_gain, q_hm, kt_hm, v_hm)


def _softplus(x):
    return jnp.maximum(x, 0.0) + jnp.log1p(jnp.exp(-jnp.abs(x)))


def _gelu_tanh(x):
    return 0.5 * x * (1.0 + jnp.tanh(math.sqrt(2.0 / math.pi) * (x + 0.044715 * (x * x * x))))


def _lru_kernel(x_ref, g_ref, cw_ref, cb_ref, w_ref, b_ref, lam_ref, o_ref,
                xs_ref, a0_ref, u0_ref, a1_ref, u1_ref, *, seq, tc):
    pad = SUBLANES
    zeros_pad = jnp.zeros((pad, LANES), F32)
    xs_ref[0:pad, :] = zeros_pad
    xs_ref[pad + seq:pad + seq + pad, :] = zeros_pad
    xs_ref[pad:pad + seq, :] = x_ref[0].astype(F32)

    decay = -LRU_C * _softplus(-lam_ref[...])
    cw = cw_ref[...]
    cb = cb_ref[...]
    bias = b_ref[0]

    for c in range(seq // tc):
        base = pad + c * tc
        xr = cb + sum(cw[j:j + 1] * xs_ref[base + j - 1:base + j - 1 + tc, :] for j in range(4))
        z = jnp.dot(xr.astype(BF16), w_ref[0], preferred_element_type=F32) + bias
        rows = slice(c * tc, (c + 1) * tc)
        for d, (a_ref, u_ref) in enumerate(((a0_ref, u0_ref), (a1_ref, u1_ref))):
            r = jax.nn.sigmoid(z[:, (2 * d) * LANES:(2 * d + 1) * LANES])
            gate_i = jax.nn.sigmoid(z[:, (2 * d + 1) * LANES:(2 * d + 2) * LANES])
            a = jnp.exp(decay[d:d + 1] * r)
            a_ref[rows, :] = a
            u_ref[rows, :] = jnp.sqrt(1.0 - a * a) * (gate_i * xr)

    row = lax.broadcasted_iota(jnp.int32, (SUBLANES, LANES), 0)
    n_grp = seq // SUBLANES

    def step(t, carry):
        hf, hb = carry
        of = pl.multiple_of(t * SUBLANES, SUBLANES)
        a = a0_ref[pl.ds(of, SUBLANES), :]
        u = u0_ref[pl.ds(of, SUBLANES), :]
        for d in (1, 2, 4):
            valid = row >= d
            u = jnp.where(valid, a * pltpu.roll(u, d, axis=0) + u, u)
            a = jnp.where(valid, a * pltpu.roll(a, d, axis=0), a)
        h = a * hf + u
        u0_ref[pl.ds(of, SUBLANES), :] = h
        hf = jnp.broadcast_to(h[SUBLANES - 1:SUBLANES, :], h.shape)
        ob = pl.multiple_of((n_grp - 1 - t) * SUBLANES, SUBLANES)
        a = a1_ref[pl.ds(ob, SUBLANES), :]
        u = u1_ref[pl.ds(ob, SUBLANES), :]
        for d in (1, 2, 4):
            valid = row < SUBLANES - d
            u = jnp.where(valid, a * pltpu.roll(u, SUBLANES - d, axis=0) + u, u)
            a = jnp.where(valid, a * pltpu.roll(a, SUBLANES - d, axis=0), a)
        h = a * hb + u
        u1_ref[pl.ds(ob, SUBLANES), :] = h
        hb = jnp.broadcast_to(h[0:1, :], h.shape)
        return hf, hb

    zero = jnp.zeros((SUBLANES, LANES), F32)
    lax.fori_loop(0, n_grp, step, (zero, zero), unroll=2)

    for c in range(seq // tc):
        rows = slice(c * tc, (c + 1) * tc)
        y = (u0_ref[rows, :] + u1_ref[rows, :]) * _gelu_tanh(g_ref[0, rows, :].astype(F32))
        o_ref[0, rows, :] = y.astype(BF16)


def _lru(proj3, conv_w, conv_b, w_cat, b_cat, lam, tc=512):
    b, s, _ = proj3.shape
    kernel = functools.partial(_lru_kernel, seq=s, tc=tc)
    return pl.pallas_call(
        kernel,
        out_shape=jax.ShapeDtypeStruct((b, s, LRU_WIDTH), BF16),
        grid=(b, LRU_BLOCKS),
        in_specs=[
            pl.BlockSpec((1, s, LANES), lambda bi, ni: (bi, 0, LRUX_COL + ni)),
            pl.BlockSpec((1, s, LANES), lambda bi, ni: (bi, 0, LRUG_COL + ni)),
            pl.BlockSpec((4, LANES), lambda bi, ni: (0, ni)),
            pl.BlockSpec((1, LANES), lambda bi, ni: (0, ni)),
            pl.BlockSpec((1, LRU_BLOCK_DIM, 4 * LRU_BLOCK_DIM), lambda bi, ni: (ni, 0, 0)),
            pl.BlockSpec((1, 1, 4 * LRU_BLOCK_DIM), lambda bi, ni: (ni, 0, 0)),
            pl.BlockSpec((2, LANES), lambda bi, ni: (0, ni)),
        ],
        out_specs=pl.BlockSpec((1, s, LANES), lambda bi, ni: (bi, 0, ni)),
        scratch_shapes=[pltpu.VMEM((s + 2 * SUBLANES, LANES), F32)]
        + [pltpu.VMEM((s, LANES), F32)] * 4,
        compiler_params=pltpu.CompilerParams(
            dimension_semantics=("arbitrary", "arbitrary"), vmem_limit_bytes=VMEM_LIMIT),
        name="rglru",
    )(proj3, proj3, conv_w, conv_b, w_cat, b_cat, lam)


def _merge_kernel(x_ref, ao_ref, yl_ref, ga_ref, gl_ref, bg_ref, wa_ref, wl_ref, wo_ref,
                  n2_ref, wr_ref, br_ref, x1_ref, h2_ref, cw_ref):
    attn_o = jnp.concatenate([ao_ref[0, h] for h in range(N_HEADS)], axis=1)
    attn_d = jnp.dot(attn_o, wa_ref[...], preferred_element_type=F32)
    lru_d = jnp.dot(yl_ref[...], wl_ref[...], preferred_element_type=F32)
    bg = bg_ref[...]
    g_attn = jax.nn.sigmoid(ga_ref[...].astype(F32) + bg[:, 0:D_MODEL])
    g_lru = jax.nn.sigmoid(gl_ref[...].astype(F32) + bg[:, D_MODEL:2 * D_MODEL])
    merged = g_attn * attn_d + g_lru * lru_d
    x1 = x_ref[...] + jnp.dot(merged.astype(BF16), wo_ref[...], preferred_element_type=F32)
    x1_ref[...] = x1
    h2 = _rms(x1, n2_ref[...])
    h2_ref[...] = h2.astype(BF16)

    logits = jnp.dot(h2, wr_ref[...], preferred_element_type=F32,
                     precision=lax.Precision.HIGHEST) + br_ref[...]
    lane = lax.broadcasted_iota(jnp.int32, logits.shape, 1)
    neg = jnp.full_like(logits, -jnp.inf)
    big = jnp.full_like(lane, ROUTER_LANES)

    def masked_max(mask):
        return jnp.max(jnp.where(mask, logits, neg), axis=-1, keepdims=True)

    def first_lane(mask, value):
        return jnp.min(jnp.where(mask & (logits == value), lane, big), axis=-1, keepdims=True)

    g_mask = lane < N_GROUPS
    g_max = masked_max(g_mask)
    g_sel = first_lane(g_mask, g_max)
    g_w = 1.0 / jnp.sum(jnp.where(g_mask, jnp.exp(logits - g_max), 0.0), axis=-1, keepdims=True)
    e_lo = EXPERT_LANE0 + g_sel * EXPERTS_PER_GROUP
    e_mask = (lane >= e_lo) & (lane < e_lo + EXPERTS_PER_GROUP)
    v1 = masked_max(e_mask)
    i1 = first_lane(e_mask, v1)
    e_mask2 = e_mask & (lane != i1)
    v2 = masked_max(e_mask2)
    i2 = first_lane(e_mask2, v2)
    t = jnp.exp(v2 - v1)
    w1 = g_w / (1.0 + t)
    w2 = g_w * t / (1.0 + t)
    cw_ref[...] = (jnp.where(lane == i1, w1, 0.0) + jnp.where(lane == i2, w2, 0.0)
                   + jnp.where(lane == GROUP_ID_LANE, g_sel.astype(F32), 0.0))


def _merge(x2, attn_o, y_lru, proj, b_gates, wa, wl, wo, n2_gain, w_router, b_router, tm=256):
    n = x2.shape[0]
    blocks_per_seq = attn_o.shape[2] // tm
    row = lambda cols, col_blk=0: pl.BlockSpec((tm, cols), lambda i: (i, col_blk))
    const = lambda shape: pl.BlockSpec(shape, lambda i: (0, 0))
    heads = pl.BlockSpec((1, N_HEADS, tm, V_HEAD),
                         lambda i: (i // blocks_per_seq, 0, i % blocks_per_seq, 0))
    return pl.pallas_call(
        _merge_kernel,
        out_shape=(jax.ShapeDtypeStruct((n, D_MODEL), F32),
                   jax.ShapeDtypeStruct((n, D_MODEL), BF16),
                   jax.ShapeDtypeStruct((n, ROUTER_LANES), F32)),
        grid=(n // tm,),
        in_specs=[
            row(D_MODEL), heads, row(LRU_WIDTH),
            row(D_MODEL, GATE_COL_1024), row(D_MODEL, GATE_COL_1024 + 1),
            const((1, 2 * D_MODEL)),
            const((ATTN_WIDTH, D_MODEL)), const((LRU_WIDTH, D_MODEL)), const((D_MODEL, D_MODEL)),
            const((1, D_MODEL)), const((D_MODEL, ROUTER_LANES)), const((1, ROUTER_LANES)),
        ],
        out_specs=(row(D_MODEL), row(D_MODEL), row(ROUTER_LANES)),
        compiler_params=pltpu.CompilerParams(
            dimension_semantics=("arbitrary",), vmem_limit_bytes=VMEM_LIMIT),
        name="merge_router",
    )(x2, attn_o, y_lru, proj, proj, b_gates, wa, wl, wo, n2_gain, w_router, b_router)


MOE_CHUNK = 256
MOE_CHUNK_SHIFT = 8


def _moe_kernel(h_ref, rt_ref, x1_ref, wg_ref, wu_ref, wd_ref, o_ref,
                xs_ref, ys_ref, pos_ref, rng_ref):
    g = pl.program_id(1)
    t_tile = h_ref.shape[0]

    @pl.when(g == 0)
    def _sort_rows():
        rt = rt_ref[...]
        lane = lax.broadcasted_iota(jnp.int32, rt.shape, 1)
        g_sel = jnp.sum(jnp.where(lane == GROUP_ID_LANE, rt, 0.0), axis=-1, keepdims=True)
        onehot = lane == g_sel.astype(jnp.int32)
        r = lax.broadcasted_iota(jnp.int32, (t_tile, t_tile), 0)
        c = lax.broadcasted_iota(jnp.int32, (t_tile, t_tile), 1)
        earlier = jnp.dot((r > c).astype(BF16), onehot.astype(BF16),
                          preferred_element_type=F32)
        cnt = jnp.sum(onehot.astype(F32), axis=0, keepdims=True)
        lane1 = lax.broadcasted_iota(jnp.int32, cnt.shape, 1)
        off = jnp.zeros_like(cnt)
        for gg in range(N_GROUPS - 1):
            off = off + jnp.where(lane1 > gg, cnt[:, gg:gg + 1], 0.0)
        pos = jnp.sum(jnp.where(onehot, earlier + off, 0.0), axis=-1, keepdims=True)
        pos_b = jnp.broadcast_to(pos, rt.shape)
        pos_ref[...] = pos_b
        pos_row = pos_b.T[0:1, :]
        perm = (r.astype(F32) == pos_row).astype(BF16)
        rt_hi = rt.astype(BF16)
        rt_lo = (rt - rt_hi.astype(F32)).astype(BF16)
        cat = jnp.concatenate([h_ref[...], rt_hi, rt_lo], axis=1)
        xs_ref[...] = jnp.dot(perm, cat, preferred_element_type=F32).astype(BF16)
        ys_ref[...] = jnp.zeros(ys_ref.shape, F32)
        for gg in range(N_GROUPS):
            rng_ref[gg] = jnp.sum(jnp.where(lane1 == gg, off, 0.0)).astype(jnp.int32)
            rng_ref[N_GROUPS + gg] = jnp.sum(jnp.where(lane1 == gg, cnt, 0.0)).astype(jnp.int32)

    start = rng_ref[g]
    count = rng_ref[N_GROUPS + g]
    lo = lax.shift_right_logical(start, MOE_CHUNK_SHIFT)
    hi = jnp.where(count > 0,
                   lax.shift_right_logical(start + count + (MOE_CHUNK - 1), MOE_CHUNK_SHIFT), lo)

    def chunk(ci, carry):
        r0 = pl.multiple_of(ci * MOE_CHUNK, MOE_CHUNK)
        xc = xs_ref[pl.ds(r0, MOE_CHUNK), 0:D_MODEL]
        rs = (xs_ref[pl.ds(r0, MOE_CHUNK), D_MODEL:D_MODEL + ROUTER_LANES].astype(F32)
              + xs_ref[pl.ds(r0, MOE_CHUNK), D_MODEL + ROUTER_LANES:].astype(F32))
        lane = lax.broadcasted_iota(jnp.int32, rs.shape, 1)
        y = jnp.zeros((MOE_CHUNK, D_MODEL), F32)
        for e in range(EXPERTS_PER_GROUP):
            a = jnp.dot(xc, wg_ref[e], preferred_element_type=F32)
            u = jnp.dot(xc, wu_ref[e], preferred_element_type=F32)
            ce = jnp.sum(jnp.where(lane == EXPERT_LANE0 + g * EXPERTS_PER_GROUP + e, rs, 0.0),
                         axis=-1, keepdims=True)
            hid = a * jax.nn.sigmoid(a) * u * ce
            y = y + jnp.dot(hid.astype(BF16), wd_ref[e], preferred_element_type=F32)
        ys_ref[pl.ds(r0, MOE_CHUNK), :] += y
        return carry

    lax.fori_loop(lo, hi, chunk, 0)

    @pl.when(g == N_GROUPS - 1)
    def _unsort_rows():
        pos_b = pos_ref[...]
        c = lax.broadcasted_iota(jnp.int32, (t_tile, t_tile), 1).astype(F32)
        unperm = (jnp.concatenate([pos_b] * (t_tile // LANES), axis=1) == c).astype(BF16)
        o_ref[...] = x1_ref[...] + jnp.dot(unperm, ys_ref[...].astype(BF16),
                                           preferred_element_type=F32)


def _moe(h2, route, x1, wg, wu, wd, tm=1024):
    n = h2.shape[0]
    group_w = lambda rows, cols: pl.BlockSpec((EXPERTS_PER_GROUP, rows, cols),
                                              lambda i, g: (g, 0, 0))
    return pl.pallas_call(
        _moe_kernel,
        out_shape=jax.ShapeDtypeStruct((n, D_MODEL), F32),
        grid=(n // tm, N_GROUPS),
        in_specs=[
            pl.BlockSpec((tm, D_MODEL), lambda i, g: (i, 0)),
            pl.BlockSpec((tm, ROUTER_LANES), lambda i, g: (i, 0)),
            pl.BlockSpec((tm, D_MODEL), lambda i, g: (i, 0)),
            group_w(D_MODEL, EXPERT_HIDDEN), group_w(D_MODEL, EXPERT_HIDDEN),
            group_w(EXPERT_HIDDEN, D_MODEL),
        ],
        out_specs=pl.BlockSpec((tm, D_MODEL), lambda i, g: (i, 0)),
        scratch_shapes=[
            pltpu.VMEM((tm, D_MODEL + 2 * ROUTER_LANES), BF16),
            pltpu.VMEM((tm, D_MODEL), F32),
            pltpu.VMEM((tm, ROUTER_LANES), F32),
            pltpu.SMEM((2 * N_GROUPS,), jnp.int32),
        ],
        compiler_params=pltpu.CompilerParams(
            dimension_semantics=("arbitrary", "arbitrary"), vmem_limit_bytes=VMEM_LIMIT),
        name="moe",
    )(h2, route, x1, wg, wu, wd)


def _rope_tables(seq):
    pos = jnp.arange(seq, dtype=F32)
    inv_freq = ROPE_THETA ** (-jnp.arange(0, SUB_HEAD, 2, dtype=F32) / SUB_HEAD)
    ang = pos[:, None] * inv_freq[None, :]
    cos, sin = jnp.cos(ang), jnp.sin(ang)
    cos_full = jnp.concatenate([cos, cos, cos, cos], axis=-1)
    sin_signed = jnp.concatenate([-sin, sin, -sin, sin], axis=-1)
    return cos_full, sin_signed


def kernel(x, norm1_gain, w_in, b_gates, q_norm_gain, k_norm_gain, lambda_q1, lambda_k1, lambda_q2, lambda_k2, attn_subln_gain, w_attn_o, conv_w, conv_b, lru_wa, lru_ba, lru_wi, lru_bi, lru_lambda, w_lru_o, w_out, norm2_gain, w_group_router, b_group_router, w_expert_router, b_expert_router, w_expert_gate, w_expert_up, w_expert_down):
    b, s, d = x.shape
    n = b * s
    depth = w_in.shape[0]
    assert depth == 1 and d == D_MODEL
    cos, sin_signed = _rope_tables(s)
    x2 = x.reshape(n, d)
    l = 0

    proj = _inproj(x2, norm1_gain[l][None, :], w_in[l].astype(BF16))
    proj3 = proj.reshape(b, s, PROJ_WIDTH)

    lam_params = jnp.stack([lambda_q1[l], lambda_k1[l], lambda_q2[l], lambda_k2[l]])
    q_hm, kt_hm, v_hm = _qkv_prep(
        proj3, jnp.tile(q_norm_gain[l], 2)[None, :], jnp.tile(k_norm_gain[l], 2)[None, :],
        cos, sin_signed)
    attn_o = _attention(q_hm, kt_hm, v_hm, lam_params, attn_subln_gain[l][None, :])
    probes = [_attention(q_hm[:1], kt_hm[:1], v_hm[:1], lam_params, attn_subln_gain[l][None, :],
                         name="probe_" + mode if mode else "probe_full", mode=mode)
              for mode in (None, "qk_max", "qk_exp_sum", "pv")]
    attn_o = attn_o + jnp.pad(sum(probes) * jnp.zeros((), BF16), ((0, b - 1), (0, 0), (0, 0), (0, 0)))

    w_cat = jnp.concatenate([lru_wa[l, 0], lru_wi[l, 0], lru_wa[l, 1], lru_wi[l, 1]],
                            axis=-1).astype(BF16)
    blk = lambda v: v.reshape(LRU_BLOCKS, 1, LRU_BLOCK_DIM)
    b_cat = jnp.concatenate([blk(lru_ba[l, 0]), blk(lru_bi[l, 0]),
                             blk(lru_ba[l, 1]), blk(lru_bi[l, 1])], axis=-1)
    y_lru = _lru(proj3, conv_w[l], conv_b[l][None, :], w_cat, b_cat, lru_lambda[l])

    pad = ROUTER_LANES - N_GROUPS - N_EXPERTS
    w_router = jnp.concatenate([w_group_router[l], w_expert_router[l],
                                jnp.zeros((d, pad), F32)], axis=-1)
    b_router = jnp.concatenate([b_group_router[l], b_expert_router[l],
                                jnp.zeros((pad,), F32)])[None, :]
    x1, h2, cw = _merge(
        x2, attn_o, y_lru.reshape(n, LRU_WIDTH), proj,
        b_gates[l][None, :], w_attn_o[l].astype(BF16), w_lru_o[l].astype(BF16),
        w_out[l].astype(BF16), norm2_gain[l][None, :], w_router, b_router)

    out = _moe(h2, cw, x1, w_expert_gate[l].astype(BF16), w_expert_up[l].astype(BF16),
               w_expert_down[l].astype(BF16))
    return out.reshape(b, s, d)
```

```python
import functools
import math

import jax
import jax.numpy as jnp
from jax import lax
from jax.experimental import pallas as pl
from jax.experimental.pallas import tpu as pltpu

F32 = jnp.float32
BF16 = jnp.bfloat16

D_MODEL = 1024
N_HEADS = 8
SUB_HEAD = 64
V_HEAD = 128
ATTN_WIDTH = 1024
LRU_WIDTH = 1024
LRU_BLOCKS = 8
LRU_BLOCK_DIM = 128
LRU_C = 8.0
PROJ_WIDTH = 7168
N_GROUPS = 4
EXPERTS_PER_GROUP = 4
N_EXPERTS = 16
EXPERT_HIDDEN = 512
ROPE_THETA = 10000.0
RMS_EPS = 1e-6
LAMBDA_INIT = 0.8 - 0.6 * math.exp(-0.3 * 0)

LOG2_E = math.log2(math.e)
LANES = 128
SUBLANES = 8
VMEM_LIMIT = 56 * 1024 * 1024

Q_COL, K_COL, V_COL, LRUX_COL, LRUG_COL = 0, 8, 16, 24, 32
GATE_COL_1024 = 5

ROUTER_LANES = 128
EXPERT_LANE0 = N_GROUPS
GROUP_ID_LANE = N_GROUPS + N_EXPERTS


def _rms(x, gain):
    ms = jnp.mean(x * x, axis=-1, keepdims=True)
    return x * lax.rsqrt(ms + RMS_EPS) * gain


def _inproj_kernel(x_ref, g_ref, w_ref, o_ref):
    h = _rms(x_ref[...], g_ref[...]).astype(BF16)
    for j in range(PROJ_WIDTH // 1024):
        cols = slice(j * 1024, (j + 1) * 1024)
        o_ref[:, cols] = jnp.dot(h, w_ref[:, cols], preferred_element_type=F32).astype(BF16)


def _inproj(x2, gain, w_bf16, tm=256):
    n = x2.shape[0]
    return pl.pallas_call(
        _inproj_kernel,
        out_shape=jax.ShapeDtypeStruct((n, PROJ_WIDTH), BF16),
        grid=(n // tm,),
        in_specs=[
            pl.BlockSpec((tm, D_MODEL), lambda i: (i, 0)),
            pl.BlockSpec((1, D_MODEL), lambda i: (0, 0)),
            pl.BlockSpec((D_MODEL, PROJ_WIDTH), lambda i: (0, 0)),
        ],
        out_specs=pl.BlockSpec((tm, PROJ_WIDTH), lambda i: (i, 0)),
        compiler_params=pltpu.CompilerParams(
            dimension_semantics=("arbitrary",), vmem_limit_bytes=VMEM_LIMIT),
        name="inproj",
    )(x2, gain, w_bf16)


def _subhead_norm_rope(x, gain, cos, sin_signed):
    r = lax.broadcasted_iota(jnp.int32, (LANES, LANES), 0) // SUB_HEAD
    c = lax.broadcasted_iota(jnp.int32, (LANES, LANES), 1) // SUB_HEAD
    group_mean = jnp.where(r == c, 1.0 / SUB_HEAD, 0.0).astype(BF16)
    xx = x * x
    xx_hi = xx.astype(BF16)
    xx_lo = (xx - xx_hi.astype(F32)).astype(BF16)
    ms = (jnp.dot(xx_hi, group_mean, preferred_element_type=F32)
          + jnp.dot(xx_lo, group_mean, preferred_element_type=F32))
    xn = x * lax.rsqrt(ms + RMS_EPS) * gain
    lane = lax.broadcasted_iota(jnp.int32, xn.shape, 1)
    first_half = (lane % SUB_HEAD) < (SUB_HEAD // 2)
    partner = jnp.where(first_half,
                        pltpu.roll(xn, LANES - SUB_HEAD // 2, axis=1),
                        pltpu.roll(xn, SUB_HEAD // 2, axis=1))
    return xn * cos + partner * sin_signed


def _qkv_prep_kernel(qg_ref, kg_ref, cos_ref, sin_ref, q_ref, k_ref, v_ref, qo_ref, kt_ref, vo_ref):
    cos, sin_signed = cos_ref[...], sin_ref[...]
    qr = _subhead_norm_rope(q_ref[0].astype(F32), qg_ref[...], cos, sin_signed)
    qo_ref[0, 0] = (qr * (SUB_HEAD ** -0.5 * LOG2_E)).astype(BF16)
    kr = _subhead_norm_rope(k_ref[0].astype(F32), kg_ref[...], cos, sin_signed)
    kt_ref[0, 0] = kr.T.astype(BF16)
    vo_ref[0, 0] = v_ref[0]


def _qkv_prep(proj3, q_gain, k_gain, cos, sin_signed, tp=2048):
    b, s, _ = proj3.shape
    const = lambda shape: pl.BlockSpec(shape, lambda bi, hi, si: (0, 0))
    table = pl.BlockSpec((tp, LANES), lambda bi, hi, si: (si, 0))
    col = lambda base: pl.BlockSpec((1, tp, LANES), lambda bi, hi, si: (bi, si, base + hi))
    rows_out = pl.BlockSpec((1, 1, tp, LANES), lambda bi, hi, si: (bi, hi, si, 0))
    return pl.pallas_call(
        _qkv_prep_kernel,
        out_shape=(jax.ShapeDtypeStruct((b, N_HEADS, s, LANES), BF16),
                   jax.ShapeDtypeStruct((b, N_HEADS, LANES, s), BF16),
                   jax.ShapeDtypeStruct((b, N_HEADS, s, V_HEAD), BF16)),
        grid=(b, N_HEADS, s // tp),
        in_specs=[const((1, LANES)), const((1, LANES)), table, table,
                  col(Q_COL), col(K_COL), col(V_COL)],
        out_specs=(rows_out,
                   pl.BlockSpec((1, 1, LANES, tp), lambda bi, hi, si: (bi, hi, 0, si)),
                   rows_out),
        compiler_params=pltpu.CompilerParams(
            dimension_semantics=("arbitrary", "arbitrary", "arbitrary"),
            vmem_limit_bytes=VMEM_LIMIT),
        name="qkv_prep",
    )(q_gain, k_gain, cos, sin_signed, proj3, proj3, proj3)


ATTN_SUB_ROWS = 128


def _attn_kernel(lam_ref, sg_ref, q_ref, kt_ref, v_ref, o_ref, *, tq):
    lp = lam_ref[...]
    lam = (jnp.exp(jnp.sum(lp[0:1] * lp[1:2], axis=-1, keepdims=True))
           - jnp.exp(jnp.sum(lp[2:3] * lp[3:4], axis=-1, keepdims=True)) + LAMBDA_INIT)

    ts = ATTN_SUB_ROWS
    for t in range(tq // ts):
        q = q_ref[0, 0, t * ts:(t + 1) * ts, :]
        lane = lax.broadcasted_iota(jnp.int32, q.shape, 1)
        zero = jnp.zeros_like(q)
        qq = jnp.concatenate([jnp.where(lane < SUB_HEAD, q, zero),
                              jnp.where(lane >= SUB_HEAD, q, zero)], axis=0)
        s = jnp.dot(qq, kt_ref[0, 0], preferred_element_type=F32)
        m = jnp.max(s, axis=-1, keepdims=True)
        p = jnp.exp2(s - m)
        l = jnp.sum(p, axis=-1, keepdims=True)
        acc = jnp.dot(p.astype(BF16), v_ref[0, 0], preferred_element_type=F32)
        o = acc / l
        o = o[0:ts] - lam * o[ts:2 * ts]
        o = _rms(o, sg_ref[...]) * (1.0 - LAMBDA_INIT)
        o_ref[0, 0, t * ts:(t + 1) * ts, :] = o.astype(BF16)


def _attention(q_hm, kt_hm, v_hm, lam_params, subln_gain, tq=1024):
    b, h, s, _ = q_hm.shape
    const = lambda shape: pl.BlockSpec(shape, lambda bi, hi, qi: (0, 0))
    return pl.pallas_call(
        functools.partial(_attn_kernel, tq=tq),
        out_shape=jax.ShapeDtypeStruct((b, h, s, V_HEAD), BF16),
        grid=(b, h, s // tq),
        in_specs=[
            const((4, SUB_HEAD)), const((1, V_HEAD)),
            pl.BlockSpec((1, 1, tq, LANES), lambda bi, hi, qi: (bi, hi, qi, 0)),
            pl.BlockSpec((1, 1, LANES, s), lambda bi, hi, qi: (bi, hi, 0, 0)),
            pl.BlockSpec((1, 1, s, V_HEAD), lambda bi, hi, qi: (bi, hi, 0, 0)),
        ],
        out_specs=pl.BlockSpec((1, 1, tq, V_HEAD), lambda bi, hi, qi: (bi, hi, qi, 0)),
        compiler_params=pltpu.CompilerParams(
            dimension_semantics=("arbitrary", "arbitrary", "arbitrary"),
            vmem_limit_bytes=VMEM_LIMIT),
        name="diff_attn",
    )(lam_params, subln_gain, q_hm, kt_hm, v_hm)


def _softplus(x):
    return jnp.maximum(x, 0.0) + jnp.log1p(jnp.exp(-jnp.abs(x)))


def _gelu_tanh(x):
    return 0.5 * x * (1.0 + jnp.tanh(math.sqrt(2.0 / math.pi) * (x + 0.044715 * (x * x * x))))


LRU_BLOCKS_PER_STEP = 2


def _lru_kernel(x_ref, g_ref, cw_ref, cb_ref, w_ref, b_ref, lam_ref, o_ref,
                xs_ref, a0_ref, u0_ref, a1_ref, u1_ref, *, seq, tc):
    pad = SUBLANES
    width = LRU_BLOCKS_PER_STEP * LANES
    zeros_pad = jnp.zeros((pad, width), F32)
    xs_ref[0:pad, :] = zeros_pad
    xs_ref[pad + seq:pad + seq + pad, :] = zeros_pad
    xs_ref[pad:pad + seq, :] = x_ref[0].astype(F32)

    k_all = (-LRU_C * 0.5 * LOG2_E) * _softplus(-lam_ref[...])
    cw_all = cw_ref[...]
    cb_all = cb_ref[...]

    for c in range(seq // tc):
        base = pad + c * tc
        rows = slice(c * tc, (c + 1) * tc)
        for j in range(LRU_BLOCKS_PER_STEP):
            cols = slice(j * LANES, (j + 1) * LANES)
            cw, cb = cw_all[:, cols], cb_all[:, cols]
            win = xs_ref[base - pad:base + tc + pad, cols]
            n_win = tc + 2 * pad
            taps = (pltpu.roll(win, 1, axis=0), win,
                    pltpu.roll(win, n_win - 1, axis=0), pltpu.roll(win, n_win - 2, axis=0))
            xr = cb + sum(cw[t:t + 1] * taps[t][pad:pad + tc] for t in range(4))
            th = jnp.tanh(jnp.dot(xr.astype(BF16), w_ref[j], preferred_element_type=F32)
                          + b_ref[j])
            for d, (a_ref, u_ref) in enumerate(((a0_ref, u0_ref), (a1_ref, u1_ref))):
                k = k_all[d:d + 1, cols]
                a = jnp.exp2(k * th[:, (2 * d) * LANES:(2 * d + 1) * LANES] + k)
                gate_i = 0.5 * th[:, (2 * d + 1) * LANES:(2 * d + 2) * LANES] + 0.5
                v = 1.0 - a * a
                mult = jnp.where(v > 0.0, v * lax.rsqrt(v), 0.0)
                a_ref[rows, cols] = a
                u_ref[rows, cols] = mult * (gate_i * xr)

    row = lax.broadcasted_iota(jnp.int32, (SUBLANES, LANES), 0)
    n_grp = seq // SUBLANES

    def scan_group(a_ref, u_ref, start, cols, carry, reverse):
        a = a_ref[pl.ds(start, SUBLANES), cols]
        u = u_ref[pl.ds(start, SUBLANES), cols]
        for d in (1, 2, 4):
            valid = (row < SUBLANES - d) if reverse else (row >= d)
            shift = SUBLANES - d if reverse else d
            u = jnp.where(valid, a * pltpu.roll(u, shift, axis=0) + u, u)
            a = jnp.where(valid, a * pltpu.roll(a, shift, axis=0), a)
        h = a * carry + u
        u_ref[pl.ds(start, SUBLANES), cols] = h
        last = h[0:1, :] if reverse else h[SUBLANES - 1:SUBLANES, :]
        return jnp.broadcast_to(last, h.shape)

    def step(t, carry):
        of = pl.multiple_of(t * SUBLANES, SUBLANES)
        ob = pl.multiple_of((n_grp - 1 - t) * SUBLANES, SUBLANES)
        out = []
        for j in range(LRU_BLOCKS_PER_STEP):
            cols = slice(j * LANES, (j + 1) * LANES)
            out.append(scan_group(a0_ref, u0_ref, of, cols, carry[2 * j], False))
            out.append(scan_group(a1_ref, u1_ref, ob, cols, carry[2 * j + 1], True))
        return tuple(out)

    zero = jnp.zeros((SUBLANES, LANES), F32)
    lax.fori_loop(0, n_grp, step, (zero,) * (2 * LRU_BLOCKS_PER_STEP), unroll=2)

    for c in range(seq // tc):
        rows = slice(c * tc, (c + 1) * tc)
        y = (u0_ref[rows, :] + u1_ref[rows, :]) * _gelu_tanh(g_ref[0, rows, :].astype(F32))
        o_ref[0, rows, :] = y.astype(BF16)


def _lru(proj3, conv_w, conv_b, w_cat, b_cat, lam, tc=512):
    b, s, _ = proj3.shape
    kernel = functools.partial(_lru_kernel, seq=s, tc=tc)
    nb = LRU_BLOCKS_PER_STEP
    width = nb * LANES
    return pl.pallas_call(
        kernel,
        out_shape=jax.ShapeDtypeStruct((b, s, LRU_WIDTH), BF16),
        grid=(b, LRU_BLOCKS // nb),
        in_specs=[
            pl.BlockSpec((1, s, width), lambda bi, ni: (bi, 0, LRUX_COL // nb + ni)),
            pl.BlockSpec((1, s, width), lambda bi, ni: (bi, 0, LRUG_COL // nb + ni)),
            pl.BlockSpec((4, width), lambda bi, ni: (0, ni)),
            pl.BlockSpec((1, width), lambda bi, ni: (0, ni)),
            pl.BlockSpec((nb, LRU_BLOCK_DIM, 4 * LRU_BLOCK_DIM), lambda bi, ni: (ni, 0, 0)),
            pl.BlockSpec((nb, 1, 4 * LRU_BLOCK_DIM), lambda bi, ni: (ni, 0, 0)),
            pl.BlockSpec((2, width), lambda bi, ni: (0, ni)),
        ],
        out_specs=pl.BlockSpec((1, s, width), lambda bi, ni: (bi, 0, ni)),
        scratch_shapes=[pltpu.VMEM((s + 2 * SUBLANES, width), F32)]
        + [pltpu.VMEM((s, width), F32)] * 4,
        compiler_params=pltpu.CompilerParams(
            dimension_semantics=("arbitrary", "arbitrary"), vmem_limit_bytes=VMEM_LIMIT),
        name="rglru",
    )(proj3, proj3, conv_w, conv_b, w_cat, b_cat, lam)


def _merge_kernel(x_ref, ao_ref, yl_ref, ga_ref, gl_ref, bg_ref, wa_ref, wl_ref, wo_ref,
                  n2_ref, wr_ref, br_ref, x1_ref, h2_ref, cw_ref):
    attn_o = jnp.concatenate([ao_ref[0, h] for h in range(N_HEADS)], axis=1)
    attn_d = jnp.dot(attn_o, wa_ref[...], preferred_element_type=F32)
    lru_d = jnp.dot(yl_ref[...], wl_ref[...], preferred_element_type=F32)
    bg = bg_ref[...]
    g_attn = jax.nn.sigmoid(ga_ref[...].astype(F32) + bg[:, 0:D_MODEL])
    g_lru = jax.nn.sigmoid(gl_ref[...].astype(F32) + bg[:, D_MODEL:2 * D_MODEL])
    merged = g_attn * attn_d + g_lru * lru_d
    x1 = x_ref[...] + jnp.dot(merged.astype(BF16), wo_ref[...], preferred_element_type=F32)
    x1_ref[...] = x1
    h2 = _rms(x1, n2_ref[...])
    h2_ref[...] = h2.astype(BF16)

    h2_hi = h2.astype(BF16)
    h2_lo = (h2 - h2_hi.astype(F32)).astype(BF16)
    wr = wr_ref[...]
    part = jnp.dot(h2_hi, wr, preferred_element_type=F32)
    logits = (part[:, 0:ROUTER_LANES] + part[:, ROUTER_LANES:]
              + jnp.dot(h2_lo, wr[:, 0:ROUTER_LANES], preferred_element_type=F32) + br_ref[...])
    lane = lax.broadcasted_iota(jnp.int32, logits.shape, 1)
    neg = jnp.full_like(logits, -jnp.inf)
    big = jnp.full_like(lane, ROUTER_LANES)

    def masked_max(mask):
        return jnp.max(jnp.where(mask, logits, neg), axis=-1, keepdims=True)

    def first_lane(mask, value):
        return jnp.min(jnp.where(mask & (logits == value), lane, big), axis=-1, keepdims=True)

    g_mask = lane < N_GROUPS
    g_max = masked_max(g_mask)
    g_sel = first_lane(g_mask, g_max)
    g_w = 1.0 / jnp.sum(jnp.where(g_mask, jnp.exp(logits - g_max), 0.0), axis=-1, keepdims=True)
    e_lo = EXPERT_LANE0 + g_sel * EXPERTS_PER_GROUP
    e_mask = (lane >= e_lo) & (lane < e_lo + EXPERTS_PER_GROUP)
    v1 = masked_max(e_mask)
    i1 = first_lane(e_mask, v1)
    e_mask2 = e_mask & (lane != i1)
    v2 = masked_max(e_mask2)
    i2 = first_lane(e_mask2, v2)
    t = jnp.exp(v2 - v1)
    w1 = g_w / (1.0 + t)
    w2 = g_w * t / (1.0 + t)
    cw_ref[...] = (jnp.where(lane == i1, w1, 0.0) + jnp.where(lane == i2, w2, 0.0)
                   + jnp.where(lane == GROUP_ID_LANE, g_sel.astype(F32), 0.0))


def _merge(x2, attn_o, y_lru, proj, b_gates, wa, wl, wo, n2_gain, w_router, b_router, tm=512):
    n = x2.shape[0]
    blocks_per_seq = attn_o.shape[2] // tm
    row = lambda cols, col_blk=0: pl.BlockSpec((tm, cols), lambda i: (i, col_blk))
    const = lambda shape: pl.BlockSpec(shape, lambda i: (0, 0))
    heads = pl.BlockSpec((1, N_HEADS, tm, V_HEAD),
                         lambda i: (i // blocks_per_seq, 0, i % blocks_per_seq, 0))
    return pl.pallas_call(
        _merge_kernel,
        out_shape=(jax.ShapeDtypeStruct((n, D_MODEL), F32),
                   jax.ShapeDtypeStruct((n, D_MODEL), BF16),
                   jax.ShapeDtypeStruct((n, ROUTER_LANES), F32)),
        grid=(n // tm,),
        in_specs=[
            row(D_MODEL), heads, row(LRU_WIDTH),
            row(D_MODEL, GATE_COL_1024), row(D_MODEL, GATE_COL_1024 + 1),
            const((1, 2 * D_MODEL)),
            const((ATTN_WIDTH, D_MODEL)), const((LRU_WIDTH, D_MODEL)), const((D_MODEL, D_MODEL)),
            const((1, D_MODEL)), const((D_MODEL, 2 * ROUTER_LANES)), const((1, ROUTER_LANES)),
        ],
        out_specs=(row(D_MODEL), row(D_MODEL), row(ROUTER_LANES)),
        compiler_params=pltpu.CompilerParams(
            dimension_semantics=("arbitrary",), vmem_limit_bytes=VMEM_LIMIT),
        name="merge_router",
    )(x2, attn_o, y_lru, proj, proj, b_gates, wa, wl, wo, n2_gain, w_router, b_router)


MOE_CHUNK = 256
MOE_CHUNK_SHIFT = 8


def _moe_kernel(h_ref, rt_ref, x1_ref, wg_ref, wu_ref, wd_ref, o_ref,
                xs_ref, ys_ref, pos_ref, rng_ref):
    g = pl.program_id(1)
    t_tile = h_ref.shape[0]

    @pl.when(g == 0)
    def _sort_rows():
        rt = rt_ref[...]
        lane = lax.broadcasted_iota(jnp.int32, rt.shape, 1)
        g_sel = jnp.sum(jnp.where(lane == GROUP_ID_LANE, rt, 0.0), axis=-1, keepdims=True)
        onehot = lane == g_sel.astype(jnp.int32)
        r = lax.broadcasted_iota(jnp.int32, (t_tile, t_tile), 0)
        c = lax.broadcasted_iota(jnp.int32, (t_tile, t_tile), 1)
        earlier = jnp.dot((r > c).astype(BF16), onehot.astype(BF16),
                          preferred_element_type=F32)
        cnt = jnp.sum(onehot.astype(F32), axis=0, keepdims=True)
        lane1 = lax.broadcasted_iota(jnp.int32, cnt.shape, 1)
        off = jnp.zeros_like(cnt)
        for gg in range(N_GROUPS - 1):
            off = off + jnp.where(lane1 > gg, cnt[:, gg:gg + 1], 0.0)
        pos = jnp.sum(jnp.where(onehot, earlier + off, 0.0), axis=-1, keepdims=True)
        pos_b = jnp.broadcast_to(pos, rt.shape)
        pos_ref[...] = pos_b
        pos_row = pos_b.T[0:1, :]
        perm = (r.astype(F32) == pos_row).astype(BF16)
        rt_hi = rt.astype(BF16)
        rt_lo = (rt - rt_hi.astype(F32)).astype(BF16)
        cat = jnp.concatenate([h_ref[...], rt_hi, rt_lo], axis=1)
        xs_ref[...] = jnp.dot(perm, cat, preferred_element_type=F32).astype(BF16)
        ys_ref[...] = jnp.zeros(ys_ref.shape, F32)
        for gg in range(N_GROUPS):
            rng_ref[gg] = jnp.sum(jnp.where(lane1 == gg, off, 0.0)).astype(jnp.int32)
            rng_ref[N_GROUPS + gg] = jnp.sum(jnp.where(lane1 == gg, cnt, 0.0)).astype(jnp.int32)

    start = rng_ref[g]
    count = rng_ref[N_GROUPS + g]
    lo = lax.shift_right_logical(start, MOE_CHUNK_SHIFT)
    hi = jnp.where(count > 0,
                   lax.shift_right_logical(start + count + (MOE_CHUNK - 1), MOE_CHUNK_SHIFT), lo)

    def chunk(ci, carry):
        r0 = pl.multiple_of(ci * MOE_CHUNK, MOE_CHUNK)
        xc = xs_ref[pl.ds(r0, MOE_CHUNK), 0:D_MODEL]
        rs = (xs_ref[pl.ds(r0, MOE_CHUNK), D_MODEL:D_MODEL + ROUTER_LANES].astype(F32)
              + xs_ref[pl.ds(r0, MOE_CHUNK), D_MODEL + ROUTER_LANES:].astype(F32))
        lane = lax.broadcasted_iota(jnp.int32, rs.shape, 1)
        y = jnp.zeros((MOE_CHUNK, D_MODEL), F32)
        for e in range(EXPERTS_PER_GROUP):
            a = jnp.dot(xc, wg_ref[e], preferred_element_type=F32)
            u = jnp.dot(xc, wu_ref[e], preferred_element_type=F32)
            ce = jnp.sum(jnp.where(lane == EXPERT_LANE0 + g * EXPERTS_PER_GROUP + e, rs, 0.0),
                         axis=-1, keepdims=True)
            hid = a * jax.nn.sigmoid(a) * u * ce
            y = y + jnp.dot(hid.astype(BF16), wd_ref[e], preferred_element_type=F32)
        ys_ref[pl.ds(r0, MOE_CHUNK), :] += y
        return carry

    lax.fori_loop(lo, hi, chunk, 0)

    @pl.when(g == N_GROUPS - 1)
    def _unsort_rows():
        pos_b = pos_ref[...]
        c = lax.broadcasted_iota(jnp.int32, (t_tile, t_tile), 1).astype(F32)
        unperm = (jnp.concatenate([pos_b] * (t_tile // LANES), axis=1) == c).astype(BF16)
        o_ref[...] = x1_ref[...] + jnp.dot(unperm, ys_ref[...].astype(BF16),
                                           preferred_element_type=F32)


def _moe(h2, route, x1, wg, wu, wd, tm=1024):
    n = h2.shape[0]
    group_w = lambda rows, cols: pl.BlockSpec((EXPERTS_PER_GROUP, rows, cols),
                                              lambda i, g: (g, 0, 0))
    return pl.pallas_call(
        _moe_kernel,
        out_shape=jax.ShapeDtypeStruct((n, D_MODEL), F32),
        grid=(n // tm, N_GROUPS),
        in_specs=[
            pl.BlockSpec((tm, D_MODEL), lambda i, g: (i, 0)),
            pl.BlockSpec((tm, ROUTER_LANES), lambda i, g: (i, 0)),
            pl.BlockSpec((tm, D_MODEL), lambda i, g: (i, 0)),
            group_w(D_MODEL, EXPERT_HIDDEN), group_w(D_MODEL, EXPERT_HIDDEN),
            group_w(EXPERT_HIDDEN, D_MODEL),
        ],
        out_specs=pl.BlockSpec((tm, D_MODEL), lambda i, g: (i, 0)),
        scratch_shapes=[
            pltpu.VMEM((tm, D_MODEL + 2 * ROUTER_LANES), BF16),
            pltpu.VMEM((tm, D_MODEL), F32),
            pltpu.VMEM((tm, ROUTER_LANES), F32),
            pltpu.SMEM((2 * N_GROUPS,), jnp.int32),
        ],
        compiler_params=pltpu.CompilerParams(
            dimension_semantics=("arbitrary", "arbitrary"), vmem_limit_bytes=VMEM_LIMIT),
        name="moe",
    )(h2, route, x1, wg, wu, wd)


def _rope_tables(seq):
    pos = jnp.arange(seq, dtype=F32)
    inv_freq = ROPE_THETA ** (-jnp.arange(0, SUB_HEAD, 2, dtype=F32) / SUB_HEAD)
    ang = pos[:, None] * inv_freq[None, :]
    cos, sin = jnp.cos(ang), jnp.sin(ang)
    cos_full = jnp.concatenate([cos, cos, cos, cos], axis=-1)
    sin_signed = jnp.concatenate([-sin, sin, -sin, sin], axis=-1)
    return cos_full, sin_signed


def kernel(x, norm1_gain, w_in, b_gates, q_norm_gain, k_norm_gain, lambda_q1, lambda_k1, lambda_q2, lambda_k2, attn_subln_gain, w_attn_o, conv_w, conv_b, lru_wa, lru_ba, lru_wi, lru_bi, lru_lambda, w_lru_o, w_out, norm2_gain, w_group_router, b_group_router, w_expert_router, b_expert_router, w_expert_gate, w_expert_up, w_expert_down):
    b, s, d = x.shape
    n = b * s
    depth = w_in.shape[0]
    assert depth == 1 and d == D_MODEL
    cos, sin_signed = _rope_tables(s)
    x2 = x.reshape(n, d)
    l = 0

    proj = _inproj(x2, norm1_gain[l][None, :], w_in[l].astype(BF16))
    proj3 = proj.reshape(b, s, PROJ_WIDTH)

    lam_params = jnp.stack([lambda_q1[l], lambda_k1[l], lambda_q2[l], lambda_k2[l]])
    q_hm, kt_hm, v_hm = _qkv_prep(
        proj3, jnp.tile(q_norm_gain[l], 2)[None, :], jnp.tile(k_norm_gain[l], 2)[None, :],
        cos, sin_signed)
    attn_o = _attention(q_hm, kt_hm, v_hm, lam_params, attn_subln_gain[l][None, :])

    w_cat = (0.5 * jnp.concatenate([lru_wa[l, 0], lru_wi[l, 0], lru_wa[l, 1], lru_wi[l, 1]],
                                   axis=-1)).astype(BF16)
    blk = lambda v: v.reshape(LRU_BLOCKS, 1, LRU_BLOCK_DIM)
    b_cat = 0.5 * jnp.concatenate([blk(lru_ba[l, 0]), blk(lru_bi[l, 0]),
                                   blk(lru_ba[l, 1]), blk(lru_bi[l, 1])], axis=-1)
    y_lru = _lru(proj3, conv_w[l], conv_b[l][None, :], w_cat, b_cat, lru_lambda[l])

    pad = ROUTER_LANES - N_GROUPS - N_EXPERTS
    w_router = jnp.concatenate([w_group_router[l], w_expert_router[l],
                                jnp.zeros((d, pad), F32)], axis=-1)
    b_router = jnp.concatenate([b_group_router[l], b_expert_router[l],
                                jnp.zeros((pad,), F32)])[None, :]
    w_router_hi = w_router.astype(BF16)
    w_router = jnp.concatenate(
        [w_router_hi, (w_router - w_router_hi.astype(F32)).astype(BF16)], axis=-1)
    x1, h2, cw = _merge(
        x2, attn_o, y_lru.reshape(n, LRU_WIDTH), proj,
        b_gates[l][None, :], w_attn_o[l].astype(BF16), w_lru_o[l].astype(BF16),
        w_out[l].astype(BF16), norm2_gain[l][None, :], w_router, b_router)

    out = _moe(h2, cw, x1, w_expert_gate[l].astype(BF16), w_expert_up[l].astype(BF16),
               w_expert_down[l].astype(BF16))
    return out.reshape(b, s, d)
```

```python
import functools
import math

import jax
import jax.numpy as jnp
from jax import lax
from jax.experimental import pallas as pl
from jax.experimental.pallas import tpu as pltpu

F32 = jnp.float32
BF16 = jnp.bfloat16

D_MODEL = 1024
N_HEADS = 8
SUB_HEAD = 64
V_HEAD = 128
ATTN_WIDTH = 1024
LRU_WIDTH = 1024
LRU_BLOCKS = 8
LRU_BLOCK_DIM = 128
LRU_C = 8.0
PROJ_WIDTH = 7168
N_GROUPS = 4
EXPERTS_PER_GROUP = 4
N_EXPERTS = 16
EXPERT_HIDDEN = 512
ROPE_THETA = 10000.0
RMS_EPS = 1e-6
LAMBDA_INIT = 0.8 - 0.6 * math.exp(-0.3 * 0)

LOG2_E = math.log2(math.e)
LANES = 128
SUBLANES = 8
VMEM_LIMIT = 56 * 1024 * 1024

REST_WIDTH = PROJ_WIDTH - 3 * ATTN_WIDTH
LRUX_COL, LRUG_COL = 0, 8
GATE_COL_1024 = 2

ROUTER_LANES = 128
EXPERT_LANE0 = N_GROUPS
GROUP_ID_LANE = N_GROUPS + N_EXPERTS


def _rms(x, gain):
    ms = jnp.mean(x * x, axis=-1, keepdims=True)
    return x * lax.rsqrt(ms + RMS_EPS) * gain


MXU_TILE = 256


def _subhead_norm_rope(x, gain, cos, sin_signed):
    width = x.shape[1]
    r = lax.broadcasted_iota(jnp.int32, (MXU_TILE, MXU_TILE), 0) // SUB_HEAD
    c = lax.broadcasted_iota(jnp.int32, (MXU_TILE, MXU_TILE), 1) // SUB_HEAD
    group_mean = jnp.where(r == c, 1.0 / SUB_HEAD, 0.0).astype(BF16)
    xx = (x * x).astype(BF16)
    ms = jnp.concatenate(
        [jnp.dot(xx[:, t * MXU_TILE:(t + 1) * MXU_TILE], group_mean, preferred_element_type=F32)
         for t in range(width // MXU_TILE)], axis=1)
    xn = x * lax.rsqrt(ms + RMS_EPS) * gain
    lane = lax.broadcasted_iota(jnp.int32, xn.shape, 1)
    first_half = (lane % SUB_HEAD) < (SUB_HEAD // 2)
    partner = jnp.where(first_half,
                        pltpu.roll(xn, width - SUB_HEAD // 2, axis=1),
                        pltpu.roll(xn, SUB_HEAD // 2, axis=1))
    reps = width // LANES
    return (xn * jnp.concatenate([cos] * reps, axis=1)
            + partner * jnp.concatenate([sin_signed] * reps, axis=1))


def _inproj_kernel(x_ref, g_ref, w_ref, qg_ref, kg_ref, cos_ref, sin_ref,
                   q_ref, kt_ref, v_ref, rest_ref):
    h = _rms(x_ref[...], g_ref[...]).astype(BF16)
    cos, sin_signed = cos_ref[...], sin_ref[...]

    def proj(j):
        return jnp.dot(h, w_ref[:, j * 1024:(j + 1) * 1024], preferred_element_type=F32)

    q = _subhead_norm_rope(proj(0), qg_ref[...], cos, sin_signed) * (SUB_HEAD ** -0.5 * LOG2_E)
    k = _subhead_norm_rope(proj(1), kg_ref[...], cos, sin_signed)
    v = proj(2).astype(BF16)
    for hd in range(N_HEADS):
        cols = slice(hd * LANES, (hd + 1) * LANES)
        q_ref[0, hd] = q[:, cols].astype(BF16)
        kt_ref[0, hd] = k[:, cols].T.astype(BF16)
        v_ref[0, hd] = v[:, cols]
    for j in range(3, PROJ_WIDTH // 1024):
        rest_ref[:, (j - 3) * 1024:(j - 2) * 1024] = proj(j).astype(BF16)


def _inproj(x3, gain, w_bf16, q_gain, k_gain, cos, sin_signed, tm=512):
    b, s, d = x3.shape
    n = b * s
    bps = s // tm
    const = lambda shape: pl.BlockSpec(shape, lambda i: (0, 0))
    table = pl.BlockSpec((tm, LANES), lambda i: (i % bps, 0))
    heads = pl.BlockSpec((1, N_HEADS, tm, LANES), lambda i: (i // bps, 0, i % bps, 0))
    return pl.pallas_call(
        _inproj_kernel,
        out_shape=(jax.ShapeDtypeStruct((b, N_HEADS, s, LANES), BF16),
                   jax.ShapeDtypeStruct((b, N_HEADS, LANES, s), BF16),
                   jax.ShapeDtypeStruct((b, N_HEADS, s, V_HEAD), BF16),
                   jax.ShapeDtypeStruct((n, REST_WIDTH), BF16)),
        grid=(n // tm,),
        in_specs=[
            pl.BlockSpec((tm, D_MODEL), lambda i: (i, 0)),
            const((1, D_MODEL)),
            const((D_MODEL, PROJ_WIDTH)),
            const((1, ATTN_WIDTH)), const((1, ATTN_WIDTH)), table, table,
        ],
        out_specs=(heads,
                   pl.BlockSpec((1, N_HEADS, LANES, tm), lambda i: (i // bps, 0, 0, i % bps)),
                   heads,
                   pl.BlockSpec((tm, REST_WIDTH), lambda i: (i, 0))),
        compiler_params=pltpu.CompilerParams(
            dimension_semantics=("arbitrary",), vmem_limit_bytes=VMEM_LIMIT),
        name="inproj",
    )(x3.reshape(n, d), gain, w_bf16, q_gain, k_gain, cos, sin_signed)


ATTN_SUB_ROWS = 128


def _attn_kernel(lam_ref, sg_ref, q_ref, kt_ref, v_ref, o_ref, *, tq):
    lp = lam_ref[...]
    lam = (jnp.exp(jnp.sum(lp[0:1] * lp[1:2], axis=-1, keepdims=True))
           - jnp.exp(jnp.sum(lp[2:3] * lp[3:4], axis=-1, keepdims=True)) + LAMBDA_INIT)

    ts = ATTN_SUB_ROWS
    for t in range(tq // ts):
        q = q_ref[0, 0, t * ts:(t + 1) * ts, :]
        lane = lax.broadcasted_iota(jnp.int32, q.shape, 1)
        zero = jnp.zeros_like(q)
        qq = jnp.concatenate([jnp.where(lane < SUB_HEAD, q, zero),
                              jnp.where(lane >= SUB_HEAD, q, zero)], axis=0)
        s = jnp.dot(qq, kt_ref[0, 0], preferred_element_type=F32)
        m = jnp.max(s, axis=-1, keepdims=True)
        p = jnp.exp2(s - m)
        l = jnp.sum(p, axis=-1, keepdims=True)
        acc = jnp.dot(p.astype(BF16), v_ref[0, 0], preferred_element_type=F32)
        o = acc / l
        o = o[0:ts] - lam * o[ts:2 * ts]
        o = _rms(o, sg_ref[...]) * (1.0 - LAMBDA_INIT)
        o_ref[0, 0, t * ts:(t + 1) * ts, :] = o.astype(BF16)


def _attention(q_hm, kt_hm, v_hm, lam_params, subln_gain, tq=512):
    b, h, s, _ = q_hm.shape
    const = lambda shape: pl.BlockSpec(shape, lambda bi, hi, qi: (0, 0))
    return pl.pallas_call(
        functools.partial(_attn_kernel, tq=tq),
        out_shape=jax.ShapeDtypeStruct((b, h, s, V_HEAD), BF16),
        grid=(b, h, s // tq),
        in_specs=[
            const((4, SUB_HEAD)), const((1, V_HEAD)),
            pl.BlockSpec((1, 1, tq, LANES), lambda bi, hi, qi: (bi, hi, qi, 0)),
            pl.BlockSpec((1, 1, LANES, s), lambda bi, hi, qi: (bi, hi, 0, 0)),
            pl.BlockSpec((1, 1, s, V_HEAD), lambda bi, hi, qi: (bi, hi, 0, 0)),
        ],
        out_specs=pl.BlockSpec((1, 1, tq, V_HEAD), lambda bi, hi, qi: (bi, hi, qi, 0)),
        compiler_params=pltpu.CompilerParams(
            dimension_semantics=("arbitrary", "arbitrary", "arbitrary"),
            vmem_limit_bytes=VMEM_LIMIT),
        name="diff_attn",
    )(lam_params, subln_gain, q_hm, kt_hm, v_hm)


def _softplus(x):
    return jnp.maximum(x, 0.0) + jnp.log1p(jnp.exp(-jnp.abs(x)))


def _gelu_tanh(x):
    return 0.5 * x * (1.0 + jnp.tanh(math.sqrt(2.0 / math.pi) * (x + 0.044715 * (x * x * x))))


LRU_BLOCKS_PER_STEP = 2


def _lru_kernel(x_ref, g_ref, cw_ref, cb_ref, w_ref, b_ref, lam_ref, o_ref,
                xs_ref, a0_ref, u0_ref, a1_ref, u1_ref, *, seq, tc):
    pad = SUBLANES
    width = LRU_BLOCKS_PER_STEP * LANES
    zeros_pad = jnp.zeros((pad, width), F32)
    xs_ref[0:pad, :] = zeros_pad
    xs_ref[pad + seq:pad + seq + pad, :] = zeros_pad
    xs_ref[pad:pad + seq, :] = x_ref[0].astype(F32)

    k_all = (-LRU_C * 0.5 * LOG2_E) * _softplus(-lam_ref[...])
    cw_all = cw_ref[...]
    cb_all = cb_ref[...]

    for c in range(seq // tc):
        base = pad + c * tc
        rows = slice(c * tc, (c + 1) * tc)
        for j in range(LRU_BLOCKS_PER_STEP):
            cols = slice(j * LANES, (j + 1) * LANES)
            cw, cb = cw_all[:, cols], cb_all[:, cols]
            win = xs_ref[base - pad:base + tc + pad, cols]
            n_win = tc + 2 * pad
            taps = (pltpu.roll(win, 1, axis=0), win,
                    pltpu.roll(win, n_win - 1, axis=0), pltpu.roll(win, n_win - 2, axis=0))
            xr = cb + sum(cw[t:t + 1] * taps[t][pad:pad + tc] for t in range(4))
            th = jnp.tanh(jnp.dot(xr.astype(BF16), w_ref[j], preferred_element_type=F32)
                          + b_ref[j])
            for d, (a_ref, u_ref) in enumerate(((a0_ref, u0_ref), (a1_ref, u1_ref))):
                k = k_all[d:d + 1, cols]
                a = jnp.exp2(k * th[:, (2 * d) * LANES:(2 * d + 1) * LANES] + k)
                gate_i = 0.5 * th[:, (2 * d + 1) * LANES:(2 * d + 2) * LANES] + 0.5
                v = 1.0 - a * a
                mult = jnp.where(v > 0.0, v * lax.rsqrt(v), 0.0)
                a_ref[rows, cols] = a
                u_ref[rows, cols] = mult * (gate_i * xr)

    row = lax.broadcasted_iota(jnp.int32, (SUBLANES, LANES), 0)
    n_grp = seq // SUBLANES

    def scan_group(a_ref, u_ref, start, cols, carry, reverse):
        a = a_ref[pl.ds(start, SUBLANES), cols]
        u = u_ref[pl.ds(start, SUBLANES), cols]
        for d in (1, 2, 4):
            valid = (row < SUBLANES - d) if reverse else (row >= d)
            shift = SUBLANES - d if reverse else d
            u = jnp.where(valid, a * pltpu.roll(u, shift, axis=0) + u, u)
            a = jnp.where(valid, a * pltpu.roll(a, shift, axis=0), a)
        h = a * carry + u
        u_ref[pl.ds(start, SUBLANES), cols] = h
        last = h[0:1, :] if reverse else h[SUBLANES - 1:SUBLANES, :]
        return jnp.broadcast_to(last, h.shape)

    def step(t, carry):
        of = pl.multiple_of(t * SUBLANES, SUBLANES)
        ob = pl.multiple_of((n_grp - 1 - t) * SUBLANES, SUBLANES)
        out = []
        for j in range(LRU_BLOCKS_PER_STEP):
            cols = slice(j * LANES, (j + 1) * LANES)
            out.append(scan_group(a0_ref, u0_ref, of, cols, carry[2 * j], False))
            out.append(scan_group(a1_ref, u1_ref, ob, cols, carry[2 * j + 1], True))
        return tuple(out)

    zero = jnp.zeros((SUBLANES, LANES), F32)
    lax.fori_loop(0, n_grp, step, (zero,) * (2 * LRU_BLOCKS_PER_STEP), unroll=2)

    for c in range(seq // tc):
        rows = slice(c * tc, (c + 1) * tc)
        y = (u0_ref[rows, :] + u1_ref[rows, :]) * _gelu_tanh(g_ref[0, rows, :].astype(F32))
        o_ref[0, rows, :] = y.astype(BF16)


def _lru(proj3, conv_w, conv_b, w_cat, b_cat, lam, tc=512):
    b, s, _ = proj3.shape
    kernel = functools.partial(_lru_kernel, seq=s, tc=tc)
    nb = LRU_BLOCKS_PER_STEP
    width = nb * LANES
    return pl.pallas_call(
        kernel,
        out_shape=jax.ShapeDtypeStruct((b, s, LRU_WIDTH), BF16),
        grid=(b, LRU_BLOCKS // nb),
        in_specs=[
            pl.BlockSpec((1, s, width), lambda bi, ni: (bi, 0, LRUX_COL // nb + ni)),
            pl.BlockSpec((1, s, width), lambda bi, ni: (bi, 0, LRUG_COL // nb + ni)),
            pl.BlockSpec((4, width), lambda bi, ni: (0, ni)),
            pl.BlockSpec((1, width), lambda bi, ni: (0, ni)),
            pl.BlockSpec((nb, LRU_BLOCK_DIM, 4 * LRU_BLOCK_DIM), lambda bi, ni: (ni, 0, 0)),
            pl.BlockSpec((nb, 1, 4 * LRU_BLOCK_DIM), lambda bi, ni: (ni, 0, 0)),
            pl.BlockSpec((2, width), lambda bi, ni: (0, ni)),
        ],
        out_specs=pl.BlockSpec((1, s, width), lambda bi, ni: (bi, 0, ni)),
        scratch_shapes=[pltpu.VMEM((s + 2 * SUBLANES, width), F32)]
        + [pltpu.VMEM((s, width), F32)] * 4,
        compiler_params=pltpu.CompilerParams(
            dimension_semantics=("arbitrary", "arbitrary"), vmem_limit_bytes=VMEM_LIMIT),
        name="rglru",
    )(proj3, proj3, conv_w, conv_b, w_cat, b_cat, lam)


def _merge_kernel(x_ref, ao_ref, yl_ref, ga_ref, gl_ref, bg_ref, wa_ref, wl_ref, wo_ref,
                  n2_ref, wr_ref, br_ref, x1_ref, h2_ref, cw_ref):
    attn_o = jnp.concatenate([ao_ref[0, h] for h in range(N_HEADS)], axis=1)
    attn_d = jnp.dot(attn_o, wa_ref[...], preferred_element_type=F32)
    lru_d = jnp.dot(yl_ref[...], wl_ref[...], preferred_element_type=F32)
    bg = bg_ref[...]
    g_attn = jax.nn.sigmoid(ga_ref[...].astype(F32) + bg[:, 0:D_MODEL])
    g_lru = jax.nn.sigmoid(gl_ref[...].astype(F32) + bg[:, D_MODEL:2 * D_MODEL])
    merged = g_attn * attn_d + g_lru * lru_d
    x1 = x_ref[...] + jnp.dot(merged.astype(BF16), wo_ref[...], preferred_element_type=F32)
    x1_ref[...] = x1
    h2 = _rms(x1, n2_ref[...])
    h2_ref[...] = h2.astype(BF16)

    h2_hi = h2.astype(BF16)
    h2_lo = (h2 - h2_hi.astype(F32)).astype(BF16)
    wr = wr_ref[...]
    part = jnp.dot(h2_hi, wr, preferred_element_type=F32)
    logits = (part[:, 0:ROUTER_LANES] + part[:, ROUTER_LANES:]
              + jnp.dot(h2_lo, wr[:, 0:ROUTER_LANES], preferred_element_type=F32) + br_ref[...])
    lane = lax.broadcasted_iota(jnp.int32, logits.shape, 1)
    neg = jnp.full_like(logits, -jnp.inf)
    big = jnp.full_like(lane, ROUTER_LANES)

    def masked_max(mask):
        return jnp.max(jnp.where(mask, logits, neg), axis=-1, keepdims=True)

    def first_lane(mask, value):
        return jnp.min(jnp.where(mask & (logits == value), lane, big), axis=-1, keepdims=True)

    g_mask = lane < N_GROUPS
    g_max = masked_max(g_mask)
    g_sel = first_lane(g_mask, g_max)
    g_w = 1.0 / jnp.sum(jnp.where(g_mask, jnp.exp(logits - g_max), 0.0), axis=-1, keepdims=True)
    e_lo = EXPERT_LANE0 + g_sel * EXPERTS_PER_GROUP
    e_mask = (lane >= e_lo) & (lane < e_lo + EXPERTS_PER_GROUP)
    v1 = masked_max(e_mask)
    i1 = first_lane(e_mask, v1)
    e_mask2 = e_mask & (lane != i1)
    v2 = masked_max(e_mask2)
    i2 = first_lane(e_mask2, v2)
    t = jnp.exp(v2 - v1)
    w1 = g_w / (1.0 + t)
    w2 = g_w * t / (1.0 + t)
    cw_ref[...] = (jnp.where(lane == i1, w1, 0.0) + jnp.where(lane == i2, w2, 0.0)
                   + jnp.where(lane == GROUP_ID_LANE, g_sel.astype(F32), 0.0))


def _merge(x2, attn_o, y_lru, proj, b_gates, wa, wl, wo, n2_gain, w_router, b_router, tm=512):
    n = x2.shape[0]
    blocks_per_seq = attn_o.shape[2] // tm
    row = lambda cols, col_blk=0: pl.BlockSpec((tm, cols), lambda i: (i, col_blk))
    const = lambda shape: pl.BlockSpec(shape, lambda i: (0, 0))
    heads = pl.BlockSpec((1, N_HEADS, tm, V_HEAD),
                         lambda i: (i // blocks_per_seq, 0, i % blocks_per_seq, 0))
    return pl.pallas_call(
        _merge_kernel,
        out_shape=(jax.ShapeDtypeStruct((n, D_MODEL), F32),
                   jax.ShapeDtypeStruct((n, D_MODEL), BF16),
                   jax.ShapeDtypeStruct((n, ROUTER_LANES), F32)),
        grid=(n // tm,),
        in_specs=[
            row(D_MODEL), heads, row(LRU_WIDTH),
            row(D_MODEL, GATE_COL_1024), row(D_MODEL, GATE_COL_1024 + 1),
            const((1, 2 * D_MODEL)),
            const((ATTN_WIDTH, D_MODEL)), const((LRU_WIDTH, D_MODEL)), const((D_MODEL, D_MODEL)),
            const((1, D_MODEL)), const((D_MODEL, 2 * ROUTER_LANES)), const((1, ROUTER_LANES)),
        ],
        out_specs=(row(D_MODEL), row(D_MODEL), row(ROUTER_LANES)),
        compiler_params=pltpu.CompilerParams(
            dimension_semantics=("arbitrary",), vmem_limit_bytes=VMEM_LIMIT),
        name="merge_router",
    )(x2, attn_o, y_lru, proj, proj, b_gates, wa, wl, wo, n2_gain, w_router, b_router)


MOE_CHUNK = 256
MOE_CHUNK_SHIFT = 8


def _moe_kernel(h_ref, rt_ref, x1_ref, wg_ref, wu_ref, wd_ref, o_ref,
                xs_ref, ys_ref, pos_ref, rng_ref):
    g = pl.program_id(1)
    t_tile = h_ref.shape[0]

    @pl.when(g == 0)
    def _sort_rows():
        rt = rt_ref[...]
        lane = lax.broadcasted_iota(jnp.int32, rt.shape, 1)
        g_sel = jnp.sum(jnp.where(lane == GROUP_ID_LANE, rt, 0.0), axis=-1, keepdims=True)
        onehot = lane == g_sel.astype(jnp.int32)
        r = lax.broadcasted_iota(jnp.int32, (t_tile, t_tile), 0)
        c = lax.broadcasted_iota(jnp.int32, (t_tile, t_tile), 1)
        earlier = jnp.dot((r > c).astype(BF16), onehot.astype(BF16),
                          preferred_element_type=F32)
        cnt = jnp.sum(onehot.astype(F32), axis=0, keepdims=True)
        lane1 = lax.broadcasted_iota(jnp.int32, cnt.shape, 1)
        off = jnp.zeros_like(cnt)
        for gg in range(N_GROUPS - 1):
            off = off + jnp.where(lane1 > gg, cnt[:, gg:gg + 1], 0.0)
        pos = jnp.sum(jnp.where(onehot, earlier + off, 0.0), axis=-1, keepdims=True)
        pos_b = jnp.broadcast_to(pos, rt.shape)
        pos_ref[...] = pos_b
        pos_row = pos_b.T[0:1, :]
        perm = (r.astype(F32) == pos_row).astype(BF16)
        rt_hi = rt.astype(BF16)
        rt_lo = (rt - rt_hi.astype(F32)).astype(BF16)
        cat = jnp.concatenate([h_ref[...], rt_hi, rt_lo], axis=1)
        xs_ref[...] = jnp.dot(perm, cat, preferred_element_type=F32).astype(BF16)
        ys_ref[...] = jnp.zeros(ys_ref.shape, F32)
        for gg in range(N_GROUPS):
            rng_ref[gg] = jnp.sum(jnp.where(lane1 == gg, off, 0.0)).astype(jnp.int32)
            rng_ref[N_GROUPS + gg] = jnp.sum(jnp.where(lane1 == gg, cnt, 0.0)).astype(jnp.int32)

    start = rng_ref[g]
    count = rng_ref[N_GROUPS + g]
    lo = lax.shift_right_logical(start, MOE_CHUNK_SHIFT)
    hi = jnp.where(count > 0,
                   lax.shift_right_logical(start + count + (MOE_CHUNK - 1), MOE_CHUNK_SHIFT), lo)

    def chunk(ci, carry):
        r0 = pl.multiple_of(ci * MOE_CHUNK, MOE_CHUNK)
        xc = xs_ref[pl.ds(r0, MOE_CHUNK), 0:D_MODEL]
        rs = (xs_ref[pl.ds(r0, MOE_CHUNK), D_MODEL:D_MODEL + ROUTER_LANES].astype(F32)
              + xs_ref[pl.ds(r0, MOE_CHUNK), D_MODEL + ROUTER_LANES:].astype(F32))
        lane = lax.broadcasted_iota(jnp.int32, rs.shape, 1)
        y = jnp.zeros((MOE_CHUNK, D_MODEL), F32)
        for e in range(EXPERTS_PER_GROUP):
            a = jnp.dot(xc, wg_ref[e], preferred_element_type=F32)
            u = jnp.dot(xc, wu_ref[e], preferred_element_type=F32)
            ce = jnp.sum(jnp.where(lane == EXPERT_LANE0 + g * EXPERTS_PER_GROUP + e, rs, 0.0),
                         axis=-1, keepdims=True)
            hid = a * jax.nn.sigmoid(a) * u * ce
            y = y + jnp.dot(hid.astype(BF16), wd_ref[e], preferred_element_type=F32)
        ys_ref[pl.ds(r0, MOE_CHUNK), :] += y
        return carry

    lax.fori_loop(lo, hi, chunk, 0)

    @pl.when(g == N_GROUPS - 1)
    def _unsort_rows():
        pos_b = pos_ref[...]
        c = lax.broadcasted_iota(jnp.int32, (t_tile, t_tile), 1).astype(F32)
        unperm = (jnp.concatenate([pos_b] * (t_tile // LANES), axis=1) == c).astype(BF16)
        o_ref[...] = x1_ref[...] + jnp.dot(unperm, ys_ref[...].astype(BF16),
                                           preferred_element_type=F32)


def _moe(h2, route, x1, wg, wu, wd, tm=1024):
    n = h2.shape[0]
    group_w = lambda rows, cols: pl.BlockSpec((EXPERTS_PER_GROUP, rows, cols),
                                              lambda i, g: (g, 0, 0))
    return pl.pallas_call(
        _moe_kernel,
        out_shape=jax.ShapeDtypeStruct((n, D_MODEL), F32),
        grid=(n // tm, N_GROUPS),
        in_specs=[
            pl.BlockSpec((tm, D_MODEL), lambda i, g: (i, 0)),
            pl.BlockSpec((tm, ROUTER_LANES), lambda i, g: (i, 0)),
            pl.BlockSpec((tm, D_MODEL), lambda i, g: (i, 0)),
            group_w(D_MODEL, EXPERT_HIDDEN), group_w(D_MODEL, EXPERT_HIDDEN),
            group_w(EXPERT_HIDDEN, D_MODEL),
        ],
        out_specs=pl.BlockSpec((tm, D_MODEL), lambda i, g: (i, 0)),
        scratch_shapes=[
            pltpu.VMEM((tm, D_MODEL + 2 * ROUTER_LANES), BF16),
            pltpu.VMEM((tm, D_MODEL), F32),
            pltpu.VMEM((tm, ROUTER_LANES), F32),
            pltpu.SMEM((2 * N_GROUPS,), jnp.int32),
        ],
        compiler_params=pltpu.CompilerParams(
            dimension_semantics=("arbitrary", "arbitrary"), vmem_limit_bytes=VMEM_LIMIT),
        name="moe",
    )(h2, route, x1, wg, wu, wd)


def _rope_tables(seq):
    pos = jnp.arange(seq, dtype=F32)
    inv_freq = ROPE_THETA ** (-jnp.arange(0, SUB_HEAD, 2, dtype=F32) / SUB_HEAD)
    ang = pos[:, None] * inv_freq[None, :]
    cos, sin = jnp.cos(ang), jnp.sin(ang)
    cos_full = jnp.concatenate([cos, cos, cos, cos], axis=-1)
    sin_signed = jnp.concatenate([-sin, sin, -sin, sin], axis=-1)
    return cos_full, sin_signed


def kernel(x, norm1_gain, w_in, b_gates, q_norm_gain, k_norm_gain, lambda_q1, lambda_k1, lambda_q2, lambda_k2, attn_subln_gain, w_attn_o, conv_w, conv_b, lru_wa, lru_ba, lru_wi, lru_bi, lru_lambda, w_lru_o, w_out, norm2_gain, w_group_router, b_group_router, w_expert_router, b_expert_router, w_expert_gate, w_expert_up, w_expert_down):
    b, s, d = x.shape
    n = b * s
    depth = w_in.shape[0]
    assert depth == 1 and d == D_MODEL
    cos, sin_signed = _rope_tables(s)
    x2 = x.reshape(n, d)
    l = 0

    sub_heads = ATTN_WIDTH // SUB_HEAD
    q_hm, kt_hm, v_hm, rest = _inproj(
        x, norm1_gain[l][None, :], w_in[l].astype(BF16),
        jnp.tile(q_norm_gain[l], sub_heads)[None, :], jnp.tile(k_norm_gain[l], sub_heads)[None, :],
        cos, sin_signed)
    proj3 = rest.reshape(b, s, REST_WIDTH)

    lam_params = jnp.stack([lambda_q1[l], lambda_k1[l], lambda_q2[l], lambda_k2[l]])
    attn_o = _attention(q_hm, kt_hm, v_hm, lam_params, attn_subln_gain[l][None, :])

    w_cat = (0.5 * jnp.concatenate([lru_wa[l, 0], lru_wi[l, 0], lru_wa[l, 1], lru_wi[l, 1]],
                                   axis=-1)).astype(BF16)
    blk = lambda v: v.reshape(LRU_BLOCKS, 1, LRU_BLOCK_DIM)
    b_cat = 0.5 * jnp.concatenate([blk(lru_ba[l, 0]), blk(lru_bi[l, 0]),
                                   blk(lru_ba[l, 1]), blk(lru_bi[l, 1])], axis=-1)
    y_lru = _lru(proj3, conv_w[l], conv_b[l][None, :], w_cat, b_cat, lru_lambda[l])

    pad = ROUTER_LANES - N_GROUPS - N_EXPERTS
    w_router = jnp.concatenate([w_group_router[l], w_expert_router[l],
                                jnp.zeros((d, pad), F32)], axis=-1)
    b_router = jnp.concatenate([b_group_router[l], b_expert_router[l],
                                jnp.zeros((pad,), F32)])[None, :]
    w_router_hi = w_router.astype(BF16)
    w_router = jnp.concatenate(
        [w_router_hi, (w_router - w_router_hi.astype(F32)).astype(BF16)], axis=-1)
    x1, h2, cw = _merge(
        x2, attn_o, y_lru.reshape(n, LRU_WIDTH), rest,
        b_gates[l][None, :], w_attn_o[l].astype(BF16), w_lru_o[l].astype(BF16),
        w_out[l].astype(BF16), norm2_gain[l][None, :], w_router, b_router)

    out = _moe(h2, cw, x1, w_expert_gate[l].astype(BF16), w_expert_up[l].astype(BF16),
               w_expert_down[l].astype(BF16))
    return out.reshape(b, s, d)
```

```python
import functools
import math

import jax
import jax.numpy as jnp
from jax import lax
from jax.experimental import pallas as pl
from jax.experimental.pallas import tpu as pltpu

F32 = jnp.float32
BF16 = jnp.bfloat16

D_MODEL = 1024
N_HEADS = 8
SUB_HEAD = 64
V_HEAD = 128
ATTN_WIDTH = 1024
LRU_WIDTH = 1024
LRU_BLOCKS = 8
LRU_BLOCK_DIM = 128
LRU_C = 8.0
PROJ_WIDTH = 7168
N_GROUPS = 4
EXPERTS_PER_GROUP = 4
N_EXPERTS = 16
EXPERT_HIDDEN = 512
ROPE_THETA = 10000.0
RMS_EPS = 1e-6
LAMBDA_INIT = 0.8 - 0.6 * math.exp(-0.3 * 0)

LOG2_E = math.log2(math.e)
LANES = 128
SUBLANES = 8
VMEM_LIMIT = 56 * 1024 * 1024

REST_WIDTH = PROJ_WIDTH - 3 * ATTN_WIDTH
LRUX_COL, LRUG_COL = 0, 8
GATE_COL_1024 = 2

ROUTER_LANES = 128
EXPERT_LANE0 = N_GROUPS
GROUP_ID_LANE = N_GROUPS + N_EXPERTS


def _rms(x, gain):
    ms = jnp.mean(x * x, axis=-1, keepdims=True)
    return x * lax.rsqrt(ms + RMS_EPS) * gain


MXU_TILE = 256


def _subhead_norm_rope(x, gain, cos, sin_signed):
    width = x.shape[1]
    r = lax.broadcasted_iota(jnp.int32, (MXU_TILE, MXU_TILE), 0) // SUB_HEAD
    c = lax.broadcasted_iota(jnp.int32, (MXU_TILE, MXU_TILE), 1) // SUB_HEAD
    group_mean = jnp.where(r == c, 1.0 / SUB_HEAD, 0.0).astype(BF16)
    xx = (x * x).astype(BF16)
    ms = jnp.concatenate(
        [jnp.dot(xx[:, t * MXU_TILE:(t + 1) * MXU_TILE], group_mean, preferred_element_type=F32)
         for t in range(width // MXU_TILE)], axis=1)
    xn = x * lax.rsqrt(ms + RMS_EPS) * gain
    lane = lax.broadcasted_iota(jnp.int32, xn.shape, 1)
    first_half = (lane % SUB_HEAD) < (SUB_HEAD // 2)
    partner = jnp.where(first_half,
                        pltpu.roll(xn, width - SUB_HEAD // 2, axis=1),
                        pltpu.roll(xn, SUB_HEAD // 2, axis=1))
    reps = width // LANES
    return (xn * jnp.concatenate([cos] * reps, axis=1)
            + partner * jnp.concatenate([sin_signed] * reps, axis=1))


def _inproj_kernel(x_ref, g_ref, w_ref, qg_ref, kg_ref, cos_ref, sin_ref,
                   q_ref, kt_ref, v_ref, rest_ref):
    h = _rms(x_ref[...], g_ref[...]).astype(BF16)
    cos, sin_signed = cos_ref[...], sin_ref[...]

    def proj(j):
        return jnp.dot(h, w_ref[:, j * 1024:(j + 1) * 1024], preferred_element_type=F32)

    q = _subhead_norm_rope(proj(0), qg_ref[...], cos, sin_signed) * (SUB_HEAD ** -0.5 * LOG2_E)
    k = _subhead_norm_rope(proj(1), kg_ref[...], cos, sin_signed)
    v = proj(2).astype(BF16)
    for hd in range(N_HEADS):
        cols = slice(hd * LANES, (hd + 1) * LANES)
        q_ref[0, hd] = q[:, cols].astype(BF16)
        kt_ref[0, hd] = k[:, cols].T.astype(BF16)
        v_ref[0, hd] = v[:, cols]
    for j in range(3, PROJ_WIDTH // 1024):
        rest_ref[:, (j - 3) * 1024:(j - 2) * 1024] = proj(j).astype(BF16)


def _inproj(x3, gain, w_bf16, q_gain, k_gain, cos, sin_signed, tm=512):
    b, s, d = x3.shape
    n = b * s
    bps = s // tm
    const = lambda shape: pl.BlockSpec(shape, lambda i: (0, 0))
    table = pl.BlockSpec((tm, LANES), lambda i: (i % bps, 0))
    heads = pl.BlockSpec((1, N_HEADS, tm, LANES), lambda i: (i // bps, 0, i % bps, 0))
    return pl.pallas_call(
        _inproj_kernel,
        out_shape=(jax.ShapeDtypeStruct((b, N_HEADS, s, LANES), BF16),
                   jax.ShapeDtypeStruct((b, N_HEADS, LANES, s), BF16),
                   jax.ShapeDtypeStruct((b, N_HEADS, s, V_HEAD), BF16),
                   jax.ShapeDtypeStruct((n, REST_WIDTH), BF16)),
        grid=(n // tm,),
        in_specs=[
            pl.BlockSpec((tm, D_MODEL), lambda i: (i, 0)),
            const((1, D_MODEL)),
            const((D_MODEL, PROJ_WIDTH)),
            const((1, ATTN_WIDTH)), const((1, ATTN_WIDTH)), table, table,
        ],
        out_specs=(heads,
                   pl.BlockSpec((1, N_HEADS, LANES, tm), lambda i: (i // bps, 0, 0, i % bps)),
                   heads,
                   pl.BlockSpec((tm, REST_WIDTH), lambda i: (i, 0))),
        compiler_params=pltpu.CompilerParams(
            dimension_semantics=("arbitrary",), vmem_limit_bytes=VMEM_LIMIT),
        name="inproj",
    )(x3.reshape(n, d), gain, w_bf16, q_gain, k_gain, cos, sin_signed)


ATTN_SUB_ROWS = 128


def _attn_kernel(lam_ref, sg_ref, q_ref, kt_ref, v_ref, o_ref, *, tq):
    lp = lam_ref[...]
    lam = (jnp.exp(jnp.sum(lp[0:1] * lp[1:2], axis=-1, keepdims=True))
           - jnp.exp(jnp.sum(lp[2:3] * lp[3:4], axis=-1, keepdims=True)) + LAMBDA_INIT)

    ts = ATTN_SUB_ROWS
    for t in range(tq // ts):
        q = q_ref[0, 0, t * ts:(t + 1) * ts, :]
        lane = lax.broadcasted_iota(jnp.int32, q.shape, 1)
        zero = jnp.zeros_like(q)
        qq = jnp.concatenate([jnp.where(lane < SUB_HEAD, q, zero),
                              jnp.where(lane >= SUB_HEAD, q, zero)], axis=0)
        s = jnp.dot(qq, kt_ref[0, 0], preferred_element_type=F32)
        m = jnp.max(s, axis=-1, keepdims=True)
        p = jnp.exp2(s - m)
        l = jnp.sum(p, axis=-1, keepdims=True)
        acc = jnp.dot(p.astype(BF16), v_ref[0, 0], preferred_element_type=F32)
        o = acc / l
        o = o[0:ts] - lam * o[ts:2 * ts]
        o = _rms(o, sg_ref[...]) * (1.0 - LAMBDA_INIT)
        o_ref[0, 0, t * ts:(t + 1) * ts, :] = o.astype(BF16)


def _attention(q_hm, kt_hm, v_hm, lam_params, subln_gain, tq=1024):
    b, h, s, _ = q_hm.shape
    const = lambda shape: pl.BlockSpec(shape, lambda bi, hi, qi: (0, 0))
    return pl.pallas_call(
        functools.partial(_attn_kernel, tq=tq),
        out_shape=jax.ShapeDtypeStruct((b, h, s, V_HEAD), BF16),
        grid=(b, h, s // tq),
        in_specs=[
            const((4, SUB_HEAD)), const((1, V_HEAD)),
            pl.BlockSpec((1, 1, tq, LANES), lambda bi, hi, qi: (bi, hi, qi, 0)),
            pl.BlockSpec((1, 1, LANES, s), lambda bi, hi, qi: (bi, hi, 0, 0)),
            pl.BlockSpec((1, 1, s, V_HEAD), lambda bi, hi, qi: (bi, hi, 0, 0)),
        ],
        out_specs=pl.BlockSpec((1, 1, tq, V_HEAD), lambda bi, hi, qi: (bi, hi, qi, 0)),
        compiler_params=pltpu.CompilerParams(
            dimension_semantics=("arbitrary", "arbitrary", "arbitrary"),
            vmem_limit_bytes=VMEM_LIMIT),
        name="diff_attn",
    )(lam_params, subln_gain, q_hm, kt_hm, v_hm)


def _softplus(x):
    return jnp.maximum(x, 0.0) + jnp.log1p(jnp.exp(-jnp.abs(x)))


def _gelu_tanh(x):
    return 0.5 * x * (1.0 + jnp.tanh(math.sqrt(2.0 / math.pi) * (x + 0.044715 * (x * x * x))))


LRU_BLOCKS_PER_STEP = 2


def _lru_kernel(x_ref, g_ref, cw_ref, cb_ref, w_ref, b_ref, lam_ref, o_ref,
                xs_ref, a0_ref, u0_ref, a1_ref, u1_ref, *, seq, tc):
    pad = SUBLANES
    width = LRU_BLOCKS_PER_STEP * LANES
    zeros_pad = jnp.zeros((pad, width), F32)
    xs_ref[0:pad, :] = zeros_pad
    xs_ref[pad + seq:pad + seq + pad, :] = zeros_pad
    xs_ref[pad:pad + seq, :] = x_ref[0].astype(F32)

    k_all = (-LRU_C * 0.5 * LOG2_E) * _softplus(-lam_ref[...])
    cw_all = cw_ref[...]
    cb_all = cb_ref[...]

    n_seg = SUBLANES
    seg = seq // n_seg
    assert tc == seg
    for c in range(n_seg):
        base = pad + c * tc
        for j in range(LRU_BLOCKS_PER_STEP):
            cols = slice(j * LANES, (j + 1) * LANES)
            cw, cb = cw_all[:, cols], cb_all[:, cols]
            win = xs_ref[base - pad:base + tc + pad, cols]
            n_win = tc + 2 * pad
            taps = (pltpu.roll(win, 1, axis=0), win,
                    pltpu.roll(win, n_win - 1, axis=0), pltpu.roll(win, n_win - 2, axis=0))
            xr = cb + sum(cw[t:t + 1] * taps[t][pad:pad + tc] for t in range(4))
            th = jnp.tanh(jnp.dot(xr.astype(BF16), w_ref[j], preferred_element_type=F32)
                          + b_ref[j])
            for d, (a_ref, u_ref) in enumerate(((a0_ref, u0_ref), (a1_ref, u1_ref))):
                k = k_all[d:d + 1, cols]
                a = jnp.exp2(k * th[:, (2 * d) * LANES:(2 * d + 1) * LANES] + k)
                gate_i = 0.5 * th[:, (2 * d + 1) * LANES:(2 * d + 2) * LANES] + 0.5
                v = 1.0 - a * a
                mult = jnp.where(v > 0.0, v * lax.rsqrt(v), 0.0)
                a_ref[j, pl.ds(c, seg, stride=SUBLANES), :] = a
                u_ref[j, pl.ds(c, seg, stride=SUBLANES), :] = mult * (gate_i * xr)

    chains = [(a0_ref, u0_ref, j, False) for j in range(LRU_BLOCKS_PER_STEP)] \
        + [(a1_ref, u1_ref, j, True) for j in range(LRU_BLOCKS_PER_STEP)]

    def step(t, carry):
        out = []
        for (a_ref, u_ref, j, reverse), (h, p) in zip(chains, carry):
            r0 = pl.multiple_of((seg - 1 - t if reverse else t) * SUBLANES, SUBLANES)
            a = a_ref[j, pl.ds(r0, SUBLANES), :]
            h = a * h + u_ref[j, pl.ds(r0, SUBLANES), :]
            p = a * p
            u_ref[j, pl.ds(r0, SUBLANES), :] = h
            a_ref[j, pl.ds(r0, SUBLANES), :] = p
            out.append((h, p))
        return tuple(out)

    zero = jnp.zeros((SUBLANES, LANES), F32)
    one = jnp.ones((SUBLANES, LANES), F32)
    ends = lax.fori_loop(0, seg, step, ((zero, one),) * len(chains), unroll=8)

    row = lax.broadcasted_iota(jnp.int32, (SUBLANES, LANES), 0)
    entering = []
    for (_, _, _, reverse), (h_end, p_end) in zip(chains, ends):
        state = zero
        order = range(SUBLANES - 2, -1, -1) if reverse else range(1, SUBLANES)
        for s in order:
            nxt = pltpu.roll(h_end + p_end * state, SUBLANES - 1 if reverse else 1, axis=0)
            state = jnp.where(row == s, nxt, state)
        entering.append(state)

    for c in range(n_seg):
        rows = slice(c * tc, (c + 1) * tc)
        for j in range(LRU_BLOCKS_PER_STEP):
            cols = slice(j * LANES, (j + 1) * LANES)
            hs = 0.0
            for (a_ref, u_ref, jj, _), state in zip(chains, entering):
                if jj == j:
                    hs = hs + (u_ref[j, pl.ds(c, seg, stride=SUBLANES), :]
                               + a_ref[j, pl.ds(c, seg, stride=SUBLANES), :] * state[c:c + 1, :])
            y = hs * _gelu_tanh(g_ref[0, rows, cols].astype(F32))
            o_ref[0, rows, cols] = y.astype(BF16)


def _lru(proj3, conv_w, conv_b, w_cat, b_cat, lam):
    b, s, _ = proj3.shape
    tc = s // SUBLANES
    kernel = functools.partial(_lru_kernel, seq=s, tc=tc)
    nb = LRU_BLOCKS_PER_STEP
    width = nb * LANES
    return pl.pallas_call(
        kernel,
        out_shape=jax.ShapeDtypeStruct((b, s, LRU_WIDTH), BF16),
        grid=(b, LRU_BLOCKS // nb),
        in_specs=[
            pl.BlockSpec((1, s, width), lambda bi, ni: (bi, 0, LRUX_COL // nb + ni)),
            pl.BlockSpec((1, s, width), lambda bi, ni: (bi, 0, LRUG_COL // nb + ni)),
            pl.BlockSpec((4, width), lambda bi, ni: (0, ni)),
            pl.BlockSpec((1, width), lambda bi, ni: (0, ni)),
            pl.BlockSpec((nb, LRU_BLOCK_DIM, 4 * LRU_BLOCK_DIM), lambda bi, ni: (ni, 0, 0)),
            pl.BlockSpec((nb, 1, 4 * LRU_BLOCK_DIM), lambda bi, ni: (ni, 0, 0)),
            pl.BlockSpec((2, width), lambda bi, ni: (0, ni)),
        ],
        out_specs=pl.BlockSpec((1, s, width), lambda bi, ni: (bi, 0, ni)),
        scratch_shapes=[pltpu.VMEM((s + 2 * SUBLANES, width), F32)]
        + [pltpu.VMEM((nb, s, LANES), F32)] * 4,
        compiler_params=pltpu.CompilerParams(
            dimension_semantics=("arbitrary", "arbitrary"), vmem_limit_bytes=VMEM_LIMIT),
        name="rglru",
    )(proj3, proj3, conv_w, conv_b, w_cat, b_cat, lam)


def _merge_kernel(x_ref, ao_ref, yl_ref, ga_ref, gl_ref, bg_ref, wa_ref, wl_ref, wo_ref,
                  n2_ref, wr_ref, br_ref, x1_ref, h2_ref, cw_ref):
    attn_o = jnp.concatenate([ao_ref[0, h] for h in range(N_HEADS)], axis=1)
    attn_d = jnp.dot(attn_o, wa_ref[...], preferred_element_type=F32)
    lru_d = jnp.dot(yl_ref[...], wl_ref[...], preferred_element_type=F32)
    bg = bg_ref[...]
    g_attn = jax.nn.sigmoid(ga_ref[...].astype(F32) + bg[:, 0:D_MODEL])
    g_lru = jax.nn.sigmoid(gl_ref[...].astype(F32) + bg[:, D_MODEL:2 * D_MODEL])
    merged = g_attn * attn_d + g_lru * lru_d
    x1 = x_ref[...] + jnp.dot(merged.astype(BF16), wo_ref[...], preferred_element_type=F32)
    x1_ref[...] = x1
    h2 = _rms(x1, n2_ref[...])
    h2_ref[...] = h2.astype(BF16)

    h2_hi = h2.astype(BF16)
    h2_lo = (h2 - h2_hi.astype(F32)).astype(BF16)
    wr = wr_ref[...]
    part = jnp.dot(h2_hi, wr, preferred_element_type=F32)
    logits = (part[:, 0:ROUTER_LANES] + part[:, ROUTER_LANES:]
              + jnp.dot(h2_lo, wr[:, 0:ROUTER_LANES], preferred_element_type=F32) + br_ref[...])
    lane = lax.broadcasted_iota(jnp.int32, logits.shape, 1)
    neg = jnp.full_like(logits, -jnp.inf)
    big = jnp.full_like(lane, ROUTER_LANES)

    def masked_max(mask):
        return jnp.max(jnp.where(mask, logits, neg), axis=-1, keepdims=True)

    def first_lane(mask, value):
        return jnp.min(jnp.where(mask & (logits == value), lane, big), axis=-1, keepdims=True)

    g_mask = lane < N_GROUPS
    g_max = masked_max(g_mask)
    g_sel = first_lane(g_mask, g_max)
    g_w = 1.0 / jnp.sum(jnp.where(g_mask, jnp.exp(logits - g_max), 0.0), axis=-1, keepdims=True)
    e_lo = EXPERT_LANE0 + g_sel * EXPERTS_PER_GROUP
    e_mask = (lane >= e_lo) & (lane < e_lo + EXPERTS_PER_GROUP)
    v1 = masked_max(e_mask)
    i1 = first_lane(e_mask, v1)
    e_mask2 = e_mask & (lane != i1)
    v2 = masked_max(e_mask2)
    i2 = first_lane(e_mask2, v2)
    t = jnp.exp(v2 - v1)
    w1 = g_w / (1.0 + t)
    w2 = g_w * t / (1.0 + t)
    cw_ref[...] = (jnp.where(lane == i1, w1, 0.0) + jnp.where(lane == i2, w2, 0.0)
                   + jnp.where(lane == GROUP_ID_LANE, g_sel.astype(F32), 0.0))


def _merge(x2, attn_o, y_lru, proj, b_gates, wa, wl, wo, n2_gain, w_router, b_router, tm=512):
    n = x2.shape[0]
    blocks_per_seq = attn_o.shape[2] // tm
    row = lambda cols, col_blk=0: pl.BlockSpec((tm, cols), lambda i: (i, col_blk))
    const = lambda shape: pl.BlockSpec(shape, lambda i: (0, 0))
    heads = pl.BlockSpec((1, N_HEADS, tm, V_HEAD),
                         lambda i: (i // blocks_per_seq, 0, i % blocks_per_seq, 0))
    return pl.pallas_call(
        _merge_kernel,
        out_shape=(jax.ShapeDtypeStruct((n, D_MODEL), F32),
                   jax.ShapeDtypeStruct((n, D_MODEL), BF16),
                   jax.ShapeDtypeStruct((n, ROUTER_LANES), F32)),
        grid=(n // tm,),
        in_specs=[
            row(D_MODEL), heads, row(LRU_WIDTH),
            row(D_MODEL, GATE_COL_1024), row(D_MODEL, GATE_COL_1024 + 1),
            const((1, 2 * D_MODEL)),
            const((ATTN_WIDTH, D_MODEL)), const((LRU_WIDTH, D_MODEL)), const((D_MODEL, D_MODEL)),
            const((1, D_MODEL)), const((D_MODEL, 2 * ROUTER_LANES)), const((1, ROUTER_LANES)),
        ],
        out_specs=(row(D_MODEL), row(D_MODEL), row(ROUTER_LANES)),
        compiler_params=pltpu.CompilerParams(
            dimension_semantics=("arbitrary",), vmem_limit_bytes=VMEM_LIMIT),
        name="merge_router",
    )(x2, attn_o, y_lru, proj, proj, b_gates, wa, wl, wo, n2_gain, w_router, b_router)


MOE_ALIGN = 128
MOE_ALIGN_SHIFT = 7
MOE_CHUNK = 384
MOE_MAX_WINDOWS = 3


def _moe_kernel(h_ref, rt_ref, x1_ref, wg_ref, wu_ref, wd_ref, o_ref,
                xs_ref, ys_ref, pos_ref, rng_ref):
    g = pl.program_id(1)
    t_tile = h_ref.shape[0]

    @pl.when(g == 0)
    def _sort_rows():
        rt = rt_ref[...]
        lane = lax.broadcasted_iota(jnp.int32, rt.shape, 1)
        g_sel = jnp.sum(jnp.where(lane == GROUP_ID_LANE, rt, 0.0), axis=-1, keepdims=True)
        onehot = lane == g_sel.astype(jnp.int32)
        r = lax.broadcasted_iota(jnp.int32, (t_tile, t_tile), 0)
        c = lax.broadcasted_iota(jnp.int32, (t_tile, t_tile), 1)
        earlier = jnp.dot((r > c).astype(BF16), onehot.astype(BF16),
                          preferred_element_type=F32)
        cnt = jnp.sum(onehot.astype(F32), axis=0, keepdims=True)
        lane1 = lax.broadcasted_iota(jnp.int32, cnt.shape, 1)
        off = jnp.zeros_like(cnt)
        for gg in range(N_GROUPS - 1):
            off = off + jnp.where(lane1 > gg, cnt[:, gg:gg + 1], 0.0)
        pos = jnp.sum(jnp.where(onehot, earlier + off, 0.0), axis=-1, keepdims=True)
        pos_b = jnp.broadcast_to(pos, rt.shape)
        pos_ref[...] = pos_b
        pos_row = pos_b.T[0:1, :]
        perm = (r.astype(F32) == pos_row).astype(BF16)
        rt_hi = rt.astype(BF16)
        rt_lo = (rt - rt_hi.astype(F32)).astype(BF16)
        cat = jnp.concatenate([h_ref[...], rt_hi, rt_lo], axis=1)
        xs_ref[0:t_tile, :] = jnp.dot(perm, cat, preferred_element_type=F32).astype(BF16)
        xs_ref[t_tile:, :] = jnp.zeros((xs_ref.shape[0] - t_tile, xs_ref.shape[1]), BF16)
        ys_ref[...] = jnp.zeros(ys_ref.shape, F32)
        for gg in range(N_GROUPS):
            rng_ref[gg] = jnp.sum(jnp.where(lane1 == gg, off, 0.0)).astype(jnp.int32)
            rng_ref[N_GROUPS + gg] = jnp.sum(jnp.where(lane1 == gg, cnt, 0.0)).astype(jnp.int32)

    start = rng_ref[g]
    count = rng_ref[N_GROUPS + g]
    first = lax.shift_left(lax.shift_right_logical(start, MOE_ALIGN_SHIFT), MOE_ALIGN_SHIFT)
    span = start + count - first
    n_windows = jnp.where(count > 0, 1, 0)
    for w in range(1, MOE_MAX_WINDOWS):
        n_windows = n_windows + jnp.where(span > w * MOE_CHUNK, 1, 0)

    def chunk(ci, carry):
        r0 = pl.multiple_of(first + ci * MOE_CHUNK, MOE_ALIGN)
        xc = xs_ref[pl.ds(r0, MOE_CHUNK), 0:D_MODEL]
        rs = (xs_ref[pl.ds(r0, MOE_CHUNK), D_MODEL:D_MODEL + ROUTER_LANES].astype(F32)
              + xs_ref[pl.ds(r0, MOE_CHUNK), D_MODEL + ROUTER_LANES:].astype(F32))
        lane = lax.broadcasted_iota(jnp.int32, rs.shape, 1)
        y = jnp.zeros((MOE_CHUNK, D_MODEL), F32)
        for e in range(EXPERTS_PER_GROUP):
            a = jnp.dot(xc, wg_ref[e], preferred_element_type=F32)
            u = jnp.dot(xc, wu_ref[e], preferred_element_type=F32)
            ce = jnp.sum(jnp.where(lane == EXPERT_LANE0 + g * EXPERTS_PER_GROUP + e, rs, 0.0),
                         axis=-1, keepdims=True)
            hid = a * jax.nn.sigmoid(a) * u * ce
            y = y + jnp.dot(hid.astype(BF16), wd_ref[e], preferred_element_type=F32)
        ys_ref[pl.ds(r0, MOE_CHUNK), :] += y
        return carry

    lax.fori_loop(0, n_windows, chunk, 0)

    @pl.when(g == N_GROUPS - 1)
    def _unsort_rows():
        pos_b = pos_ref[...]
        c = lax.broadcasted_iota(jnp.int32, (t_tile, t_tile), 1).astype(F32)
        unperm = (jnp.concatenate([pos_b] * (t_tile // LANES), axis=1) == c).astype(BF16)
        o_ref[...] = x1_ref[...] + jnp.dot(unperm, ys_ref[0:t_tile, :].astype(BF16),
                                           preferred_element_type=F32)


def _moe(h2, route, x1, wg, wu, wd, tm=1024):
    n = h2.shape[0]
    group_w = lambda rows, cols: pl.BlockSpec((EXPERTS_PER_GROUP, rows, cols),
                                              lambda i, g: (g, 0, 0))
    return pl.pallas_call(
        _moe_kernel,
        out_shape=jax.ShapeDtypeStruct((n, D_MODEL), F32),
        grid=(n // tm, N_GROUPS),
        in_specs=[
            pl.BlockSpec((tm, D_MODEL), lambda i, g: (i, 0)),
            pl.BlockSpec((tm, ROUTER_LANES), lambda i, g: (i, 0)),
            pl.BlockSpec((tm, D_MODEL), lambda i, g: (i, 0), pipeline_mode=pl.Buffered(1)),
            group_w(D_MODEL, EXPERT_HIDDEN), group_w(D_MODEL, EXPERT_HIDDEN),
            group_w(EXPERT_HIDDEN, D_MODEL),
        ],
        out_specs=pl.BlockSpec((tm, D_MODEL), lambda i, g: (i, 0)),
        scratch_shapes=[
            pltpu.VMEM((tm + MOE_CHUNK, D_MODEL + 2 * ROUTER_LANES), BF16),
            pltpu.VMEM((tm + MOE_CHUNK, D_MODEL), F32),
            pltpu.VMEM((tm, ROUTER_LANES), F32),
            pltpu.SMEM((2 * N_GROUPS,), jnp.int32),
        ],
        compiler_params=pltpu.CompilerParams(
            dimension_semantics=("arbitrary", "arbitrary"), vmem_limit_bytes=VMEM_LIMIT),
        name="moe",
    )(h2, route, x1, wg, wu, wd)


def _rope_tables(seq):
    pos = jnp.arange(seq, dtype=F32)
    inv_freq = ROPE_THETA ** (-jnp.arange(0, SUB_HEAD, 2, dtype=F32) / SUB_HEAD)
    ang = pos[:, None] * inv_freq[None, :]
    cos, sin = jnp.cos(ang), jnp.sin(ang)
    cos_full = jnp.concatenate([cos, cos, cos, cos], axis=-1)
    sin_signed = jnp.concatenate([-sin, sin, -sin, sin], axis=-1)
    return cos_full, sin_signed


def kernel(x, norm1_gain, w_in, b_gates, q_norm_gain, k_norm_gain, lambda_q1, lambda_k1, lambda_q2, lambda_k2, attn_subln_gain, w_attn_o, conv_w, conv_b, lru_wa, lru_ba, lru_wi, lru_bi, lru_lambda, w_lru_o, w_out, norm2_gain, w_group_router, b_group_router, w_expert_router, b_expert_router, w_expert_gate, w_expert_up, w_expert_down):
    b, s, d = x.shape
    n = b * s
    depth = w_in.shape[0]
    assert depth == 1 and d == D_MODEL
    cos, sin_signed = _rope_tables(s)
    x2 = x.reshape(n, d)
    l = 0

    sub_heads = ATTN_WIDTH // SUB_HEAD
    q_hm, kt_hm, v_hm, rest = _inproj(
        x, norm1_gain[l][None, :], w_in[l].astype(BF16),
        jnp.tile(q_norm_gain[l], sub_heads)[None, :], jnp.tile(k_norm_gain[l], sub_heads)[None, :],
        cos, sin_signed)
    proj3 = rest.reshape(b, s, REST_WIDTH)

    lam_params = jnp.stack([lambda_q1[l], lambda_k1[l], lambda_q2[l], lambda_k2[l]])
    attn_o = _attention(q_hm, kt_hm, v_hm, lam_params, attn_subln_gain[l][None, :])

    w_cat = (0.5 * jnp.concatenate([lru_wa[l, 0], lru_wi[l, 0], lru_wa[l, 1], lru_wi[l, 1]],
                                   axis=-1)).astype(BF16)
    blk = lambda v: v.reshape(LRU_BLOCKS, 1, LRU_BLOCK_DIM)
    b_cat = 0.5 * jnp.concatenate([blk(lru_ba[l, 0]), blk(lru_bi[l, 0]),
                                   blk(lru_ba[l, 1]), blk(lru_bi[l, 1])], axis=-1)
    y_lru = _lru(proj3, conv_w[l], conv_b[l][None, :], w_cat, b_cat, lru_lambda[l])

    pad = ROUTER_LANES - N_GROUPS - N_EXPERTS
    w_router = jnp.concatenate([w_group_router[l], w_expert_router[l],
                                jnp.zeros((d, pad), F32)], axis=-1)
    b_router = jnp.concatenate([b_group_router[l], b_expert_router[l],
                                jnp.zeros((pad,), F32)])[None, :]
    w_router_hi = w_router.astype(BF16)
    w_router = jnp.concatenate(
        [w_router_hi, (w_router - w_router_hi.astype(F32)).astype(BF16)], axis=-1)
    x1, h2, cw = _merge(
        x2, attn_o, y_lru.reshape(n, LRU_WIDTH), rest,
        b_gates[l][None, :], w_attn_o[l].astype(BF16), w_lru_o[l].astype(BF16),
        w_out[l].astype(BF16), norm2_gain[l][None, :], w_router, b_router)

    out = _moe(h2, cw, x1, w_expert_gate[l].astype(BF16), w_expert_up[l].astype(BF16),
               w_expert_down[l].astype(BF16))
    return out.reshape(b, s, d)
```

```python
import functools
import math

import jax
import jax.numpy as jnp
from jax import lax
from jax.experimental import pallas as pl
from jax.experimental.pallas import tpu as pltpu

F32 = jnp.float32
BF16 = jnp.bfloat16

D_MODEL = 1024
N_HEADS = 8
SUB_HEAD = 64
V_HEAD = 128
ATTN_WIDTH = 1024
LRU_WIDTH = 1024
LRU_BLOCKS = 8
LRU_BLOCK_DIM = 128
LRU_C = 8.0
PROJ_WIDTH = 7168
N_GROUPS = 4
EXPERTS_PER_GROUP = 4
N_EXPERTS = 16
EXPERT_HIDDEN = 512
ROPE_THETA = 10000.0
RMS_EPS = 1e-6
LAMBDA_INIT = 0.8 - 0.6 * math.exp(-0.3 * 0)

LOG2_E = math.log2(math.e)
LANES = 128
SUBLANES = 8
VMEM_LIMIT = 56 * 1024 * 1024

REST_WIDTH = PROJ_WIDTH - 3 * ATTN_WIDTH
LRUX_COL, LRUG_COL = 0, 8
GATE_COL_1024 = 2

ROUTER_LANES = 128
EXPERT_LANE0 = N_GROUPS
GROUP_ID_LANE = N_GROUPS + N_EXPERTS


def _rms(x, gain):
    ms = jnp.mean(x * x, axis=-1, keepdims=True)
    return x * lax.rsqrt(ms + RMS_EPS) * gain


MXU_TILE = 256


def _subhead_norm_rope(x, gain, cos, sin_signed):
    width = x.shape[1]
    r = lax.broadcasted_iota(jnp.int32, (MXU_TILE, MXU_TILE), 0) // SUB_HEAD
    c = lax.broadcasted_iota(jnp.int32, (MXU_TILE, MXU_TILE), 1) // SUB_HEAD
    group_mean = jnp.where(r == c, 1.0 / SUB_HEAD, 0.0).astype(BF16)
    xx = (x * x).astype(BF16)
    ms = jnp.concatenate(
        [jnp.dot(xx[:, t * MXU_TILE:(t + 1) * MXU_TILE], group_mean, preferred_element_type=F32)
         for t in range(width // MXU_TILE)], axis=1)
    xn = x * lax.rsqrt(ms + RMS_EPS) * gain
    lane = lax.broadcasted_iota(jnp.int32, xn.shape, 1)
    first_half = (lane % SUB_HEAD) < (SUB_HEAD // 2)
    partner = jnp.where(first_half,
                        pltpu.roll(xn, width - SUB_HEAD // 2, axis=1),
                        pltpu.roll(xn, SUB_HEAD // 2, axis=1))
    reps = width // LANES
    return (xn * jnp.concatenate([cos] * reps, axis=1)
            + partner * jnp.concatenate([sin_signed] * reps, axis=1))


def _inproj_kernel(x_ref, g_ref, w_ref, qg_ref, kg_ref, cos_ref, sin_ref,
                   q_ref, kt_ref, v_ref, rest_ref):
    h = _rms(x_ref[...], g_ref[...]).astype(BF16)
    cos, sin_signed = cos_ref[...], sin_ref[...]

    def proj(j):
        return jnp.dot(h, w_ref[:, j * 1024:(j + 1) * 1024], preferred_element_type=F32)

    q = _subhead_norm_rope(proj(0), qg_ref[...], cos, sin_signed) * (SUB_HEAD ** -0.5 * LOG2_E)
    k = _subhead_norm_rope(proj(1), kg_ref[...], cos, sin_signed)
    v = proj(2).astype(BF16)
    for hd in range(N_HEADS):
        cols = slice(hd * LANES, (hd + 1) * LANES)
        q_ref[0, hd] = q[:, cols].astype(BF16)
        kt_ref[0, hd] = k[:, cols].T.astype(BF16)
        v_ref[0, hd] = v[:, cols]
    for j in range(3, PROJ_WIDTH // 1024):
        rest_ref[:, (j - 3) * 1024:(j - 2) * 1024] = proj(j).astype(BF16)


def _inproj(x3, gain, w_bf16, q_gain, k_gain, cos, sin_signed, tm=512):
    b, s, d = x3.shape
    n = b * s
    bps = s // tm
    const = lambda shape: pl.BlockSpec(shape, lambda i: (0, 0))
    table = pl.BlockSpec((tm, LANES), lambda i: (i % bps, 0))
    heads = pl.BlockSpec((1, N_HEADS, tm, LANES), lambda i: (i // bps, 0, i % bps, 0))
    return pl.pallas_call(
        _inproj_kernel,
        out_shape=(jax.ShapeDtypeStruct((b, N_HEADS, s, LANES), BF16),
                   jax.ShapeDtypeStruct((b, N_HEADS, LANES, s), BF16),
                   jax.ShapeDtypeStruct((b, N_HEADS, s, V_HEAD), BF16),
                   jax.ShapeDtypeStruct((n, REST_WIDTH), BF16)),
        grid=(n // tm,),
        in_specs=[
            pl.BlockSpec((tm, D_MODEL), lambda i: (i, 0)),
            const((1, D_MODEL)),
            const((D_MODEL, PROJ_WIDTH)),
            const((1, ATTN_WIDTH)), const((1, ATTN_WIDTH)), table, table,
        ],
        out_specs=(heads,
                   pl.BlockSpec((1, N_HEADS, LANES, tm), lambda i: (i // bps, 0, 0, i % bps)),
                   heads,
                   pl.BlockSpec((tm, REST_WIDTH), lambda i: (i, 0))),
        compiler_params=pltpu.CompilerParams(
            dimension_semantics=("arbitrary",), vmem_limit_bytes=VMEM_LIMIT),
        name="inproj",
    )(x3.reshape(n, d), gain, w_bf16, q_gain, k_gain, cos, sin_signed)


ATTN_SUB_ROWS = 128


SCORE_BOUND_LOG2 = 100.0


def _attn_kernel(lam_ref, sg_ref, qg_ref, kg_ref, q_ref, kt_ref, v_ref, o_ref, *, tq):
    lp = lam_ref[...]
    lam = (jnp.exp(jnp.sum(lp[0:1] * lp[1:2], axis=-1, keepdims=True))
           - jnp.exp(jnp.sum(lp[2:3] * lp[3:4], axis=-1, keepdims=True)) + LAMBDA_INIT)

    score_bound = (SUB_HEAD * (SUB_HEAD ** -0.5 * LOG2_E) * 1.01
                   * jnp.max(jnp.abs(qg_ref[...])) * jnp.max(jnp.abs(kg_ref[...])))
    ts = ATTN_SUB_ROWS

    def chains(subtract_max):
        for t in range(tq // ts):
            q = q_ref[0, 0, t * ts:(t + 1) * ts, :]
            lane = lax.broadcasted_iota(jnp.int32, q.shape, 1)
            zero = jnp.zeros_like(q)
            qq = jnp.concatenate([jnp.where(lane < SUB_HEAD, q, zero),
                                  jnp.where(lane >= SUB_HEAD, q, zero)], axis=0)
            s = jnp.dot(qq, kt_ref[0, 0], preferred_element_type=F32)
            if subtract_max:
                s = s - jnp.max(s, axis=-1, keepdims=True)
            p = jnp.exp2(s)
            l = jnp.sum(p, axis=-1, keepdims=True)
            acc = jnp.dot(p.astype(BF16), v_ref[0, 0], preferred_element_type=F32)
            o = acc / l
            o = o[0:ts] - lam * o[ts:2 * ts]
            o = _rms(o, sg_ref[...]) * (1.0 - LAMBDA_INIT)
            o_ref[0, 0, t * ts:(t + 1) * ts, :] = o.astype(BF16)

    @pl.when(score_bound <= SCORE_BOUND_LOG2)
    def _bounded_scores():
        chains(subtract_max=False)

    @pl.when(jnp.logical_not(score_bound <= SCORE_BOUND_LOG2))
    def _any_scores():
        chains(subtract_max=True)


def _attention(q_hm, kt_hm, v_hm, lam_params, subln_gain, q_gain, k_gain, tq=1024):
    b, h, s, _ = q_hm.shape
    const = lambda shape: pl.BlockSpec(shape, lambda bi, hi, qi: (0, 0))
    return pl.pallas_call(
        functools.partial(_attn_kernel, tq=tq),
        out_shape=jax.ShapeDtypeStruct((b, h, s, V_HEAD), BF16),
        grid=(b, h, s // tq),
        in_specs=[
            const((4, SUB_HEAD)), const((1, V_HEAD)), const((1, ATTN_WIDTH)), const((1, ATTN_WIDTH)),
            pl.BlockSpec((1, 1, tq, LANES), lambda bi, hi, qi: (bi, hi, qi, 0)),
            pl.BlockSpec((1, 1, LANES, s), lambda bi, hi, qi: (bi, hi, 0, 0)),
            pl.BlockSpec((1, 1, s, V_HEAD), lambda bi, hi, qi: (bi, hi, 0, 0)),
        ],
        out_specs=pl.BlockSpec((1, 1, tq, V_HEAD), lambda bi, hi, qi: (bi, hi, qi, 0)),
        compiler_params=pltpu.CompilerParams(
            dimension_semantics=("arbitrary", "arbitrary", "arbitrary"),
            vmem_limit_bytes=VMEM_LIMIT),
        name="diff_attn",
    )(lam_params, subln_gain, q_gain, k_gain, q_hm, kt_hm, v_hm)


def _softplus(x):
    return jnp.maximum(x, 0.0) + jnp.log1p(jnp.exp(-jnp.abs(x)))


def _gelu_tanh(x):
    return 0.5 * x * (1.0 + jnp.tanh(math.sqrt(2.0 / math.pi) * (x + 0.044715 * (x * x * x))))


LRU_BLOCKS_PER_STEP = 2


def _lru_kernel(x_ref, g_ref, cw_ref, cb_ref, w_ref, b_ref, lam_ref, o_ref,
                xs_ref, a0_ref, u0_ref, a1_ref, u1_ref, *, seq, tc):
    pad = SUBLANES
    width = LRU_BLOCKS_PER_STEP * LANES
    zeros_pad = jnp.zeros((pad, width), F32)
    xs_ref[0:pad, :] = zeros_pad
    xs_ref[pad + seq:pad + seq + pad, :] = zeros_pad
    xs_ref[pad:pad + seq, :] = x_ref[0].astype(F32)

    k_all = (-LRU_C * 0.5 * LOG2_E) * _softplus(-lam_ref[...])
    cw_all = cw_ref[...]
    cb_all = cb_ref[...]

    n_seg = SUBLANES
    seg = seq // n_seg
    assert tc == seg
    for c in range(n_seg):
        base = pad + c * tc
        for j in range(LRU_BLOCKS_PER_STEP):
            cols = slice(j * LANES, (j + 1) * LANES)
            cw, cb = cw_all[:, cols], cb_all[:, cols]
            win = xs_ref[base - pad:base + tc + pad, cols]
            n_win = tc + 2 * pad
            taps = (pltpu.roll(win, 1, axis=0), win,
                    pltpu.roll(win, n_win - 1, axis=0), pltpu.roll(win, n_win - 2, axis=0))
            xr = cb + sum(cw[t:t + 1] * taps[t][pad:pad + tc] for t in range(4))
            th = jnp.tanh(jnp.dot(xr.astype(BF16), w_ref[j], preferred_element_type=F32)
                          + b_ref[j])
            for d, (a_ref, u_ref) in enumerate(((a0_ref, u0_ref), (a1_ref, u1_ref))):
                k = k_all[d:d + 1, cols]
                a = jnp.exp2(k * th[:, (2 * d) * LANES:(2 * d + 1) * LANES] + k)
                gate_i = 0.5 * th[:, (2 * d + 1) * LANES:(2 * d + 2) * LANES] + 0.5
                v = 1.0 - a * a
                mult = jnp.where(v > 0.0, v * lax.rsqrt(v), 0.0)
                a_ref[j, pl.ds(c, seg, stride=SUBLANES), :] = a
                u_ref[j, pl.ds(c, seg, stride=SUBLANES), :] = mult * (gate_i * xr)

    chains = [(a0_ref, u0_ref, j, False) for j in range(LRU_BLOCKS_PER_STEP)] \
        + [(a1_ref, u1_ref, j, True) for j in range(LRU_BLOCKS_PER_STEP)]

    def step(t, carry):
        out = []
        for (a_ref, u_ref, j, reverse), (h, p) in zip(chains, carry):
            r0 = pl.multiple_of((seg - 1 - t if reverse else t) * SUBLANES, SUBLANES)
            a = a_ref[j, pl.ds(r0, SUBLANES), :]
            h = a * h + u_ref[j, pl.ds(r0, SUBLANES), :]
            p = a * p
            u_ref[j, pl.ds(r0, SUBLANES), :] = h
            a_ref[j, pl.ds(r0, SUBLANES), :] = p
            out.append((h, p))
        return tuple(out)

    zero = jnp.zeros((SUBLANES, LANES), F32)
    one = jnp.ones((SUBLANES, LANES), F32)
    ends = lax.fori_loop(0, seg, step, ((zero, one),) * len(chains), unroll=8)

    row = lax.broadcasted_iota(jnp.int32, (SUBLANES, LANES), 0)
    entering = []
    for (_, _, _, reverse), (h_end, p_end) in zip(chains, ends):
        state = zero
        order = range(SUBLANES - 2, -1, -1) if reverse else range(1, SUBLANES)
        for s in order:
            nxt = pltpu.roll(h_end + p_end * state, SUBLANES - 1 if reverse else 1, axis=0)
            state = jnp.where(row == s, nxt, state)
        entering.append(state)

    for c in range(n_seg):
        rows = slice(c * tc, (c + 1) * tc)
        for j in range(LRU_BLOCKS_PER_STEP):
            cols = slice(j * LANES, (j + 1) * LANES)
            hs = 0.0
            for (a_ref, u_ref, jj, _), state in zip(chains, entering):
                if jj == j:
                    hs = hs + (u_ref[j, pl.ds(c, seg, stride=SUBLANES), :]
                               + a_ref[j, pl.ds(c, seg, stride=SUBLANES), :] * state[c:c + 1, :])
            y = hs * _gelu_tanh(g_ref[0, rows, cols].astype(F32))
            o_ref[0, rows, cols] = y.astype(BF16)


def _lru(proj3, conv_w, conv_b, w_cat, b_cat, lam):
    b, s, _ = proj3.shape
    tc = s // SUBLANES
    kernel = functools.partial(_lru_kernel, seq=s, tc=tc)
    nb = LRU_BLOCKS_PER_STEP
    width = nb * LANES
    return pl.pallas_call(
        kernel,
        out_shape=jax.ShapeDtypeStruct((b, s, LRU_WIDTH), BF16),
        grid=(b, LRU_BLOCKS // nb),
        in_specs=[
            pl.BlockSpec((1, s, width), lambda bi, ni: (bi, 0, LRUX_COL // nb + ni)),
            pl.BlockSpec((1, s, width), lambda bi, ni: (bi, 0, LRUG_COL // nb + ni)),
            pl.BlockSpec((4, width), lambda bi, ni: (0, ni)),
            pl.BlockSpec((1, width), lambda bi, ni: (0, ni)),
            pl.BlockSpec((nb, LRU_BLOCK_DIM, 4 * LRU_BLOCK_DIM), lambda bi, ni: (ni, 0, 0)),
            pl.BlockSpec((nb, 1, 4 * LRU_BLOCK_DIM), lambda bi, ni: (ni, 0, 0)),
            pl.BlockSpec((2, width), lambda bi, ni: (0, ni)),
        ],
        out_specs=pl.BlockSpec((1, s, width), lambda bi, ni: (bi, 0, ni)),
        scratch_shapes=[pltpu.VMEM((s + 2 * SUBLANES, width), F32)]
        + [pltpu.VMEM((nb, s, LANES), F32)] * 4,
        compiler_params=pltpu.CompilerParams(
            dimension_semantics=("arbitrary", "arbitrary"), vmem_limit_bytes=VMEM_LIMIT),
        name="rglru",
    )(proj3, proj3, conv_w, conv_b, w_cat, b_cat, lam)


def _merge_kernel(x_ref, ao_ref, yl_ref, ga_ref, gl_ref, bg_ref, wa_ref, wl_ref, wo_ref,
                  n2_ref, wr_ref, br_ref, x1_ref, h2_ref, cw_ref):
    attn_o = jnp.concatenate([ao_ref[0, h] for h in range(N_HEADS)], axis=1)
    attn_d = jnp.dot(attn_o, wa_ref[...], preferred_element_type=F32)
    lru_d = jnp.dot(yl_ref[...], wl_ref[...], preferred_element_type=F32)
    bg = bg_ref[...]
    g_attn = jax.nn.sigmoid(ga_ref[...].astype(F32) + bg[:, 0:D_MODEL])
    g_lru = jax.nn.sigmoid(gl_ref[...].astype(F32) + bg[:, D_MODEL:2 * D_MODEL])
    merged = g_attn * attn_d + g_lru * lru_d
    x1 = x_ref[...] + jnp.dot(merged.astype(BF16), wo_ref[...], preferred_element_type=F32)
    x1_ref[...] = x1
    h2 = _rms(x1, n2_ref[...])
    h2_ref[...] = h2.astype(BF16)

    h2_hi = h2.astype(BF16)
    h2_lo = (h2 - h2_hi.astype(F32)).astype(BF16)
    wr = wr_ref[...]
    part = jnp.dot(h2_hi, wr, preferred_element_type=F32)
    logits = (part[:, 0:ROUTER_LANES] + part[:, ROUTER_LANES:]
              + jnp.dot(h2_lo, wr[:, 0:ROUTER_LANES], preferred_element_type=F32) + br_ref[...])
    lane = lax.broadcasted_iota(jnp.int32, logits.shape, 1)
    neg = jnp.full_like(logits, -jnp.inf)
    big = jnp.full_like(lane, ROUTER_LANES)

    def masked_max(mask):
        return jnp.max(jnp.where(mask, logits, neg), axis=-1, keepdims=True)

    def first_lane(mask, value):
        return jnp.min(jnp.where(mask & (logits == value), lane, big), axis=-1, keepdims=True)

    g_mask = lane < N_GROUPS
    g_max = masked_max(g_mask)
    g_sel = first_lane(g_mask, g_max)
    g_w = 1.0 / jnp.sum(jnp.where(g_mask, jnp.exp(logits - g_max), 0.0), axis=-1, keepdims=True)
    e_lo = EXPERT_LANE0 + g_sel * EXPERTS_PER_GROUP
    e_mask = (lane >= e_lo) & (lane < e_lo + EXPERTS_PER_GROUP)
    v1 = masked_max(e_mask)
    i1 = first_lane(e_mask, v1)
    e_mask2 = e_mask & (lane != i1)
    v2 = masked_max(e_mask2)
    i2 = first_lane(e_mask2, v2)
    t = jnp.exp(v2 - v1)
    w1 = g_w / (1.0 + t)
    w2 = g_w * t / (1.0 + t)
    cw_ref[...] = (jnp.where(lane == i1, w1, 0.0) + jnp.where(lane == i2, w2, 0.0)
                   + jnp.where(lane == GROUP_ID_LANE, g_sel.astype(F32), 0.0))


def _merge(x2, attn_o, y_lru, proj, b_gates, wa, wl, wo, n2_gain, w_router, b_router, tm=512):
    n = x2.shape[0]
    blocks_per_seq = attn_o.shape[2] // tm
    row = lambda cols, col_blk=0: pl.BlockSpec((tm, cols), lambda i: (i, col_blk))
    const = lambda shape: pl.BlockSpec(shape, lambda i: (0, 0))
    heads = pl.BlockSpec((1, N_HEADS, tm, V_HEAD),
                         lambda i: (i // blocks_per_seq, 0, i % blocks_per_seq, 0))
    return pl.pallas_call(
        _merge_kernel,
        out_shape=(jax.ShapeDtypeStruct((n, D_MODEL), F32),
                   jax.ShapeDtypeStruct((n, D_MODEL), BF16),
                   jax.ShapeDtypeStruct((n, ROUTER_LANES), F32)),
        grid=(n // tm,),
        in_specs=[
            row(D_MODEL), heads, row(LRU_WIDTH),
            row(D_MODEL, GATE_COL_1024), row(D_MODEL, GATE_COL_1024 + 1),
            const((1, 2 * D_MODEL)),
            const((ATTN_WIDTH, D_MODEL)), const((LRU_WIDTH, D_MODEL)), const((D_MODEL, D_MODEL)),
            const((1, D_MODEL)), const((D_MODEL, 2 * ROUTER_LANES)), const((1, ROUTER_LANES)),
        ],
        out_specs=(row(D_MODEL), row(D_MODEL), row(ROUTER_LANES)),
        compiler_params=pltpu.CompilerParams(
            dimension_semantics=("arbitrary",), vmem_limit_bytes=VMEM_LIMIT),
        name="merge_router",
    )(x2, attn_o, y_lru, proj, proj, b_gates, wa, wl, wo, n2_gain, w_router, b_router)


MOE_ALIGN = 32
MOE_ALIGN_SHIFT = 5
MOE_CHUNK = 320
MOE_MAX_WINDOWS = 4


def _moe_kernel(h_ref, rt_ref, x1_ref, wg_ref, wu_ref, wd_ref, o_ref,
                xs_ref, ys_ref, pos_ref, rng_ref):
    g = pl.program_id(1)
    t_tile = h_ref.shape[0]

    @pl.when(g == 0)
    def _sort_rows():
        rt = rt_ref[...]
        lane = lax.broadcasted_iota(jnp.int32, rt.shape, 1)
        g_sel = jnp.sum(jnp.where(lane == GROUP_ID_LANE, rt, 0.0), axis=-1, keepdims=True)
        onehot = lane == g_sel.astype(jnp.int32)
        r = lax.broadcasted_iota(jnp.int32, (t_tile, t_tile), 0)
        c = lax.broadcasted_iota(jnp.int32, (t_tile, t_tile), 1)
        earlier = jnp.dot((r > c).astype(BF16), onehot.astype(BF16),
                          preferred_element_type=F32)
        cnt = jnp.sum(onehot.astype(F32), axis=0, keepdims=True)
        lane1 = lax.broadcasted_iota(jnp.int32, cnt.shape, 1)
        off = jnp.zeros_like(cnt)
        for gg in range(N_GROUPS - 1):
            off = off + jnp.where(lane1 > gg, cnt[:, gg:gg + 1], 0.0)
        pos = jnp.sum(jnp.where(onehot, earlier + off, 0.0), axis=-1, keepdims=True)
        pos_b = jnp.broadcast_to(pos, rt.shape)
        pos_ref[...] = pos_b
        pos_row = pos_b.T[0:1, :]
        perm = (r.astype(F32) == pos_row).astype(BF16)
        rt_hi = rt.astype(BF16)
        rt_lo = (rt - rt_hi.astype(F32)).astype(BF16)
        cat = jnp.concatenate([h_ref[...], rt_hi, rt_lo], axis=1)
        xs_ref[0:t_tile, :] = jnp.dot(perm, cat, preferred_element_type=F32).astype(BF16)
        xs_ref[t_tile:, :] = jnp.zeros((xs_ref.shape[0] - t_tile, xs_ref.shape[1]), BF16)
        ys_ref[...] = jnp.zeros(ys_ref.shape, F32)
        for gg in range(N_GROUPS):
            rng_ref[gg] = jnp.sum(jnp.where(lane1 == gg, off, 0.0)).astype(jnp.int32)
            rng_ref[N_GROUPS + gg] = jnp.sum(jnp.where(lane1 == gg, cnt, 0.0)).astype(jnp.int32)

    start = rng_ref[g]
    count = rng_ref[N_GROUPS + g]
    first = lax.shift_left(lax.shift_right_logical(start, MOE_ALIGN_SHIFT), MOE_ALIGN_SHIFT)
    span = start + count - first
    n_windows = jnp.where(count > 0, 1, 0)
    for w in range(1, MOE_MAX_WINDOWS):
        n_windows = n_windows + jnp.where(span > w * MOE_CHUNK, 1, 0)

    def chunk(ci, carry):
        r0 = pl.multiple_of(first + ci * MOE_CHUNK, MOE_ALIGN)
        xc = xs_ref[pl.ds(r0, MOE_CHUNK), 0:D_MODEL]
        rs = (xs_ref[pl.ds(r0, MOE_CHUNK), D_MODEL:D_MODEL + ROUTER_LANES].astype(F32)
              + xs_ref[pl.ds(r0, MOE_CHUNK), D_MODEL + ROUTER_LANES:].astype(F32))
        lane = lax.broadcasted_iota(jnp.int32, rs.shape, 1)
        y = jnp.zeros((MOE_CHUNK, D_MODEL), F32)
        for e in range(EXPERTS_PER_GROUP):
            a = jnp.dot(xc, wg_ref[e], preferred_element_type=F32)
            u = jnp.dot(xc, wu_ref[e], preferred_element_type=F32)
            ce = jnp.sum(jnp.where(lane == EXPERT_LANE0 + g * EXPERTS_PER_GROUP + e, rs, 0.0),
                         axis=-1, keepdims=True)
            hid = a * jax.nn.sigmoid(a) * u * ce
            y = y + jnp.dot(hid.astype(BF16), wd_ref[e], preferred_element_type=F32)
        ys_ref[pl.ds(r0, MOE_CHUNK), :] += y
        return carry

    lax.fori_loop(0, n_windows, chunk, 0)

    @pl.when(g == N_GROUPS - 1)
    def _unsort_rows():
        pos_b = pos_ref[...]
        c = lax.broadcasted_iota(jnp.int32, (t_tile, t_tile), 1).astype(F32)
        unperm = (jnp.concatenate([pos_b] * (t_tile // LANES), axis=1) == c).astype(BF16)
        o_ref[...] = x1_ref[...] + jnp.dot(unperm, ys_ref[0:t_tile, :].astype(BF16),
                                           preferred_element_type=F32)


def _moe(h2, route, x1, wg, wu, wd, tm=1024):
    n = h2.shape[0]
    group_w = lambda rows, cols: pl.BlockSpec((EXPERTS_PER_GROUP, rows, cols),
                                              lambda i, g: (g, 0, 0))
    return pl.pallas_call(
        _moe_kernel,
        out_shape=jax.ShapeDtypeStruct((n, D_MODEL), F32),
        grid=(n // tm, N_GROUPS),
        in_specs=[
            pl.BlockSpec((tm, D_MODEL), lambda i, g: (i, 0)),
            pl.BlockSpec((tm, ROUTER_LANES), lambda i, g: (i, 0)),
            pl.BlockSpec((tm, D_MODEL), lambda i, g: (i, 0), pipeline_mode=pl.Buffered(1)),
            group_w(D_MODEL, EXPERT_HIDDEN), group_w(D_MODEL, EXPERT_HIDDEN),
            group_w(EXPERT_HIDDEN, D_MODEL),
        ],
        out_specs=pl.BlockSpec((tm, D_MODEL), lambda i, g: (i, 0)),
        scratch_shapes=[
            pltpu.VMEM((tm + MOE_CHUNK, D_MODEL + 2 * ROUTER_LANES), BF16),
            pltpu.VMEM((tm + MOE_CHUNK, D_MODEL), F32),
            pltpu.VMEM((tm, ROUTER_LANES), F32),
            pltpu.SMEM((2 * N_GROUPS,), jnp.int32),
        ],
        compiler_params=pltpu.CompilerParams(
            dimension_semantics=("arbitrary", "arbitrary"), vmem_limit_bytes=VMEM_LIMIT),
        name="moe",
    )(h2, route, x1, wg, wu, wd)


def _rope_tables(seq):
    pos = jnp.arange(seq, dtype=F32)
    inv_freq = ROPE_THETA ** (-jnp.arange(0, SUB_HEAD, 2, dtype=F32) / SUB_HEAD)
    ang = pos[:, None] * inv_freq[None, :]
    cos, sin = jnp.cos(ang), jnp.sin(ang)
    cos_full = jnp.concatenate([cos, cos, cos, cos], axis=-1)
    sin_signed = jnp.concatenate([-sin, sin, -sin, sin], axis=-1)
    return cos_full, sin_signed


def kernel(x, norm1_gain, w_in, b_gates, q_norm_gain, k_norm_gain, lambda_q1, lambda_k1, lambda_q2, lambda_k2, attn_subln_gain, w_attn_o, conv_w, conv_b, lru_wa, lru_ba, lru_wi, lru_bi, lru_lambda, w_lru_o, w_out, norm2_gain, w_group_router, b_group_router, w_expert_router, b_expert_router, w_expert_gate, w_expert_up, w_expert_down):
    b, s, d = x.shape
    n = b * s
    depth = w_in.shape[0]
    assert depth == 1 and d == D_MODEL
    cos, sin_signed = _rope_tables(s)
    x2 = x.reshape(n, d)
    l = 0

    sub_heads = ATTN_WIDTH // SUB_HEAD
    q_gain = jnp.tile(q_norm_gain[l], sub_heads)[None, :]
    k_gain = jnp.tile(k_norm_gain[l], sub_heads)[None, :]
    q_hm, kt_hm, v_hm, rest = _inproj(
        x, norm1_gain[l][None, :], w_in[l].astype(BF16), q_gain, k_gain, cos, sin_signed)
    proj3 = rest.reshape(b, s, REST_WIDTH)

    lam_params = jnp.stack([lambda_q1[l], lambda_k1[l], lambda_q2[l], lambda_k2[l]])
    attn_o = _attention(q_hm, kt_hm, v_hm, lam_params, attn_subln_gain[l][None, :],
                        q_gain, k_gain)

    w_cat = (0.5 * jnp.concatenate([lru_wa[l, 0], lru_wi[l, 0], lru_wa[l, 1], lru_wi[l, 1]],
                                   axis=-1)).astype(BF16)
    blk = lambda v: v.reshape(LRU_BLOCKS, 1, LRU_BLOCK_DIM)
    b_cat = 0.5 * jnp.concatenate([blk(lru_ba[l, 0]), blk(lru_bi[l, 0]),
                                   blk(lru_ba[l, 1]), blk(lru_bi[l, 1])], axis=-1)
    y_lru = _lru(proj3, conv_w[l], conv_b[l][None, :], w_cat, b_cat, lru_lambda[l])

    pad = ROUTER_LANES - N_GROUPS - N_EXPERTS
    w_router = jnp.concatenate([w_group_router[l], w_expert_router[l],
                                jnp.zeros((d, pad), F32)], axis=-1)
    b_router = jnp.concatenate([b_group_router[l], b_expert_router[l],
                                jnp.zeros((pad,), F32)])[None, :]
    w_router_hi = w_router.astype(BF16)
    w_router = jnp.concatenate(
        [w_router_hi, (w_router - w_router_hi.astype(F32)).astype(BF16)], axis=-1)
    x1, h2, cw = _merge(
        x2, attn_o, y_lru.reshape(n, LRU_WIDTH), rest,
        b_gates[l][None, :], w_attn_o[l].astype(BF16), w_lru_o[l].astype(BF16),
        w_out[l].astype(BF16), norm2_gain[l][None, :], w_router, b_router)

    out = _moe(h2, cw, x1, w_expert_gate[l].astype(BF16), w_expert_up[l].astype(BF16),
               w_expert_down[l].astype(BF16))
    return out.reshape(b, s, d)
```

```python
import functools
import math

import jax
import jax.numpy as jnp
from jax import lax
from jax.experimental import pallas as pl
from jax.experimental.pallas import tpu as pltpu

F32 = jnp.float32
BF16 = jnp.bfloat16

D_MODEL = 1024
N_HEADS = 8
SUB_HEAD = 64
V_HEAD = 128
ATTN_WIDTH = 1024
LRU_WIDTH = 1024
LRU_BLOCKS = 8
LRU_BLOCK_DIM = 128
LRU_C = 8.0
PROJ_WIDTH = 7168
N_GROUPS = 4
EXPERTS_PER_GROUP = 4
N_EXPERTS = 16
EXPERT_HIDDEN = 512
ROPE_THETA = 10000.0
RMS_EPS = 1e-6
LAMBDA_INIT = 0.8 - 0.6 * math.exp(-0.3 * 0)

LOG2_E = math.log2(math.e)
LANES = 128
SUBLANES = 8
VMEM_LIMIT = 56 * 1024 * 1024

REST_WIDTH = PROJ_WIDTH - 3 * ATTN_WIDTH
LRUX_COL, LRUG_COL = 0, 8
GATE_COL_1024 = 2

ROUTER_LANES = 128
EXPERT_LANE0 = N_GROUPS
SORT_CLASS_LANE = N_GROUPS + N_EXPERTS
PAIR_ORDER = ((0, 3), (0, 2), (0, 1), (1, 2), (1, 3), (2, 3))
PAIRS_PER_GROUP = len(PAIR_ORDER)
EXPERT_PAIR_SPAN = ((0, 3), (2, 5), (1, 6), (0, 6))
N_SORT_CLASSES = N_GROUPS * PAIRS_PER_GROUP


def _rms(x, gain):
    ms = jnp.mean(x * x, axis=-1, keepdims=True)
    return x * lax.rsqrt(ms + RMS_EPS) * gain


MXU_TILE = 256


def _subhead_norm_rope(x, gain, cos, sin_signed):
    width = x.shape[1]
    r = lax.broadcasted_iota(jnp.int32, (MXU_TILE, MXU_TILE), 0) // SUB_HEAD
    c = lax.broadcasted_iota(jnp.int32, (MXU_TILE, MXU_TILE), 1) // SUB_HEAD
    group_mean = jnp.where(r == c, 1.0 / SUB_HEAD, 0.0).astype(BF16)
    xx = (x * x).astype(BF16)
    ms = jnp.concatenate(
        [jnp.dot(xx[:, t * MXU_TILE:(t + 1) * MXU_TILE], group_mean, preferred_element_type=F32)
         for t in range(width // MXU_TILE)], axis=1)
    xn = x * lax.rsqrt(ms + RMS_EPS) * gain
    lane = lax.broadcasted_iota(jnp.int32, xn.shape, 1)
    first_half = (lane % SUB_HEAD) < (SUB_HEAD // 2)
    partner = jnp.where(first_half,
                        pltpu.roll(xn, width - SUB_HEAD // 2, axis=1),
                        pltpu.roll(xn, SUB_HEAD // 2, axis=1))
    reps = width // LANES
    return (xn * jnp.concatenate([cos] * reps, axis=1)
            + partner * jnp.concatenate([sin_signed] * reps, axis=1))


def _inproj_kernel(x_ref, g_ref, w_ref, qg_ref, kg_ref, cos_ref, sin_ref,
                   q_ref, kt_ref, v_ref, rest_ref):
    h = _rms(x_ref[...], g_ref[...]).astype(BF16)
    cos, sin_signed = cos_ref[...], sin_ref[...]

    def proj(j):
        return jnp.dot(h, w_ref[:, j * 1024:(j + 1) * 1024], preferred_element_type=F32)

    q = _subhead_norm_rope(proj(0), qg_ref[...], cos, sin_signed) * (SUB_HEAD ** -0.5 * LOG2_E)
    k = _subhead_norm_rope(proj(1), kg_ref[...], cos, sin_signed)
    v = proj(2).astype(BF16)
    for hd in range(N_HEADS):
        cols = slice(hd * LANES, (hd + 1) * LANES)
        q_ref[0, hd] = q[:, cols].astype(BF16)
        kt_ref[0, hd] = k[:, cols].T.astype(BF16)
        v_ref[0, hd] = v[:, cols]
    for j in range(3, PROJ_WIDTH // 1024):
        rest_ref[:, (j - 3) * 1024:(j - 2) * 1024] = proj(j).astype(BF16)


def _inproj(x3, gain, w_bf16, q_gain, k_gain, cos, sin_signed, tm=512):
    b, s, d = x3.shape
    n = b * s
    bps = s // tm
    const = lambda shape: pl.BlockSpec(shape, lambda i: (0, 0))
    table = pl.BlockSpec((tm, LANES), lambda i: (i % bps, 0))
    heads = pl.BlockSpec((1, N_HEADS, tm, LANES), lambda i: (i // bps, 0, i % bps, 0))
    return pl.pallas_call(
        _inproj_kernel,
        out_shape=(jax.ShapeDtypeStruct((b, N_HEADS, s, LANES), BF16),
                   jax.ShapeDtypeStruct((b, N_HEADS, LANES, s), BF16),
                   jax.ShapeDtypeStruct((b, N_HEADS, s, V_HEAD), BF16),
                   jax.ShapeDtypeStruct((n, REST_WIDTH), BF16)),
        grid=(n // tm,),
        in_specs=[
            pl.BlockSpec((tm, D_MODEL), lambda i: (i, 0)),
            const((1, D_MODEL)),
            const((D_MODEL, PROJ_WIDTH)),
            const((1, ATTN_WIDTH)), const((1, ATTN_WIDTH)), table, table,
        ],
        out_specs=(heads,
                   pl.BlockSpec((1, N_HEADS, LANES, tm), lambda i: (i // bps, 0, 0, i % bps)),
                   heads,
                   pl.BlockSpec((tm, REST_WIDTH), lambda i: (i, 0))),
        compiler_params=pltpu.CompilerParams(
            dimension_semantics=("arbitrary",), vmem_limit_bytes=VMEM_LIMIT),
        name="inproj",
    )(x3.reshape(n, d), gain, w_bf16, q_gain, k_gain, cos, sin_signed)


ATTN_SUB_ROWS = 128


SCORE_BOUND_LOG2 = 100.0


def _attn_kernel(lam_ref, sg_ref, qg_ref, kg_ref, q_ref, kt_ref, v_ref, o_ref, *, tq):
    lp = lam_ref[...]
    lam = (jnp.exp(jnp.sum(lp[0:1] * lp[1:2], axis=-1, keepdims=True))
           - jnp.exp(jnp.sum(lp[2:3] * lp[3:4], axis=-1, keepdims=True)) + LAMBDA_INIT)

    score_bound = (SUB_HEAD * (SUB_HEAD ** -0.5 * LOG2_E) * 1.01
                   * jnp.max(jnp.abs(qg_ref[...])) * jnp.max(jnp.abs(kg_ref[...])))
    ts = ATTN_SUB_ROWS

    def chains(subtract_max):
        for t in range(tq // ts):
            q = q_ref[0, 0, t * ts:(t + 1) * ts, :]
            lane = lax.broadcasted_iota(jnp.int32, q.shape, 1)
            zero = jnp.zeros_like(q)
            qq = jnp.concatenate([jnp.where(lane < SUB_HEAD, q, zero),
                                  jnp.where(lane >= SUB_HEAD, q, zero)], axis=0)
            s = jnp.dot(qq, kt_ref[0, 0], preferred_element_type=F32)
            if subtract_max:
                s = s - jnp.max(s, axis=-1, keepdims=True)
            p = jnp.exp2(s)
            l = jnp.sum(p, axis=-1, keepdims=True)
            acc = jnp.dot(p.astype(BF16), v_ref[0, 0], preferred_element_type=F32)
            o = acc / l
            o = o[0:ts] - lam * o[ts:2 * ts]
            o = _rms(o, sg_ref[...]) * (1.0 - LAMBDA_INIT)
            o_ref[0, 0, t * ts:(t + 1) * ts, :] = o.astype(BF16)

    @pl.when(score_bound <= SCORE_BOUND_LOG2)
    def _bounded_scores():
        chains(subtract_max=False)

    @pl.when(jnp.logical_not(score_bound <= SCORE_BOUND_LOG2))
    def _any_scores():
        chains(subtract_max=True)


def _attention(q_hm, kt_hm, v_hm, lam_params, subln_gain, q_gain, k_gain, tq=1024):
    b, h, s, _ = q_hm.shape
    const = lambda shape: pl.BlockSpec(shape, lambda bi, hi, qi: (0, 0))
    return pl.pallas_call(
        functools.partial(_attn_kernel, tq=tq),
        out_shape=jax.ShapeDtypeStruct((b, h, s, V_HEAD), BF16),
        grid=(b, h, s // tq),
        in_specs=[
            const((4, SUB_HEAD)), const((1, V_HEAD)), const((1, ATTN_WIDTH)), const((1, ATTN_WIDTH)),
            pl.BlockSpec((1, 1, tq, LANES), lambda bi, hi, qi: (bi, hi, qi, 0)),
            pl.BlockSpec((1, 1, LANES, s), lambda bi, hi, qi: (bi, hi, 0, 0)),
            pl.BlockSpec((1, 1, s, V_HEAD), lambda bi, hi, qi: (bi, hi, 0, 0)),
        ],
        out_specs=pl.BlockSpec((1, 1, tq, V_HEAD), lambda bi, hi, qi: (bi, hi, qi, 0)),
        compiler_params=pltpu.CompilerParams(
            dimension_semantics=("arbitrary", "arbitrary", "arbitrary"),
            vmem_limit_bytes=VMEM_LIMIT),
        name="diff_attn",
    )(lam_params, subln_gain, q_gain, k_gain, q_hm, kt_hm, v_hm)


def _softplus(x):
    return jnp.maximum(x, 0.0) + jnp.log1p(jnp.exp(-jnp.abs(x)))


def _gelu_tanh(x):
    return 0.5 * x * (1.0 + jnp.tanh(math.sqrt(2.0 / math.pi) * (x + 0.044715 * (x * x * x))))


LRU_BLOCKS_PER_STEP = 2


def _lru_kernel(x_ref, g_ref, cw_ref, cb_ref, w_ref, b_ref, lam_ref, o_ref,
                xs_ref, a0_ref, u0_ref, a1_ref, u1_ref, *, seq, tc):
    pad = SUBLANES
    width = LRU_BLOCKS_PER_STEP * LANES
    zeros_pad = jnp.zeros((pad, width), F32)
    xs_ref[0:pad, :] = zeros_pad
    xs_ref[pad + seq:pad + seq + pad, :] = zeros_pad
    xs_ref[pad:pad + seq, :] = x_ref[0].astype(F32)

    k_all = (-LRU_C * 0.5 * LOG2_E) * _softplus(-lam_ref[...])
    cw_all = cw_ref[...]
    cb_all = cb_ref[...]

    n_seg = SUBLANES
    seg = seq // n_seg
    assert tc == seg
    for c in range(n_seg):
        base = pad + c * tc
        for j in range(LRU_BLOCKS_PER_STEP):
            cols = slice(j * LANES, (j + 1) * LANES)
            cw, cb = cw_all[:, cols], cb_all[:, cols]
            win = xs_ref[base - pad:base + tc + pad, cols]
            n_win = tc + 2 * pad
            taps = (pltpu.roll(win, 1, axis=0), win,
                    pltpu.roll(win, n_win - 1, axis=0), pltpu.roll(win, n_win - 2, axis=0))
            xr = cb + sum(cw[t:t + 1] * taps[t][pad:pad + tc] for t in range(4))
            th = jnp.tanh(jnp.dot(xr.astype(BF16), w_ref[j], preferred_element_type=F32)
                          + b_ref[j])
            for d, (a_ref, u_ref) in enumerate(((a0_ref, u0_ref), (a1_ref, u1_ref))):
                k = k_all[d:d + 1, cols]
                a = jnp.exp2(k * th[:, (2 * d) * LANES:(2 * d + 1) * LANES] + k)
                gate_i = 0.5 * th[:, (2 * d + 1) * LANES:(2 * d + 2) * LANES] + 0.5
                v = 1.0 - a * a
                mult = jnp.where(v > 0.0, v * lax.rsqrt(v), 0.0)
                a_ref[j, pl.ds(c, seg, stride=SUBLANES), :] = a
                u_ref[j, pl.ds(c, seg, stride=SUBLANES), :] = mult * (gate_i * xr)

    chains = [(a0_ref, u0_ref, j, False) for j in range(LRU_BLOCKS_PER_STEP)] \
        + [(a1_ref, u1_ref, j, True) for j in range(LRU_BLOCKS_PER_STEP)]

    def step(t, carry):
        out = []
        for (a_ref, u_ref, j, reverse), (h, p) in zip(chains, carry):
            r0 = pl.multiple_of((seg - 1 - t if reverse else t) * SUBLANES, SUBLANES)
            a = a_ref[j, pl.ds(r0, SUBLANES), :]
            h = a * h + u_ref[j, pl.ds(r0, SUBLANES), :]
            p = a * p
            u_ref[j, pl.ds(r0, SUBLANES), :] = h
            a_ref[j, pl.ds(r0, SUBLANES), :] = p
            out.append((h, p))
        return tuple(out)

    zero = jnp.zeros((SUBLANES, LANES), F32)
    one = jnp.ones((SUBLANES, LANES), F32)
    ends = lax.fori_loop(0, seg, step, ((zero, one),) * len(chains), unroll=8)

    row = lax.broadcasted_iota(jnp.int32, (SUBLANES, LANES), 0)
    entering = []
    for (_, _, _, reverse), (h_end, p_end) in zip(chains, ends):
        state = zero
        order = range(SUBLANES - 2, -1, -1) if reverse else range(1, SUBLANES)
        for s in order:
            nxt = pltpu.roll(h_end + p_end * state, SUBLANES - 1 if reverse else 1, axis=0)
            state = jnp.where(row == s, nxt, state)
        entering.append(state)

    for c in range(n_seg):
        rows = slice(c * tc, (c + 1) * tc)
        for j in range(LRU_BLOCKS_PER_STEP):
            cols = slice(j * LANES, (j + 1) * LANES)
            hs = 0.0
            for (a_ref, u_ref, jj, _), state in zip(chains, entering):
                if jj == j:
                    hs = hs + (u_ref[j, pl.ds(c, seg, stride=SUBLANES), :]
                               + a_ref[j, pl.ds(c, seg, stride=SUBLANES), :] * state[c:c + 1, :])
            y = hs * _gelu_tanh(g_ref[0, rows, cols].astype(F32))
            o_ref[0, rows, cols] = y.astype(BF16)


def _lru(proj3, conv_w, conv_b, w_cat, b_cat, lam):
    b, s, _ = proj3.shape
    tc = s // SUBLANES
    kernel = functools.partial(_lru_kernel, seq=s, tc=tc)
    nb = LRU_BLOCKS_PER_STEP
    width = nb * LANES
    return pl.pallas_call(
        kernel,
        out_shape=jax.ShapeDtypeStruct((b, s, LRU_WIDTH), BF16),
        grid=(b, LRU_BLOCKS // nb),
        in_specs=[
            pl.BlockSpec((1, s, width), lambda bi, ni: (bi, 0, LRUX_COL // nb + ni)),
            pl.BlockSpec((1, s, width), lambda bi, ni: (bi, 0, LRUG_COL // nb + ni)),
            pl.BlockSpec((4, width), lambda bi, ni: (0, ni)),
            pl.BlockSpec((1, width), lambda bi, ni: (0, ni)),
            pl.BlockSpec((nb, LRU_BLOCK_DIM, 4 * LRU_BLOCK_DIM), lambda bi, ni: (ni, 0, 0)),
            pl.BlockSpec((nb, 1, 4 * LRU_BLOCK_DIM), lambda bi, ni: (ni, 0, 0)),
            pl.BlockSpec((2, width), lambda bi, ni: (0, ni)),
        ],
        out_specs=pl.BlockSpec((1, s, width), lambda bi, ni: (bi, 0, ni)),
        scratch_shapes=[pltpu.VMEM((s + 2 * SUBLANES, width), F32)]
        + [pltpu.VMEM((nb, s, LANES), F32)] * 4,
        compiler_params=pltpu.CompilerParams(
            dimension_semantics=("arbitrary", "arbitrary"), vmem_limit_bytes=VMEM_LIMIT),
        name="rglru",
    )(proj3, proj3, conv_w, conv_b, w_cat, b_cat, lam)


def _merge_kernel(x_ref, ao_ref, yl_ref, ga_ref, gl_ref, bg_ref, wa_ref, wl_ref, wo_ref,
                  n2_ref, wr_ref, br_ref, x1_ref, h2_ref, cw_ref):
    attn_o = jnp.concatenate([ao_ref[0, h] for h in range(N_HEADS)], axis=1)
    attn_d = jnp.dot(attn_o, wa_ref[...], preferred_element_type=F32)
    lru_d = jnp.dot(yl_ref[...], wl_ref[...], preferred_element_type=F32)
    bg = bg_ref[...]
    g_attn = jax.nn.sigmoid(ga_ref[...].astype(F32) + bg[:, 0:D_MODEL])
    g_lru = jax.nn.sigmoid(gl_ref[...].astype(F32) + bg[:, D_MODEL:2 * D_MODEL])
    merged = g_attn * attn_d + g_lru * lru_d
    x1 = x_ref[...] + jnp.dot(merged.astype(BF16), wo_ref[...], preferred_element_type=F32)
    x1_ref[...] = x1
    h2 = _rms(x1, n2_ref[...])
    h2_ref[...] = h2.astype(BF16)

    h2_hi = h2.astype(BF16)
    h2_lo = (h2 - h2_hi.astype(F32)).astype(BF16)
    wr = wr_ref[...]
    part = jnp.dot(h2_hi, wr, preferred_element_type=F32)
    logits = (part[:, 0:ROUTER_LANES] + part[:, ROUTER_LANES:]
              + jnp.dot(h2_lo, wr[:, 0:ROUTER_LANES], preferred_element_type=F32) + br_ref[...])
    lane = lax.broadcasted_iota(jnp.int32, logits.shape, 1)
    neg = jnp.full_like(logits, -jnp.inf)
    big = jnp.full_like(lane, ROUTER_LANES)

    def masked_max(mask):
        return jnp.max(jnp.where(mask, logits, neg), axis=-1, keepdims=True)

    def first_lane(mask, value):
        return jnp.min(jnp.where(mask & (logits == value), lane, big), axis=-1, keepdims=True)

    g_mask = lane < N_GROUPS
    g_max = masked_max(g_mask)
    g_sel = first_lane(g_mask, g_max)
    g_w = 1.0 / jnp.sum(jnp.where(g_mask, jnp.exp(logits - g_max), 0.0), axis=-1, keepdims=True)
    e_lo = EXPERT_LANE0 + g_sel * EXPERTS_PER_GROUP
    e_mask = (lane >= e_lo) & (lane < e_lo + EXPERTS_PER_GROUP)
    v1 = masked_max(e_mask)
    i1 = first_lane(e_mask, v1)
    e_mask2 = e_mask & (lane != i1)
    v2 = masked_max(e_mask2)
    i2 = first_lane(e_mask2, v2)
    t = jnp.exp(v2 - v1)
    w1 = g_w / (1.0 + t)
    w2 = g_w * t / (1.0 + t)
    ea, eb = i1 - e_lo, i2 - e_lo
    p_lo, p_hi = jnp.minimum(ea, eb), jnp.maximum(ea, eb)
    pair_rank = jnp.where(p_lo == 0, 3 - p_hi, jnp.where(p_lo == 1, p_hi + 1, 5))
    sort_class = g_sel * PAIRS_PER_GROUP + pair_rank
    cw_ref[...] = (jnp.where(lane == i1, w1, 0.0) + jnp.where(lane == i2, w2, 0.0)
                   + jnp.where(lane == SORT_CLASS_LANE, sort_class.astype(F32), 0.0))


def _merge(x2, attn_o, y_lru, proj, b_gates, wa, wl, wo, n2_gain, w_router, b_router, tm=512):
    n = x2.shape[0]
    blocks_per_seq = attn_o.shape[2] // tm
    row = lambda cols, col_blk=0: pl.BlockSpec((tm, cols), lambda i: (i, col_blk))
    const = lambda shape: pl.BlockSpec(shape, lambda i: (0, 0))
    heads = pl.BlockSpec((1, N_HEADS, tm, V_HEAD),
                         lambda i: (i // blocks_per_seq, 0, i % blocks_per_seq, 0))
    return pl.pallas_call(
        _merge_kernel,
        out_shape=(jax.ShapeDtypeStruct((n, D_MODEL), F32),
                   jax.ShapeDtypeStruct((n, D_MODEL), BF16),
                   jax.ShapeDtypeStruct((n, ROUTER_LANES), F32)),
        grid=(n // tm,),
        in_specs=[
            row(D_MODEL), heads, row(LRU_WIDTH),
            row(D_MODEL, GATE_COL_1024), row(D_MODEL, GATE_COL_1024 + 1),
            const((1, 2 * D_MODEL)),
            const((ATTN_WIDTH, D_MODEL)), const((LRU_WIDTH, D_MODEL)), const((D_MODEL, D_MODEL)),
            const((1, D_MODEL)), const((D_MODEL, 2 * ROUTER_LANES)), const((1, ROUTER_LANES)),
        ],
        out_specs=(row(D_MODEL), row(D_MODEL), row(ROUTER_LANES)),
        compiler_params=pltpu.CompilerParams(
            dimension_semantics=("arbitrary",), vmem_limit_bytes=VMEM_LIMIT),
        name="merge_router",
    )(x2, attn_o, y_lru, proj, proj, b_gates, wa, wl, wo, n2_gain, w_router, b_router)


MOE_ALIGN = 16
MOE_ALIGN_SHIFT = 4
MOE_WINDOW_ROWS = (128, 192, 256, 320, 384, 512)
MOE_LARGEST_SHIFT = 9
MOE_ROW_PAD = MOE_WINDOW_ROWS[0]


def _moe_kernel(h_ref, rt_ref, x1_ref, wg_ref, wu_ref, wd_ref, o_ref,
                xs_ref, ys_ref, pos_ref, rng_ref):
    g = pl.program_id(1)
    t_tile = h_ref.shape[0]

    @pl.when(g == 0)
    def _sort_rows():
        rt = rt_ref[...]
        lane = lax.broadcasted_iota(jnp.int32, rt.shape, 1)
        cls = jnp.sum(jnp.where(lane == SORT_CLASS_LANE, rt, 0.0), axis=-1, keepdims=True)
        onehot = lane == cls.astype(jnp.int32)
        r = lax.broadcasted_iota(jnp.int32, (t_tile, t_tile), 0)
        c = lax.broadcasted_iota(jnp.int32, (t_tile, t_tile), 1)
        earlier = jnp.dot((r > c).astype(BF16), onehot.astype(BF16),
                          preferred_element_type=F32)
        cnt = jnp.sum(onehot.astype(F32), axis=0, keepdims=True)
        lane1 = lax.broadcasted_iota(jnp.int32, cnt.shape, 1)
        incl = cnt
        for d in (1, 2, 4, 8, 16):
            incl = incl + jnp.where(lane1 >= d, pltpu.roll(incl, d, axis=1), 0.0)
        off = incl - cnt
        pos = jnp.sum(jnp.where(onehot, earlier + off, 0.0), axis=-1, keepdims=True)
        pos_b = jnp.broadcast_to(pos, rt.shape)
        pos_ref[...] = pos_b
        pos_row = pos_b.T[0:1, :]
        perm = (r.astype(F32) == pos_row).astype(BF16)
        rt_hi = rt.astype(BF16)
        rt_lo = (rt - rt_hi.astype(F32)).astype(BF16)
        cat = jnp.concatenate([h_ref[...], rt_hi, rt_lo], axis=1)
        xs_ref[0:t_tile, :] = jnp.dot(perm, cat, preferred_element_type=F32).astype(BF16)
        xs_ref[t_tile:, :] = jnp.zeros((xs_ref.shape[0] - t_tile, xs_ref.shape[1]), BF16)
        ys_ref[...] = jnp.zeros(ys_ref.shape, F32)
        for k in range(N_SORT_CLASSES + 1):
            rng_ref[k] = jnp.sum(jnp.where(lane1 == k, off, 0.0)).astype(jnp.int32)

    def window(e, r0, rows):
        r0 = pl.multiple_of(r0, MOE_ALIGN)
        xc = xs_ref[pl.ds(r0, rows), 0:D_MODEL]
        rs = (xs_ref[pl.ds(r0, rows), D_MODEL:D_MODEL + ROUTER_LANES].astype(F32)
              + xs_ref[pl.ds(r0, rows), D_MODEL + ROUTER_LANES:].astype(F32))
        lane = lax.broadcasted_iota(jnp.int32, rs.shape, 1)
        a = jnp.dot(xc, wg_ref[e], preferred_element_type=F32)
        u = jnp.dot(xc, wu_ref[e], preferred_element_type=F32)
        ce = jnp.sum(jnp.where(lane == EXPERT_LANE0 + g * EXPERTS_PER_GROUP + e, rs, 0.0),
                     axis=-1, keepdims=True)
        hid = a * jax.nn.sigmoid(a) * u * ce
        ys_ref[pl.ds(r0, rows), :] += jnp.dot(hid.astype(BF16), wd_ref[e],
                                              preferred_element_type=F32)

    largest = MOE_WINDOW_ROWS[-1]
    for e, (pair_first, pair_last) in enumerate(EXPERT_PAIR_SPAN):
        start = rng_ref[g * PAIRS_PER_GROUP + pair_first]
        stop = rng_ref[g * PAIRS_PER_GROUP + pair_last]
        first = lax.shift_left(lax.shift_right_logical(start, MOE_ALIGN_SHIFT), MOE_ALIGN_SHIFT)
        span = jnp.where(stop > start, stop - first, 0)
        n_full = lax.shift_right_logical(span, MOE_LARGEST_SHIFT)

        def full_window(ci, carry, e=e, first=first):
            window(e, first + ci * largest, largest)
            return carry

        lax.fori_loop(0, n_full, full_window, 0)
        rem = span - lax.shift_left(n_full, MOE_LARGEST_SHIFT)
        rem_start = first + lax.shift_left(n_full, MOE_LARGEST_SHIFT)
        for lower, rows in zip((0,) + MOE_WINDOW_ROWS[:-1], MOE_WINDOW_ROWS):
            @pl.when((rem > lower) & (rem <= rows))
            def _remainder_window(e=e, rows=rows, rem_start=rem_start):
                window(e, rem_start, rows)

    @pl.when(g == N_GROUPS - 1)
    def _unsort_rows():
        pos_b = pos_ref[...]
        c = lax.broadcasted_iota(jnp.int32, (t_tile, t_tile), 1).astype(F32)
        unperm = (jnp.concatenate([pos_b] * (t_tile // LANES), axis=1) == c).astype(BF16)
        o_ref[...] = x1_ref[...] + jnp.dot(unperm, ys_ref[0:t_tile, :].astype(BF16),
                                           preferred_element_type=F32)


def _moe(h2, route, x1, wg, wu, wd, tm=1024):
    n = h2.shape[0]
    group_w = lambda rows, cols: pl.BlockSpec((EXPERTS_PER_GROUP, rows, cols),
                                              lambda i, g: (g, 0, 0))
    return pl.pallas_call(
        _moe_kernel,
        out_shape=jax.ShapeDtypeStruct((n, D_MODEL), F32),
        grid=(n // tm, N_GROUPS),
        in_specs=[
            pl.BlockSpec((tm, D_MODEL), lambda i, g: (i, 0)),
            pl.BlockSpec((tm, ROUTER_LANES), lambda i, g: (i, 0)),
            pl.BlockSpec((tm, D_MODEL), lambda i, g: (i, 0), pipeline_mode=pl.Buffered(1)),
            group_w(D_MODEL, EXPERT_HIDDEN), group_w(D_MODEL, EXPERT_HIDDEN),
            group_w(EXPERT_HIDDEN, D_MODEL),
        ],
        out_specs=pl.BlockSpec((tm, D_MODEL), lambda i, g: (i, 0)),
        scratch_shapes=[
            pltpu.VMEM((tm + MOE_ROW_PAD, D_MODEL + 2 * ROUTER_LANES), BF16),
            pltpu.VMEM((tm + MOE_ROW_PAD, D_MODEL), F32),
            pltpu.VMEM((tm, ROUTER_LANES), F32),
            pltpu.SMEM((N_SORT_CLASSES + 1,), jnp.int32),
        ],
        compiler_params=pltpu.CompilerParams(
            dimension_semantics=("arbitrary", "arbitrary"), vmem_limit_bytes=VMEM_LIMIT),
        name="moe",
    )(h2, route, x1, wg, wu, wd)


def _rope_tables(seq):
    pos = jnp.arange(seq, dtype=F32)
    inv_freq = ROPE_THETA ** (-jnp.arange(0, SUB_HEAD, 2, dtype=F32) / SUB_HEAD)
    ang = pos[:, None] * inv_freq[None, :]
    cos, sin = jnp.cos(ang), jnp.sin(ang)
    cos_full = jnp.concatenate([cos, cos, cos, cos], axis=-1)
    sin_signed = jnp.concatenate([-sin, sin, -sin, sin], axis=-1)
    return cos_full, sin_signed


def kernel(x, norm1_gain, w_in, b_gates, q_norm_gain, k_norm_gain, lambda_q1, lambda_k1, lambda_q2, lambda_k2, attn_subln_gain, w_attn_o, conv_w, conv_b, lru_wa, lru_ba, lru_wi, lru_bi, lru_lambda, w_lru_o, w_out, norm2_gain, w_group_router, b_group_router, w_expert_router, b_expert_router, w_expert_gate, w_expert_up, w_expert_down):
    b, s, d = x.shape
    n = b * s
    depth = w_in.shape[0]
    assert depth == 1 and d == D_MODEL
    cos, sin_signed = _rope_tables(s)
    x2 = x.reshape(n, d)
    l = 0

    sub_heads = ATTN_WIDTH // SUB_HEAD
    q_gain = jnp.tile(q_norm_gain[l], sub_heads)[None, :]
    k_gain = jnp.tile(k_norm_gain[l], sub_heads)[None, :]
    q_hm, kt_hm, v_hm, rest = _inproj(
        x, norm1_gain[l][None, :], w_in[l].astype(BF16), q_gain, k_gain, cos, sin_signed)
    proj3 = rest.reshape(b, s, REST_WIDTH)

    lam_params = jnp.stack([lambda_q1[l], lambda_k1[l], lambda_q2[l], lambda_k2[l]])
    attn_o = _attention(q_hm, kt_hm, v_hm, lam_params, attn_subln_gain[l][None, :],
                        q_gain, k_gain)

    w_cat = (0.5 * jnp.concatenate([lru_wa[l, 0], lru_wi[l, 0], lru_wa[l, 1], lru_wi[l, 1]],
                                   axis=-1)).astype(BF16)
    blk = lambda v: v.reshape(LRU_BLOCKS, 1, LRU_BLOCK_DIM)
    b_cat = 0.5 * jnp.concatenate([blk(lru_ba[l, 0]), blk(lru_bi[l, 0]),
                                   blk(lru_ba[l, 1]), blk(lru_bi[l, 1])], axis=-1)
    y_lru = _lru(proj3, conv_w[l], conv_b[l][None, :], w_cat, b_cat, lru_lambda[l])

    pad = ROUTER_LANES - N_GROUPS - N_EXPERTS
    w_router = jnp.concatenate([w_group_router[l], w_expert_router[l],
                                jnp.zeros((d, pad), F32)], axis=-1)
    b_router = jnp.concatenate([b_group_router[l], b_expert_router[l],
                                jnp.zeros((pad,), F32)])[None, :]
    w_router_hi = w_router.astype(BF16)
    w_router = jnp.concatenate(
        [w_router_hi, (w_router - w_router_hi.astype(F32)).astype(BF16)], axis=-1)
    x1, h2, cw = _merge(
        x2, attn_o, y_lru.reshape(n, LRU_WIDTH), rest,
        b_gates[l][None, :], w_attn_o[l].astype(BF16), w_lru_o[l].astype(BF16),
        w_out[l].astype(BF16), norm2_gain[l][None, :], w_router, b_router)

    out = _moe(h2, cw, x1, w_expert_gate[l].astype(BF16), w_expert_up[l].astype(BF16),
               w_expert_down[l].astype(BF16))
    return out.reshape(b, s, d)
```

```python
import functools
import math

import jax
import jax.numpy as jnp
from jax import lax
from jax.experimental import pallas as pl
from jax.experimental.pallas import tpu as pltpu

F32 = jnp.float32
BF16 = jnp.bfloat16

D_MODEL = 1024
N_HEADS = 8
SUB_HEAD = 64
V_HEAD = 128
ATTN_WIDTH = 1024
LRU_WIDTH = 1024
LRU_BLOCKS = 8
LRU_BLOCK_DIM = 128
LRU_C = 8.0
PROJ_WIDTH = 7168
N_GROUPS = 4
EXPERTS_PER_GROUP = 4
N_EXPERTS = 16
EXPERT_HIDDEN = 512
ROPE_THETA = 10000.0
RMS_EPS = 1e-6
LAMBDA_INIT = 0.8 - 0.6 * math.exp(-0.3 * 0)

LOG2_E = math.log2(math.e)
LANES = 128
SUBLANES = 8
VMEM_LIMIT = 56 * 1024 * 1024

REST_WIDTH = PROJ_WIDTH - 3 * ATTN_WIDTH
LRUX_COL, LRUG_COL = 0, 8
GATE_COL_1024 = 2

ROUTER_LANES = 128
EXPERT_LANE0 = N_GROUPS
SORT_CLASS_LANE = N_GROUPS + N_EXPERTS
PAIR_ORDER = ((0, 3), (0, 2), (0, 1), (1, 2), (1, 3), (2, 3))
PAIRS_PER_GROUP = len(PAIR_ORDER)
EXPERT_PAIR_SPAN = ((0, 3), (2, 5), (1, 6), (0, 6))
N_SORT_CLASSES = N_GROUPS * PAIRS_PER_GROUP


def _rms(x, gain):
    ms = jnp.mean(x * x, axis=-1, keepdims=True)
    return x * lax.rsqrt(ms + RMS_EPS) * gain


MXU_TILE = 256


def _subhead_norm_rope(x, gain, cos, sin_signed):
    width = x.shape[1]
    r = lax.broadcasted_iota(jnp.int32, (MXU_TILE, MXU_TILE), 0) // SUB_HEAD
    c = lax.broadcasted_iota(jnp.int32, (MXU_TILE, MXU_TILE), 1) // SUB_HEAD
    group_mean = jnp.where(r == c, 1.0 / SUB_HEAD, 0.0).astype(BF16)
    xx = (x * x).astype(BF16)
    ms = jnp.concatenate(
        [jnp.dot(xx[:, t * MXU_TILE:(t + 1) * MXU_TILE], group_mean, preferred_element_type=F32)
         for t in range(width // MXU_TILE)], axis=1)
    xn = x * lax.rsqrt(ms + RMS_EPS) * gain
    lane = lax.broadcasted_iota(jnp.int32, xn.shape, 1)
    first_half = (lane % SUB_HEAD) < (SUB_HEAD // 2)
    partner = jnp.where(first_half,
                        pltpu.roll(xn, width - SUB_HEAD // 2, axis=1),
                        pltpu.roll(xn, SUB_HEAD // 2, axis=1))
    reps = width // LANES
    return (xn * jnp.concatenate([cos] * reps, axis=1)
            + partner * jnp.concatenate([sin_signed] * reps, axis=1))


def _inproj_kernel(x_ref, g_ref, w_ref, qg_ref, kg_ref, cos_ref, sin_ref,
                   q_ref, kt_ref, v_ref, rest_ref):
    h = _rms(x_ref[...], g_ref[...]).astype(BF16)
    cos, sin_signed = cos_ref[...], sin_ref[...]

    def proj(j):
        return jnp.dot(h, w_ref[:, j * 1024:(j + 1) * 1024], preferred_element_type=F32)

    q = _subhead_norm_rope(proj(0), qg_ref[...], cos, sin_signed) * (SUB_HEAD ** -0.5 * LOG2_E)
    k = _subhead_norm_rope(proj(1), kg_ref[...], cos, sin_signed)
    v = proj(2).astype(BF16)
    for hd in range(N_HEADS):
        cols = slice(hd * LANES, (hd + 1) * LANES)
        q_ref[0, hd] = q[:, cols].astype(BF16)
        kt_ref[0, hd] = k[:, cols].T.astype(BF16)
        v_ref[0, hd] = v[:, cols]
    for j in range(3, PROJ_WIDTH // 1024):
        rest_ref[:, (j - 3) * 1024:(j - 2) * 1024] = proj(j).astype(BF16)


def _inproj(x3, gain, w_bf16, q_gain, k_gain, cos, sin_signed, tm=512):
    b, s, d = x3.shape
    n = b * s
    bps = s // tm
    const = lambda shape: pl.BlockSpec(shape, lambda i: (0, 0))
    table = pl.BlockSpec((tm, LANES), lambda i: (i % bps, 0))
    heads = pl.BlockSpec((1, N_HEADS, tm, LANES), lambda i: (i // bps, 0, i % bps, 0))
    return pl.pallas_call(
        _inproj_kernel,
        out_shape=(jax.ShapeDtypeStruct((b, N_HEADS, s, LANES), BF16),
                   jax.ShapeDtypeStruct((b, N_HEADS, LANES, s), BF16),
                   jax.ShapeDtypeStruct((b, N_HEADS, s, V_HEAD), BF16),
                   jax.ShapeDtypeStruct((n, REST_WIDTH), BF16)),
        grid=(n // tm,),
        in_specs=[
            pl.BlockSpec((tm, D_MODEL), lambda i: (i, 0)),
            const((1, D_MODEL)),
            const((D_MODEL, PROJ_WIDTH)),
            const((1, ATTN_WIDTH)), const((1, ATTN_WIDTH)), table, table,
        ],
        out_specs=(heads,
                   pl.BlockSpec((1, N_HEADS, LANES, tm), lambda i: (i // bps, 0, 0, i % bps)),
                   heads,
                   pl.BlockSpec((tm, REST_WIDTH), lambda i: (i, 0))),
        compiler_params=pltpu.CompilerParams(
            dimension_semantics=("arbitrary",), vmem_limit_bytes=VMEM_LIMIT),
        name="inproj",
    )(x3.reshape(n, d), gain, w_bf16, q_gain, k_gain, cos, sin_signed)


ATTN_SUB_ROWS = 128


SCORE_BOUND_LOG2 = 100.0


def _attn_kernel(lam_ref, sg_ref, qg_ref, kg_ref, q_ref, kt_ref, v_ref, o_ref, *, tq):
    lp = lam_ref[...]
    lam = (jnp.exp(jnp.sum(lp[0:1] * lp[1:2], axis=-1, keepdims=True))
           - jnp.exp(jnp.sum(lp[2:3] * lp[3:4], axis=-1, keepdims=True)) + LAMBDA_INIT)

    score_bound = (SUB_HEAD * (SUB_HEAD ** -0.5 * LOG2_E) * 1.01
                   * jnp.max(jnp.abs(qg_ref[...])) * jnp.max(jnp.abs(kg_ref[...])))
    ts = ATTN_SUB_ROWS

    def chains(subtract_max):
        for t in range(tq // ts):
            q = q_ref[0, 0, t * ts:(t + 1) * ts, :]
            lane = lax.broadcasted_iota(jnp.int32, q.shape, 1)
            zero = jnp.zeros_like(q)
            qq = jnp.concatenate([jnp.where(lane < SUB_HEAD, q, zero),
                                  jnp.where(lane >= SUB_HEAD, q, zero)], axis=0)
            s = jnp.dot(qq, kt_ref[0, 0], preferred_element_type=F32)
            if subtract_max:
                s = s - jnp.max(s, axis=-1, keepdims=True)
            p = jnp.exp2(s)
            l = jnp.sum(p, axis=-1, keepdims=True)
            acc = jnp.dot(p.astype(BF16), v_ref[0, 0], preferred_element_type=F32)
            o = acc / l
            o = o[0:ts] - lam * o[ts:2 * ts]
            o = _rms(o, sg_ref[...]) * (1.0 - LAMBDA_INIT)
            o_ref[0, 0, t * ts:(t + 1) * ts, :] = o.astype(BF16)

    @pl.when(score_bound <= SCORE_BOUND_LOG2)
    def _bounded_scores():
        chains(subtract_max=False)

    @pl.when(jnp.logical_not(score_bound <= SCORE_BOUND_LOG2))
    def _any_scores():
        chains(subtract_max=True)


def _attention(q_hm, kt_hm, v_hm, lam_params, subln_gain, q_gain, k_gain, tq=1024):
    b, h, s, _ = q_hm.shape
    const = lambda shape: pl.BlockSpec(shape, lambda bi, hi, qi: (0, 0))
    return pl.pallas_call(
        functools.partial(_attn_kernel, tq=tq),
        out_shape=jax.ShapeDtypeStruct((b, h, s, V_HEAD), BF16),
        grid=(b, h, s // tq),
        in_specs=[
            const((4, SUB_HEAD)), const((1, V_HEAD)), const((1, ATTN_WIDTH)), const((1, ATTN_WIDTH)),
            pl.BlockSpec((1, 1, tq, LANES), lambda bi, hi, qi: (bi, hi, qi, 0)),
            pl.BlockSpec((1, 1, LANES, s), lambda bi, hi, qi: (bi, hi, 0, 0)),
            pl.BlockSpec((1, 1, s, V_HEAD), lambda bi, hi, qi: (bi, hi, 0, 0)),
        ],
        out_specs=pl.BlockSpec((1, 1, tq, V_HEAD), lambda bi, hi, qi: (bi, hi, qi, 0)),
        compiler_params=pltpu.CompilerParams(
            dimension_semantics=("arbitrary", "arbitrary", "arbitrary"),
            vmem_limit_bytes=VMEM_LIMIT),
        name="diff_attn",
    )(lam_params, subln_gain, q_gain, k_gain, q_hm, kt_hm, v_hm)


def _softplus(x):
    return jnp.maximum(x, 0.0) + jnp.log1p(jnp.exp(-jnp.abs(x)))


def _gelu_tanh(x):
    return 0.5 * x * (1.0 + jnp.tanh(math.sqrt(2.0 / math.pi) * (x + 0.044715 * (x * x * x))))


LRU_BLOCKS_PER_STEP = 2


def _lru_kernel(x_ref, g_ref, cw_ref, cb_ref, w_ref, b_ref, lam_ref, o_ref,
                xs_ref, a0_ref, u0_ref, a1_ref, u1_ref, *, seq, tc):
    pad = SUBLANES
    width = LRU_BLOCKS_PER_STEP * LANES
    zeros_pad = jnp.zeros((pad, width), F32)
    xs_ref[0:pad, :] = zeros_pad
    xs_ref[pad + seq:pad + seq + pad, :] = zeros_pad
    xs_ref[pad:pad + seq, :] = x_ref[0].astype(F32)

    k_all = (-LRU_C * 0.5 * LOG2_E) * _softplus(-lam_ref[...])
    cw_all = cw_ref[...]
    cb_all = cb_ref[...]

    n_seg = SUBLANES
    seg = seq // n_seg
    assert tc == seg
    for c in range(n_seg):
        base = pad + c * tc
        for j in range(LRU_BLOCKS_PER_STEP):
            cols = slice(j * LANES, (j + 1) * LANES)
            cw, cb = cw_all[:, cols], cb_all[:, cols]
            win = xs_ref[base - pad:base + tc + pad, cols]
            n_win = tc + 2 * pad
            taps = (pltpu.roll(win, 1, axis=0), win,
                    pltpu.roll(win, n_win - 1, axis=0), pltpu.roll(win, n_win - 2, axis=0))
            xr = cb + sum(cw[t:t + 1] * taps[t][pad:pad + tc] for t in range(4))
            th = jnp.tanh(jnp.dot(xr.astype(BF16), w_ref[j], preferred_element_type=F32)
                          + b_ref[j])
            for d, (a_ref, u_ref) in enumerate(((a0_ref, u0_ref), (a1_ref, u1_ref))):
                k = k_all[d:d + 1, cols]
                a = jnp.exp2(k * th[:, (2 * d) * LANES:(2 * d + 1) * LANES] + k)
                gate_i = 0.5 * th[:, (2 * d + 1) * LANES:(2 * d + 2) * LANES] + 0.5
                v = 1.0 - a * a
                mult = jnp.where(v > 0.0, v * lax.rsqrt(v), 0.0)
                a_ref[j, pl.ds(c, seg, stride=SUBLANES), :] = a
                u_ref[j, pl.ds(c, seg, stride=SUBLANES), :] = mult * (gate_i * xr)

    chains = [(a0_ref, u0_ref, j, False) for j in range(LRU_BLOCKS_PER_STEP)] \
        + [(a1_ref, u1_ref, j, True) for j in range(LRU_BLOCKS_PER_STEP)]

    def step(t, carry):
        out = []
        for (a_ref, u_ref, j, reverse), (h, p) in zip(chains, carry):
            r0 = pl.multiple_of((seg - 1 - t if reverse else t) * SUBLANES, SUBLANES)
            a = a_ref[j, pl.ds(r0, SUBLANES), :]
            h = a * h + u_ref[j, pl.ds(r0, SUBLANES), :]
            p = a * p
            u_ref[j, pl.ds(r0, SUBLANES), :] = h
            a_ref[j, pl.ds(r0, SUBLANES), :] = p
            out.append((h, p))
        return tuple(out)

    zero = jnp.zeros((SUBLANES, LANES), F32)
    one = jnp.ones((SUBLANES, LANES), F32)
    ends = lax.fori_loop(0, seg, step, ((zero, one),) * len(chains), unroll=8)

    row = lax.broadcasted_iota(jnp.int32, (SUBLANES, LANES), 0)
    entering = []
    for (_, _, _, reverse), (h_end, p_end) in zip(chains, ends):
        state = zero
        order = range(SUBLANES - 2, -1, -1) if reverse else range(1, SUBLANES)
        for s in order:
            nxt = pltpu.roll(h_end + p_end * state, SUBLANES - 1 if reverse else 1, axis=0)
            state = jnp.where(row == s, nxt, state)
        entering.append(state)

    for c in range(n_seg):
        rows = slice(c * tc, (c + 1) * tc)
        for j in range(LRU_BLOCKS_PER_STEP):
            cols = slice(j * LANES, (j + 1) * LANES)
            hs = 0.0
            for (a_ref, u_ref, jj, _), state in zip(chains, entering):
                if jj == j:
                    hs = hs + (u_ref[j, pl.ds(c, seg, stride=SUBLANES), :]
                               + a_ref[j, pl.ds(c, seg, stride=SUBLANES), :] * state[c:c + 1, :])
            y = hs * _gelu_tanh(g_ref[0, rows, cols].astype(F32))
            o_ref[0, rows, cols] = y.astype(BF16)


def _lru(proj3, conv_w, conv_b, w_cat, b_cat, lam):
    b, s, _ = proj3.shape
    tc = s // SUBLANES
    kernel = functools.partial(_lru_kernel, seq=s, tc=tc)
    nb = LRU_BLOCKS_PER_STEP
    width = nb * LANES
    return pl.pallas_call(
        kernel,
        out_shape=jax.ShapeDtypeStruct((b, s, LRU_WIDTH), BF16),
        grid=(b, LRU_BLOCKS // nb),
        in_specs=[
            pl.BlockSpec((1, s, width), lambda bi, ni: (bi, 0, LRUX_COL // nb + ni)),
            pl.BlockSpec((1, s, width), lambda bi, ni: (bi, 0, LRUG_COL // nb + ni)),
            pl.BlockSpec((4, width), lambda bi, ni: (0, ni)),
            pl.BlockSpec((1, width), lambda bi, ni: (0, ni)),
            pl.BlockSpec((nb, LRU_BLOCK_DIM, 4 * LRU_BLOCK_DIM), lambda bi, ni: (ni, 0, 0)),
            pl.BlockSpec((nb, 1, 4 * LRU_BLOCK_DIM), lambda bi, ni: (ni, 0, 0)),
            pl.BlockSpec((2, width), lambda bi, ni: (0, ni)),
        ],
        out_specs=pl.BlockSpec((1, s, width), lambda bi, ni: (bi, 0, ni)),
        scratch_shapes=[pltpu.VMEM((s + 2 * SUBLANES, width), F32)]
        + [pltpu.VMEM((nb, s, LANES), F32)] * 4,
        compiler_params=pltpu.CompilerParams(
            dimension_semantics=("arbitrary", "arbitrary"), vmem_limit_bytes=VMEM_LIMIT),
        name="rglru",
    )(proj3, proj3, conv_w, conv_b, w_cat, b_cat, lam)


def _merge_kernel(x_ref, ao_ref, yl_ref, ga_ref, gl_ref, bg_ref, wa_ref, wl_ref, wo_ref,
                  n2_ref, wr_ref, br_ref, x1_ref, h2_ref, cw_ref):
    attn_o = jnp.concatenate([ao_ref[0, h] for h in range(N_HEADS)], axis=1)
    attn_d = jnp.dot(attn_o, wa_ref[...], preferred_element_type=F32)
    lru_d = jnp.dot(yl_ref[...], wl_ref[...], preferred_element_type=F32)
    bg = bg_ref[...]
    g_attn = jax.nn.sigmoid(ga_ref[...].astype(F32) + bg[:, 0:D_MODEL])
    g_lru = jax.nn.sigmoid(gl_ref[...].astype(F32) + bg[:, D_MODEL:2 * D_MODEL])
    merged = g_attn * attn_d + g_lru * lru_d
    x1 = x_ref[...] + jnp.dot(merged.astype(BF16), wo_ref[...], preferred_element_type=F32)
    x1_ref[...] = x1
    h2 = _rms(x1, n2_ref[...])
    h2_ref[...] = h2.astype(BF16)

    h2_hi = h2.astype(BF16)
    h2_lo = (h2 - h2_hi.astype(F32)).astype(BF16)
    wr = wr_ref[...]
    part = jnp.dot(h2_hi, wr, preferred_element_type=F32)
    logits = (part[:, 0:ROUTER_LANES] + part[:, ROUTER_LANES:]
              + jnp.dot(h2_lo, wr[:, 0:ROUTER_LANES], preferred_element_type=F32) + br_ref[...])
    lane = lax.broadcasted_iota(jnp.int32, logits.shape, 1)
    neg = jnp.full_like(logits, -jnp.inf)
    big = jnp.full_like(lane, ROUTER_LANES)

    def masked_max(mask):
        return jnp.max(jnp.where(mask, logits, neg), axis=-1, keepdims=True)

    def first_lane(mask, value):
        return jnp.min(jnp.where(mask & (logits == value), lane, big), axis=-1, keepdims=True)

    g_mask = lane < N_GROUPS
    g_max = masked_max(g_mask)
    g_sel = first_lane(g_mask, g_max)
    g_w = 1.0 / jnp.sum(jnp.where(g_mask, jnp.exp(logits - g_max), 0.0), axis=-1, keepdims=True)
    e_lo = EXPERT_LANE0 + g_sel * EXPERTS_PER_GROUP
    e_mask = (lane >= e_lo) & (lane < e_lo + EXPERTS_PER_GROUP)
    v1 = masked_max(e_mask)
    i1 = first_lane(e_mask, v1)
    e_mask2 = e_mask & (lane != i1)
    v2 = masked_max(e_mask2)
    i2 = first_lane(e_mask2, v2)
    t = jnp.exp(v2 - v1)
    w1 = g_w / (1.0 + t)
    w2 = g_w * t / (1.0 + t)
    ea, eb = i1 - e_lo, i2 - e_lo
    p_lo, p_hi = jnp.minimum(ea, eb), jnp.maximum(ea, eb)
    pair_rank = jnp.where(p_lo == 0, 3 - p_hi, jnp.where(p_lo == 1, p_hi + 1, 5))
    sort_class = g_sel * PAIRS_PER_GROUP + pair_rank
    cw_ref[...] = (jnp.where(lane == i1, w1, 0.0) + jnp.where(lane == i2, w2, 0.0)
                   + jnp.where(lane == SORT_CLASS_LANE, sort_class.astype(F32), 0.0))


def _merge(x2, attn_o, y_lru, proj, b_gates, wa, wl, wo, n2_gain, w_router, b_router, tm=512):
    n = x2.shape[0]
    blocks_per_seq = attn_o.shape[2] // tm
    row = lambda cols, col_blk=0: pl.BlockSpec((tm, cols), lambda i: (i, col_blk))
    const = lambda shape: pl.BlockSpec(shape, lambda i: (0, 0))
    heads = pl.BlockSpec((1, N_HEADS, tm, V_HEAD),
                         lambda i: (i // blocks_per_seq, 0, i % blocks_per_seq, 0))
    return pl.pallas_call(
        _merge_kernel,
        out_shape=(jax.ShapeDtypeStruct((n, D_MODEL), F32),
                   jax.ShapeDtypeStruct((n, D_MODEL), BF16),
                   jax.ShapeDtypeStruct((n, ROUTER_LANES), F32)),
        grid=(n // tm,),
        in_specs=[
            row(D_MODEL), heads, row(LRU_WIDTH),
            row(D_MODEL, GATE_COL_1024), row(D_MODEL, GATE_COL_1024 + 1),
            const((1, 2 * D_MODEL)),
            const((ATTN_WIDTH, D_MODEL)), const((LRU_WIDTH, D_MODEL)), const((D_MODEL, D_MODEL)),
            const((1, D_MODEL)), const((D_MODEL, 2 * ROUTER_LANES)), const((1, ROUTER_LANES)),
        ],
        out_specs=(row(D_MODEL), row(D_MODEL), row(ROUTER_LANES)),
        compiler_params=pltpu.CompilerParams(
            dimension_semantics=("arbitrary",), vmem_limit_bytes=VMEM_LIMIT),
        name="merge_router",
    )(x2, attn_o, y_lru, proj, proj, b_gates, wa, wl, wo, n2_gain, w_router, b_router)


MOE_ALIGN = 16
MOE_ALIGN_SHIFT = 4
MOE_WINDOW_ROWS = (192, 320, 512)
MOE_LARGEST_SHIFT = 9
MOE_ROW_PAD = MOE_WINDOW_ROWS[0]
MOE_TILES_PER_PASS = 2
MOE_RNG_STRIDE = 32
MOE_VMEM_LIMIT = 60 * 1024 * 1024


def _moe_kernel(h_ref, rt_ref, x1_ref, wg_ref, wu_ref, wd_ref, o_ref,
                xs_ref, ys_ref, pos_ref, rng_ref):
    g = pl.program_id(1)
    sub = pl.program_id(2)
    t_tile = h_ref.shape[0]
    rng0 = sub * MOE_RNG_STRIDE

    @pl.when(g == 0)
    def _sort_rows():
        rt = rt_ref[...]
        lane = lax.broadcasted_iota(jnp.int32, rt.shape, 1)
        cls = jnp.sum(jnp.where(lane == SORT_CLASS_LANE, rt, 0.0), axis=-1, keepdims=True)
        onehot = lane == cls.astype(jnp.int32)
        r = lax.broadcasted_iota(jnp.int32, (t_tile, t_tile), 0)
        c = lax.broadcasted_iota(jnp.int32, (t_tile, t_tile), 1)
        earlier = jnp.dot((r > c).astype(BF16), onehot.astype(BF16),
                          preferred_element_type=F32)
        cnt = jnp.sum(onehot.astype(F32), axis=0, keepdims=True)
        lane1 = lax.broadcasted_iota(jnp.int32, cnt.shape, 1)
        incl = cnt
        for d in (1, 2, 4, 8, 16):
            incl = incl + jnp.where(lane1 >= d, pltpu.roll(incl, d, axis=1), 0.0)
        off = incl - cnt
        pos = jnp.sum(jnp.where(onehot, earlier + off, 0.0), axis=-1, keepdims=True)
        pos_b = jnp.broadcast_to(pos, rt.shape)
        pos_ref[sub] = pos_b
        pos_row = pos_b.T[0:1, :]
        perm = (r.astype(F32) == pos_row).astype(BF16)
        rt_hi = rt.astype(BF16)
        rt_lo = (rt - rt_hi.astype(F32)).astype(BF16)
        cat = jnp.concatenate([h_ref[...], rt_hi, rt_lo], axis=1)
        xs_ref[sub, 0:t_tile, :] = jnp.dot(perm, cat, preferred_element_type=F32).astype(BF16)
        xs_ref[sub, t_tile:, :] = jnp.zeros((xs_ref.shape[1] - t_tile, xs_ref.shape[2]), BF16)
        ys_ref[sub] = jnp.zeros(ys_ref.shape[1:], F32)
        for k in range(N_SORT_CLASSES + 1):
            rng_ref[rng0 + k] = jnp.sum(jnp.where(lane1 == k, off, 0.0)).astype(jnp.int32)

    def window(e, r0, rows):
        r0 = pl.multiple_of(r0, MOE_ALIGN)
        xc = xs_ref[sub, pl.ds(r0, rows), 0:D_MODEL]
        rs = (xs_ref[sub, pl.ds(r0, rows), D_MODEL:D_MODEL + ROUTER_LANES].astype(F32)
              + xs_ref[sub, pl.ds(r0, rows), D_MODEL + ROUTER_LANES:].astype(F32))
        lane = lax.broadcasted_iota(jnp.int32, rs.shape, 1)
        a = jnp.dot(xc, wg_ref[e], preferred_element_type=F32)
        u = jnp.dot(xc, wu_ref[e], preferred_element_type=F32)
        ce = jnp.sum(jnp.where(lane == EXPERT_LANE0 + g * EXPERTS_PER_GROUP + e, rs, 0.0),
                     axis=-1, keepdims=True)
        hid = a * jax.nn.sigmoid(a) * u * ce
        ys_ref[sub, pl.ds(r0, rows), :] += jnp.dot(hid.astype(BF16), wd_ref[e],
                                                   preferred_element_type=F32)

    largest = MOE_WINDOW_ROWS[-1]
    for e, (pair_first, pair_last) in enumerate(EXPERT_PAIR_SPAN):
        start = rng_ref[rng0 + g * PAIRS_PER_GROUP + pair_first]
        stop = rng_ref[rng0 + g * PAIRS_PER_GROUP + pair_last]
        first = lax.shift_left(lax.shift_right_logical(start, MOE_ALIGN_SHIFT), MOE_ALIGN_SHIFT)
        span = jnp.where(stop > start, stop - first, 0)
        n_full = lax.shift_right_logical(span, MOE_LARGEST_SHIFT)

        def full_window(ci, carry, e=e, first=first):
            window(e, first + ci * largest, largest)
            return carry

        lax.fori_loop(0, n_full, full_window, 0)
        rem = span - lax.shift_left(n_full, MOE_LARGEST_SHIFT)
        rem_start = first + lax.shift_left(n_full, MOE_LARGEST_SHIFT)
        for lower, rows in zip((0,) + MOE_WINDOW_ROWS[:-1], MOE_WINDOW_ROWS):
            @pl.when((rem > lower) & (rem <= rows))
            def _remainder_window(e=e, rows=rows, rem_start=rem_start):
                window(e, rem_start, rows)

    @pl.when(g == N_GROUPS - 1)
    def _unsort_rows():
        pos_b = pos_ref[sub]
        c = lax.broadcasted_iota(jnp.int32, (t_tile, t_tile), 1).astype(F32)
        unperm = (jnp.concatenate([pos_b] * (t_tile // LANES), axis=1) == c).astype(BF16)
        o_ref[...] = x1_ref[...] + jnp.dot(unperm, ys_ref[sub, 0:t_tile, :].astype(BF16),
                                           preferred_element_type=F32)


def _moe(h2, route, x1, wg, wu, wd, tm=1024):
    n = h2.shape[0]
    last = N_GROUPS - 1
    group_w = lambda rows, cols: pl.BlockSpec((EXPERTS_PER_GROUP, rows, cols),
                                              lambda pr, g, sub: (g, 0, 0))
    sort_rows = lambda pr, g, sub: (MOE_TILES_PER_PASS * pr + jnp.where(g == 0, sub, MOE_TILES_PER_PASS - 1), 0)
    unsort_rows = lambda pr, g, sub: (MOE_TILES_PER_PASS * pr + jnp.where(g == last, sub, 0), 0)
    return pl.pallas_call(
        _moe_kernel,
        out_shape=jax.ShapeDtypeStruct((n, D_MODEL), F32),
        grid=(n // (MOE_TILES_PER_PASS * tm), N_GROUPS, MOE_TILES_PER_PASS),
        in_specs=[
            pl.BlockSpec((tm, D_MODEL), sort_rows),
            pl.BlockSpec((tm, ROUTER_LANES), sort_rows),
            pl.BlockSpec((tm, D_MODEL), unsort_rows, pipeline_mode=pl.Buffered(1)),
            group_w(D_MODEL, EXPERT_HIDDEN), group_w(D_MODEL, EXPERT_HIDDEN),
            group_w(EXPERT_HIDDEN, D_MODEL),
        ],
        out_specs=pl.BlockSpec((tm, D_MODEL), unsort_rows),
        scratch_shapes=[
            pltpu.VMEM((MOE_TILES_PER_PASS, tm + MOE_ROW_PAD, D_MODEL + 2 * ROUTER_LANES), BF16),
            pltpu.VMEM((MOE_TILES_PER_PASS, tm + MOE_ROW_PAD, D_MODEL), F32),
            pltpu.VMEM((MOE_TILES_PER_PASS, tm, ROUTER_LANES), F32),
            pltpu.SMEM((MOE_TILES_PER_PASS * MOE_RNG_STRIDE,), jnp.int32),
        ],
        compiler_params=pltpu.CompilerParams(
            dimension_semantics=("arbitrary", "arbitrary", "arbitrary"),
            vmem_limit_bytes=MOE_VMEM_LIMIT),
        name="moe",
    )(h2, route, x1, wg, wu, wd)


def _rope_tables(seq):
    pos = jnp.arange(seq, dtype=F32)
    inv_freq = ROPE_THETA ** (-jnp.arange(0, SUB_HEAD, 2, dtype=F32) / SUB_HEAD)
    ang = pos[:, None] * inv_freq[None, :]
    cos, sin = jnp.cos(ang), jnp.sin(ang)
    cos_full = jnp.concatenate([cos, cos, cos, cos], axis=-1)
    sin_signed = jnp.concatenate([-sin, sin, -sin, sin], axis=-1)
    return cos_full, sin_signed


def kernel(x, norm1_gain, w_in, b_gates, q_norm_gain, k_norm_gain, lambda_q1, lambda_k1, lambda_q2, lambda_k2, attn_subln_gain, w_attn_o, conv_w, conv_b, lru_wa, lru_ba, lru_wi, lru_bi, lru_lambda, w_lru_o, w_out, norm2_gain, w_group_router, b_group_router, w_expert_router, b_expert_router, w_expert_gate, w_expert_up, w_expert_down):
    b, s, d = x.shape
    n = b * s
    depth = w_in.shape[0]
    assert depth == 1 and d == D_MODEL
    cos, sin_signed = _rope_tables(s)
    x2 = x.reshape(n, d)
    l = 0

    sub_heads = ATTN_WIDTH // SUB_HEAD
    q_gain = jnp.tile(q_norm_gain[l], sub_heads)[None, :]
    k_gain = jnp.tile(k_norm_gain[l], sub_heads)[None, :]
    q_hm, kt_hm, v_hm, rest = _inproj(
        x, norm1_gain[l][None, :], w_in[l].astype(BF16), q_gain, k_gain, cos, sin_signed)
    proj3 = rest.reshape(b, s, REST_WIDTH)

    lam_params = jnp.stack([lambda_q1[l], lambda_k1[l], lambda_q2[l], lambda_k2[l]])
    attn_o = _attention(q_hm, kt_hm, v_hm, lam_params, attn_subln_gain[l][None, :],
                        q_gain, k_gain)

    w_cat = (0.5 * jnp.concatenate([lru_wa[l, 0], lru_wi[l, 0], lru_wa[l, 1], lru_wi[l, 1]],
                                   axis=-1)).astype(BF16)
    blk = lambda v: v.reshape(LRU_BLOCKS, 1, LRU_BLOCK_DIM)
    b_cat = 0.5 * jnp.concatenate([blk(lru_ba[l, 0]), blk(lru_bi[l, 0]),
                                   blk(lru_ba[l, 1]), blk(lru_bi[l, 1])], axis=-1)
    y_lru = _lru(proj3, conv_w[l], conv_b[l][None, :], w_cat, b_cat, lru_lambda[l])

    pad = ROUTER_LANES - N_GROUPS - N_EXPERTS
    w_router = jnp.concatenate([w_group_router[l], w_expert_router[l],
                                jnp.zeros((d, pad), F32)], axis=-1)
    b_router = jnp.concatenate([b_group_router[l], b_expert_router[l],
                                jnp.zeros((pad,), F32)])[None, :]
    w_router_hi = w_router.astype(BF16)
    w_router = jnp.concatenate(
        [w_router_hi, (w_router - w_router_hi.astype(F32)).astype(BF16)], axis=-1)
    x1, h2, cw = _merge(
        x2, attn_o, y_lru.reshape(n, LRU_WIDTH), rest,
        b_gates[l][None, :], w_attn_o[l].astype(BF16), w_lru_o[l].astype(BF16),
        w_out[l].astype(BF16), norm2_gain[l][None, :], w_router, b_router)

    out = _moe(h2, cw, x1, w_expert_gate[l].astype(BF16), w_expert_up[l].astype(BF16),
               w_expert_down[l].astype(BF16))
    return out.reshape(b, s, d)
```

```python
import functools
import math

import jax
import jax.numpy as jnp
from jax import lax
from jax.experimental import pallas as pl
from jax.experimental.pallas import tpu as pltpu

F32 = jnp.float32
BF16 = jnp.bfloat16

D_MODEL = 1024
N_HEADS = 8
SUB_HEAD = 64
V_HEAD = 128
ATTN_WIDTH = 1024
LRU_WIDTH = 1024
LRU_BLOCKS = 8
LRU_BLOCK_DIM = 128
LRU_C = 8.0
PROJ_WIDTH = 7168
N_GROUPS = 4
EXPERTS_PER_GROUP = 4
N_EXPERTS = 16
EXPERT_HIDDEN = 512
ROPE_THETA = 10000.0
RMS_EPS = 1e-6
LAMBDA_INIT = 0.8 - 0.6 * math.exp(-0.3 * 0)

LOG2_E = math.log2(math.e)
LANES = 128
SUBLANES = 8
VMEM_LIMIT = 56 * 1024 * 1024

REST_WIDTH = PROJ_WIDTH - 3 * ATTN_WIDTH
LRUX_COL, LRUG_COL = 0, 8
GATE_COL_1024 = 2

ROUTER_LANES = 128
EXPERT_LANE0 = N_GROUPS
SORT_CLASS_LANE = N_GROUPS + N_EXPERTS
PAIR_ORDER = ((0, 3), (0, 2), (0, 1), (1, 2), (1, 3), (2, 3))
PAIRS_PER_GROUP = len(PAIR_ORDER)
EXPERT_PAIR_SPAN = ((0, 3), (2, 5), (1, 6), (0, 6))
N_SORT_CLASSES = N_GROUPS * PAIRS_PER_GROUP


def _rms(x, gain):
    ms = jnp.mean(x * x, axis=-1, keepdims=True)
    return x * lax.rsqrt(ms + RMS_EPS) * gain


MXU_TILE = 256


def _subhead_norm_rope(x, gain, cos, sin_signed):
    width = x.shape[1]
    r = lax.broadcasted_iota(jnp.int32, (MXU_TILE, MXU_TILE), 0) // SUB_HEAD
    c = lax.broadcasted_iota(jnp.int32, (MXU_TILE, MXU_TILE), 1) // SUB_HEAD
    group_mean = jnp.where(r == c, 1.0 / SUB_HEAD, 0.0).astype(BF16)
    xx = (x * x).astype(BF16)
    ms = jnp.concatenate(
        [jnp.dot(xx[:, t * MXU_TILE:(t + 1) * MXU_TILE], group_mean, preferred_element_type=F32)
         for t in range(width // MXU_TILE)], axis=1)
    xn = x * lax.rsqrt(ms + RMS_EPS) * gain
    lane = lax.broadcasted_iota(jnp.int32, xn.shape, 1)
    first_half = (lane % SUB_HEAD) < (SUB_HEAD // 2)
    partner = jnp.where(first_half,
                        pltpu.roll(xn, width - SUB_HEAD // 2, axis=1),
                        pltpu.roll(xn, SUB_HEAD // 2, axis=1))
    reps = width // LANES
    return (xn * jnp.concatenate([cos] * reps, axis=1)
            + partner * jnp.concatenate([sin_signed] * reps, axis=1))


def _inproj_kernel(x_ref, g_ref, w_ref, qg_ref, kg_ref, cos_ref, sin_ref,
                   q_ref, kt_ref, v_ref, rest_ref):
    h = _rms(x_ref[...], g_ref[...]).astype(BF16)
    cos, sin_signed = cos_ref[...], sin_ref[...]

    def proj(j):
        return jnp.dot(h, w_ref[:, j * 1024:(j + 1) * 1024], preferred_element_type=F32)

    q = _subhead_norm_rope(proj(0), qg_ref[...], cos, sin_signed) * (SUB_HEAD ** -0.5 * LOG2_E)
    k = _subhead_norm_rope(proj(1), kg_ref[...], cos, sin_signed)
    v = proj(2).astype(BF16)
    for hd in range(N_HEADS):
        cols = slice(hd * LANES, (hd + 1) * LANES)
        q_ref[0, hd] = q[:, cols].astype(BF16)
        kt_ref[0, hd] = k[:, cols].T.astype(BF16)
        v_ref[0, hd] = v[:, cols]
    for j in range(3, PROJ_WIDTH // 1024):
        rest_ref[:, (j - 3) * 1024:(j - 2) * 1024] = proj(j).astype(BF16)


def _inproj(x3, gain, w_bf16, q_gain, k_gain, cos, sin_signed, tm=512):
    b, s, d = x3.shape
    n = b * s
    bps = s // tm
    const = lambda shape: pl.BlockSpec(shape, lambda i: (0, 0))
    table = pl.BlockSpec((tm, LANES), lambda i: (i % bps, 0))
    heads = pl.BlockSpec((1, N_HEADS, tm, LANES), lambda i: (i // bps, 0, i % bps, 0))
    return pl.pallas_call(
        _inproj_kernel,
        out_shape=(jax.ShapeDtypeStruct((b, N_HEADS, s, LANES), BF16),
                   jax.ShapeDtypeStruct((b, N_HEADS, LANES, s), BF16),
                   jax.ShapeDtypeStruct((b, N_HEADS, s, V_HEAD), BF16),
                   jax.ShapeDtypeStruct((n, REST_WIDTH), BF16)),
        grid=(n // tm,),
        in_specs=[
            pl.BlockSpec((tm, D_MODEL), lambda i: (i, 0)),
            const((1, D_MODEL)),
            const((D_MODEL, PROJ_WIDTH)),
            const((1, ATTN_WIDTH)), const((1, ATTN_WIDTH)), table, table,
        ],
        out_specs=(heads,
                   pl.BlockSpec((1, N_HEADS, LANES, tm), lambda i: (i // bps, 0, 0, i % bps)),
                   heads,
                   pl.BlockSpec((tm, REST_WIDTH), lambda i: (i, 0))),
        compiler_params=pltpu.CompilerParams(
            dimension_semantics=("arbitrary",), vmem_limit_bytes=VMEM_LIMIT),
        name="inproj",
    )(x3.reshape(n, d), gain, w_bf16, q_gain, k_gain, cos, sin_signed)


ATTN_SUB_ROWS = 128


SCORE_BOUND_LOG2 = 100.0


def _attn_kernel(lam_ref, sg_ref, qg_ref, kg_ref, q_ref, kt_ref, v_ref, o_ref, *, tq):
    lp = lam_ref[...]
    lam = (jnp.exp(jnp.sum(lp[0:1] * lp[1:2], axis=-1, keepdims=True))
           - jnp.exp(jnp.sum(lp[2:3] * lp[3:4], axis=-1, keepdims=True)) + LAMBDA_INIT)

    score_bound = (SUB_HEAD * (SUB_HEAD ** -0.5 * LOG2_E) * 1.01
                   * jnp.max(jnp.abs(qg_ref[...])) * jnp.max(jnp.abs(kg_ref[...])))
    ts = ATTN_SUB_ROWS

    def chains(subtract_max):
        for t in range(tq // ts):
            q = q_ref[0, 0, t * ts:(t + 1) * ts, :]
            lane = lax.broadcasted_iota(jnp.int32, q.shape, 1)
            zero = jnp.zeros_like(q)
            qq = jnp.concatenate([jnp.where(lane < SUB_HEAD, q, zero),
                                  jnp.where(lane >= SUB_HEAD, q, zero)], axis=0)
            s = jnp.dot(qq, kt_ref[0, 0], preferred_element_type=F32)
            if subtract_max:
                s = s - jnp.max(s, axis=-1, keepdims=True)
            p = jnp.exp2(s)
            l = jnp.sum(p, axis=-1, keepdims=True)
            acc = jnp.dot(p.astype(BF16), v_ref[0, 0], preferred_element_type=F32)
            o = acc / l
            o = o[0:ts] - lam * o[ts:2 * ts]
            o = _rms(o, sg_ref[...]) * (1.0 - LAMBDA_INIT)
            o_ref[0, 0, t * ts:(t + 1) * ts, :] = o.astype(BF16)

    @pl.when(score_bound <= SCORE_BOUND_LOG2)
    def _bounded_scores():
        chains(subtract_max=False)

    @pl.when(jnp.logical_not(score_bound <= SCORE_BOUND_LOG2))
    def _any_scores():
        chains(subtract_max=True)


def _attention(q_hm, kt_hm, v_hm, lam_params, subln_gain, q_gain, k_gain, tq=1024):
    b, h, s, _ = q_hm.shape
    const = lambda shape: pl.BlockSpec(shape, lambda bi, hi, qi: (0, 0))
    return pl.pallas_call(
        functools.partial(_attn_kernel, tq=tq),
        out_shape=jax.ShapeDtypeStruct((b, h, s, V_HEAD), BF16),
        grid=(b, h, s // tq),
        in_specs=[
            const((4, SUB_HEAD)), const((1, V_HEAD)), const((1, ATTN_WIDTH)), const((1, ATTN_WIDTH)),
            pl.BlockSpec((1, 1, tq, LANES), lambda bi, hi, qi: (bi, hi, qi, 0)),
            pl.BlockSpec((1, 1, LANES, s), lambda bi, hi, qi: (bi, hi, 0, 0)),
            pl.BlockSpec((1, 1, s, V_HEAD), lambda bi, hi, qi: (bi, hi, 0, 0)),
        ],
        out_specs=pl.BlockSpec((1, 1, tq, V_HEAD), lambda bi, hi, qi: (bi, hi, qi, 0)),
        compiler_params=pltpu.CompilerParams(
            dimension_semantics=("arbitrary", "arbitrary", "arbitrary"),
            vmem_limit_bytes=VMEM_LIMIT),
        name="diff_attn",
    )(lam_params, subln_gain, q_gain, k_gain, q_hm, kt_hm, v_hm)


def _softplus(x):
    return jnp.maximum(x, 0.0) + jnp.log1p(jnp.exp(-jnp.abs(x)))


def _gelu_tanh(x):
    return 0.5 * x * (1.0 + jnp.tanh(math.sqrt(2.0 / math.pi) * (x + 0.044715 * (x * x * x))))


LRU_BLOCKS_PER_STEP = 2


def _lru_kernel(x_ref, g_ref, cw_ref, cb_ref, w_ref, b_ref, lam_ref, o_ref,
                xs_ref, a0_ref, u0_ref, a1_ref, u1_ref, *, seq, tc):
    pad = SUBLANES
    width = LRU_BLOCKS_PER_STEP * LANES
    zeros_pad = jnp.zeros((pad, width), F32)
    xs_ref[0:pad, :] = zeros_pad
    xs_ref[pad + seq:pad + seq + pad, :] = zeros_pad
    xs_ref[pad:pad + seq, :] = x_ref[0].astype(F32)

    k_all = (-LRU_C * 0.5 * LOG2_E) * _softplus(-lam_ref[...])
    cw_all = cw_ref[...]
    cb_all = cb_ref[...]

    n_seg = SUBLANES
    seg = seq // n_seg
    assert tc == seg
    for c in range(n_seg):
        base = pad + c * tc
        for j in range(LRU_BLOCKS_PER_STEP):
            cols = slice(j * LANES, (j + 1) * LANES)
            cw, cb = cw_all[:, cols], cb_all[:, cols]
            win = xs_ref[base - pad:base + tc + pad, cols]
            n_win = tc + 2 * pad
            taps = (pltpu.roll(win, 1, axis=0), win,
                    pltpu.roll(win, n_win - 1, axis=0), pltpu.roll(win, n_win - 2, axis=0))
            xr = cb + sum(cw[t:t + 1] * taps[t][pad:pad + tc] for t in range(4))
            th = jnp.tanh(jnp.dot(xr.astype(BF16), w_ref[j], preferred_element_type=F32)
                          + b_ref[j])
            for d, (a_ref, u_ref) in enumerate(((a0_ref, u0_ref), (a1_ref, u1_ref))):
                k = k_all[d:d + 1, cols]
                a = jnp.exp2(k * th[:, (2 * d) * LANES:(2 * d + 1) * LANES] + k)
                gate_i = 0.5 * th[:, (2 * d + 1) * LANES:(2 * d + 2) * LANES] + 0.5
                v = 1.0 - a * a
                mult = jnp.where(v > 0.0, v * lax.rsqrt(v), 0.0)
                a_ref[j, pl.ds(c, seg, stride=SUBLANES), :] = a
                u_ref[j, pl.ds(c, seg, stride=SUBLANES), :] = mult * (gate_i * xr)

    chains = [(a0_ref, u0_ref, j, False) for j in range(LRU_BLOCKS_PER_STEP)] \
        + [(a1_ref, u1_ref, j, True) for j in range(LRU_BLOCKS_PER_STEP)]

    def step(t, carry):
        out = []
        for (a_ref, u_ref, j, reverse), (h, p) in zip(chains, carry):
            r0 = pl.multiple_of((seg - 1 - t if reverse else t) * SUBLANES, SUBLANES)
            a = a_ref[j, pl.ds(r0, SUBLANES), :]
            h = a * h + u_ref[j, pl.ds(r0, SUBLANES), :]
            p = a * p
            u_ref[j, pl.ds(r0, SUBLANES), :] = h
            a_ref[j, pl.ds(r0, SUBLANES), :] = p
            out.append((h, p))
        return tuple(out)

    zero = jnp.zeros((SUBLANES, LANES), F32)
    one = jnp.ones((SUBLANES, LANES), F32)
    ends = lax.fori_loop(0, seg, step, ((zero, one),) * len(chains), unroll=8)

    row = lax.broadcasted_iota(jnp.int32, (SUBLANES, LANES), 0)
    entering = []
    for (_, _, _, reverse), (h_end, p_end) in zip(chains, ends):
        state = zero
        order = range(SUBLANES - 2, -1, -1) if reverse else range(1, SUBLANES)
        for s in order:
            nxt = pltpu.roll(h_end + p_end * state, SUBLANES - 1 if reverse else 1, axis=0)
            state = jnp.where(row == s, nxt, state)
        entering.append(state)

    for c in range(n_seg):
        rows = slice(c * tc, (c + 1) * tc)
        for j in range(LRU_BLOCKS_PER_STEP):
            cols = slice(j * LANES, (j + 1) * LANES)
            hs = 0.0
            for (a_ref, u_ref, jj, _), state in zip(chains, entering):
                if jj == j:
                    hs = hs + (u_ref[j, pl.ds(c, seg, stride=SUBLANES), :]
                               + a_ref[j, pl.ds(c, seg, stride=SUBLANES), :] * state[c:c + 1, :])
            y = hs * _gelu_tanh(g_ref[0, rows, cols].astype(F32))
            o_ref[0, rows, cols] = y.astype(BF16)


def _lru(proj3, conv_w, conv_b, w_cat, b_cat, lam):
    b, s, _ = proj3.shape
    tc = s // SUBLANES
    kernel = functools.partial(_lru_kernel, seq=s, tc=tc)
    nb = LRU_BLOCKS_PER_STEP
    width = nb * LANES
    return pl.pallas_call(
        kernel,
        out_shape=jax.ShapeDtypeStruct((b, s, LRU_WIDTH), BF16),
        grid=(b, LRU_BLOCKS // nb),
        in_specs=[
            pl.BlockSpec((1, s, width), lambda bi, ni: (bi, 0, LRUX_COL // nb + ni)),
            pl.BlockSpec((1, s, width), lambda bi, ni: (bi, 0, LRUG_COL // nb + ni)),
            pl.BlockSpec((4, width), lambda bi, ni: (0, ni)),
            pl.BlockSpec((1, width), lambda bi, ni: (0, ni)),
            pl.BlockSpec((nb, LRU_BLOCK_DIM, 4 * LRU_BLOCK_DIM), lambda bi, ni: (ni, 0, 0)),
            pl.BlockSpec((nb, 1, 4 * LRU_BLOCK_DIM), lambda bi, ni: (ni, 0, 0)),
            pl.BlockSpec((2, width), lambda bi, ni: (0, ni)),
        ],
        out_specs=pl.BlockSpec((1, s, width), lambda bi, ni: (bi, 0, ni)),
        scratch_shapes=[pltpu.VMEM((s + 2 * SUBLANES, width), F32)]
        + [pltpu.VMEM((nb, s, LANES), F32)] * 4,
        compiler_params=pltpu.CompilerParams(
            dimension_semantics=("arbitrary", "arbitrary"), vmem_limit_bytes=VMEM_LIMIT),
        name="rglru",
    )(proj3, proj3, conv_w, conv_b, w_cat, b_cat, lam)


def _merge_kernel(x_ref, ao_ref, yl_ref, ga_ref, gl_ref, bg_ref, wa_ref, wl_ref, wo_ref,
                  n2_ref, wr_ref, br_ref, x1_ref, h2_ref, cw_ref):
    attn_o = jnp.concatenate([ao_ref[0, h] for h in range(N_HEADS)], axis=1)
    attn_d = jnp.dot(attn_o, wa_ref[...], preferred_element_type=F32)
    lru_d = jnp.dot(yl_ref[...], wl_ref[...], preferred_element_type=F32)
    bg = bg_ref[...]
    g_attn = jax.nn.sigmoid(ga_ref[...].astype(F32) + bg[:, 0:D_MODEL])
    g_lru = jax.nn.sigmoid(gl_ref[...].astype(F32) + bg[:, D_MODEL:2 * D_MODEL])
    merged = g_attn * attn_d + g_lru * lru_d
    x1 = x_ref[...] + jnp.dot(merged.astype(BF16), wo_ref[...], preferred_element_type=F32)
    x1_ref[...] = x1
    h2 = _rms(x1, n2_ref[...])
    h2_ref[...] = h2.astype(BF16)

    h2_hi = h2.astype(BF16)
    h2_lo = (h2 - h2_hi.astype(F32)).astype(BF16)
    wr = wr_ref[...]
    part = jnp.dot(h2_hi, wr, preferred_element_type=F32)
    logits = (part[:, 0:ROUTER_LANES] + part[:, ROUTER_LANES:]
              + jnp.dot(h2_lo, wr[:, 0:ROUTER_LANES], preferred_element_type=F32) + br_ref[...])
    lane = lax.broadcasted_iota(jnp.int32, logits.shape, 1)
    neg = jnp.full_like(logits, -jnp.inf)
    big = jnp.full_like(lane, ROUTER_LANES)

    def masked_max(mask):
        return jnp.max(jnp.where(mask, logits, neg), axis=-1, keepdims=True)

    def first_lane(mask, value):
        return jnp.min(jnp.where(mask & (logits == value), lane, big), axis=-1, keepdims=True)

    g_mask = lane < N_GROUPS
    g_max = masked_max(g_mask)
    g_sel = first_lane(g_mask, g_max)
    g_w = 1.0 / jnp.sum(jnp.where(g_mask, jnp.exp(logits - g_max), 0.0), axis=-1, keepdims=True)
    e_lo = EXPERT_LANE0 + g_sel * EXPERTS_PER_GROUP
    e_mask = (lane >= e_lo) & (lane < e_lo + EXPERTS_PER_GROUP)
    v1 = masked_max(e_mask)
    i1 = first_lane(e_mask, v1)
    e_mask2 = e_mask & (lane != i1)
    v2 = masked_max(e_mask2)
    i2 = first_lane(e_mask2, v2)
    t = jnp.exp(v2 - v1)
    w1 = g_w / (1.0 + t)
    w2 = g_w * t / (1.0 + t)
    ea, eb = i1 - e_lo, i2 - e_lo
    p_lo, p_hi = jnp.minimum(ea, eb), jnp.maximum(ea, eb)
    pair_rank = jnp.where(p_lo == 0, 3 - p_hi, jnp.where(p_lo == 1, p_hi + 1, 5))
    sort_class = g_sel * PAIRS_PER_GROUP + pair_rank
    cw_ref[...] = (jnp.where(lane == i1, w1, 0.0) + jnp.where(lane == i2, w2, 0.0)
                   + jnp.where(lane == SORT_CLASS_LANE, sort_class.astype(F32), 0.0))


def _merge(x2, attn_o, y_lru, proj, b_gates, wa, wl, wo, n2_gain, w_router, b_router, tm=512):
    n = x2.shape[0]
    blocks_per_seq = attn_o.shape[2] // tm
    row = lambda cols, col_blk=0: pl.BlockSpec((tm, cols), lambda i: (i, col_blk))
    const = lambda shape: pl.BlockSpec(shape, lambda i: (0, 0))
    heads = pl.BlockSpec((1, N_HEADS, tm, V_HEAD),
                         lambda i: (i // blocks_per_seq, 0, i % blocks_per_seq, 0))
    return pl.pallas_call(
        _merge_kernel,
        out_shape=(jax.ShapeDtypeStruct((n, D_MODEL), F32),
                   jax.ShapeDtypeStruct((n, D_MODEL), BF16),
                   jax.ShapeDtypeStruct((n, ROUTER_LANES), F32)),
        grid=(n // tm,),
        in_specs=[
            row(D_MODEL), heads, row(LRU_WIDTH),
            row(D_MODEL, GATE_COL_1024), row(D_MODEL, GATE_COL_1024 + 1),
            const((1, 2 * D_MODEL)),
            const((ATTN_WIDTH, D_MODEL)), const((LRU_WIDTH, D_MODEL)), const((D_MODEL, D_MODEL)),
            const((1, D_MODEL)), const((D_MODEL, 2 * ROUTER_LANES)), const((1, ROUTER_LANES)),
        ],
        out_specs=(row(D_MODEL), row(D_MODEL), row(ROUTER_LANES)),
        compiler_params=pltpu.CompilerParams(
            dimension_semantics=("arbitrary",), vmem_limit_bytes=VMEM_LIMIT),
        name="merge_router",
    )(x2, attn_o, y_lru, proj, proj, b_gates, wa, wl, wo, n2_gain, w_router, b_router)


MOE_ALIGN = 16
MOE_ALIGN_SHIFT = 4
MOE_WINDOW_ROWS = (256, 384, 512)
MOE_LARGEST_SHIFT = 9
MOE_ROW_PAD = MOE_WINDOW_ROWS[0]
MOE_TILES_PER_PASS = 1
MOE_RNG_STRIDE = 32
MOE_VMEM_LIMIT = VMEM_LIMIT


def _moe_kernel(h_ref, rt_ref, x1_ref, wg_ref, wu_ref, wd_ref, o_ref,
                xs_ref, ys_ref, pos_ref, rng_ref):
    g = pl.program_id(1)
    sub = pl.program_id(2)
    t_tile = h_ref.shape[0]
    rng0 = sub * MOE_RNG_STRIDE

    @pl.when(g == 0)
    def _sort_rows():
        rt = rt_ref[...]
        lane = lax.broadcasted_iota(jnp.int32, rt.shape, 1)
        cls = jnp.sum(jnp.where(lane == SORT_CLASS_LANE, rt, 0.0), axis=-1, keepdims=True)
        onehot = lane == cls.astype(jnp.int32)
        r = lax.broadcasted_iota(jnp.int32, (t_tile, t_tile), 0)
        c = lax.broadcasted_iota(jnp.int32, (t_tile, t_tile), 1)
        earlier = jnp.dot((r > c).astype(BF16), onehot.astype(BF16),
                          preferred_element_type=F32)
        cnt = jnp.sum(onehot.astype(F32), axis=0, keepdims=True)
        lane1 = lax.broadcasted_iota(jnp.int32, cnt.shape, 1)
        incl = cnt
        for d in (1, 2, 4, 8, 16):
            incl = incl + jnp.where(lane1 >= d, pltpu.roll(incl, d, axis=1), 0.0)
        off = incl - cnt
        pos = jnp.sum(jnp.where(onehot, earlier + off, 0.0), axis=-1, keepdims=True)
        pos_b = jnp.broadcast_to(pos, rt.shape)
        pos_ref[sub] = pos_b
        pos_row = pos_b.T[0:1, :]
        perm = (r.astype(F32) == pos_row).astype(BF16)
        rt_hi = rt.astype(BF16)
        rt_lo = (rt - rt_hi.astype(F32)).astype(BF16)
        cat = jnp.concatenate([h_ref[...], rt_hi, rt_lo], axis=1)
        xs_ref[sub, 0:t_tile, :] = jnp.dot(perm, cat, preferred_element_type=F32).astype(BF16)
        xs_ref[sub, t_tile:, :] = jnp.zeros((xs_ref.shape[1] - t_tile, xs_ref.shape[2]), BF16)
        ys_ref[sub] = jnp.zeros(ys_ref.shape[1:], F32)
        for k in range(N_SORT_CLASSES + 1):
            rng_ref[rng0 + k] = jnp.sum(jnp.where(lane1 == k, off, 0.0)).astype(jnp.int32)

    def window(e, r0, rows):
        r0 = pl.multiple_of(r0, MOE_ALIGN)
        xc = xs_ref[sub, pl.ds(r0, rows), 0:D_MODEL]
        rs = (xs_ref[sub, pl.ds(r0, rows), D_MODEL:D_MODEL + ROUTER_LANES].astype(F32)
              + xs_ref[sub, pl.ds(r0, rows), D_MODEL + ROUTER_LANES:].astype(F32))
        lane = lax.broadcasted_iota(jnp.int32, rs.shape, 1)
        a = jnp.dot(xc, wg_ref[e], preferred_element_type=F32)
        u = jnp.dot(xc, wu_ref[e], preferred_element_type=F32)
        ce = jnp.sum(jnp.where(lane == EXPERT_LANE0 + g * EXPERTS_PER_GROUP + e, rs, 0.0),
                     axis=-1, keepdims=True)
        hid = a * jax.nn.sigmoid(a) * u * ce
        ys_ref[sub, pl.ds(r0, rows), :] += jnp.dot(hid.astype(BF16), wd_ref[e],
                                                   preferred_element_type=F32)

    largest = MOE_WINDOW_ROWS[-1]
    for e, (pair_first, pair_last) in enumerate(EXPERT_PAIR_SPAN):
        start = rng_ref[rng0 + g * PAIRS_PER_GROUP + pair_first]
        stop = rng_ref[rng0 + g * PAIRS_PER_GROUP + pair_last]
        first = lax.shift_left(lax.shift_right_logical(start, MOE_ALIGN_SHIFT), MOE_ALIGN_SHIFT)
        span = jnp.where(stop > start, stop - first, 0)
        n_full = lax.shift_right_logical(span, MOE_LARGEST_SHIFT)

        def full_window(ci, carry, e=e, first=first):
            window(e, first + ci * largest, largest)
            return carry

        lax.fori_loop(0, n_full, full_window, 0)
        rem = span - lax.shift_left(n_full, MOE_LARGEST_SHIFT)
        rem_start = first + lax.shift_left(n_full, MOE_LARGEST_SHIFT)
        for lower, rows in zip((0,) + MOE_WINDOW_ROWS[:-1], MOE_WINDOW_ROWS):
            @pl.when((rem > lower) & (rem <= rows))
            def _remainder_window(e=e, rows=rows, rem_start=rem_start):
                window(e, rem_start, rows)

    @pl.when(g == N_GROUPS - 1)
    def _unsort_rows():
        pos_b = pos_ref[sub]
        c = lax.broadcasted_iota(jnp.int32, (t_tile, t_tile), 1).astype(F32)
        unperm = (jnp.concatenate([pos_b] * (t_tile // LANES), axis=1) == c).astype(BF16)
        o_ref[...] = x1_ref[...] + jnp.dot(unperm, ys_ref[sub, 0:t_tile, :].astype(BF16),
                                           preferred_element_type=F32)


def _moe(h2, route, x1, wg, wu, wd, tm=1024):
    n = h2.shape[0]
    last = N_GROUPS - 1
    group_w = lambda rows, cols: pl.BlockSpec((EXPERTS_PER_GROUP, rows, cols),
                                              lambda pr, g, sub: (g, 0, 0))
    sort_rows = lambda pr, g, sub: (MOE_TILES_PER_PASS * pr + jnp.where(g == 0, sub, MOE_TILES_PER_PASS - 1), 0)
    unsort_rows = lambda pr, g, sub: (MOE_TILES_PER_PASS * pr + jnp.where(g == last, sub, 0), 0)
    return pl.pallas_call(
        _moe_kernel,
        out_shape=jax.ShapeDtypeStruct((n, D_MODEL), F32),
        grid=(n // (MOE_TILES_PER_PASS * tm), N_GROUPS, MOE_TILES_PER_PASS),
        in_specs=[
            pl.BlockSpec((tm, D_MODEL), sort_rows),
            pl.BlockSpec((tm, ROUTER_LANES), sort_rows),
            pl.BlockSpec((tm, D_MODEL), unsort_rows, pipeline_mode=pl.Buffered(1)),
            group_w(D_MODEL, EXPERT_HIDDEN), group_w(D_MODEL, EXPERT_HIDDEN),
            group_w(EXPERT_HIDDEN, D_MODEL),
        ],
        out_specs=pl.BlockSpec((tm, D_MODEL), unsort_rows),
        scratch_shapes=[
            pltpu.VMEM((MOE_TILES_PER_PASS, tm + MOE_ROW_PAD, D_MODEL + 2 * ROUTER_LANES), BF16),
            pltpu.VMEM((MOE_TILES_PER_PASS, tm + MOE_ROW_PAD, D_MODEL), F32),
            pltpu.VMEM((MOE_TILES_PER_PASS, tm, ROUTER_LANES), F32),
            pltpu.SMEM((MOE_TILES_PER_PASS * MOE_RNG_STRIDE,), jnp.int32),
        ],
        compiler_params=pltpu.CompilerParams(
            dimension_semantics=("arbitrary", "arbitrary", "arbitrary"),
            vmem_limit_bytes=MOE_VMEM_LIMIT),
        name="moe",
    )(h2, route, x1, wg, wu, wd)


def _rope_tables(seq):
    pos = jnp.arange(seq, dtype=F32)
    inv_freq = ROPE_THETA ** (-jnp.arange(0, SUB_HEAD, 2, dtype=F32) / SUB_HEAD)
    ang = pos[:, None] * inv_freq[None, :]
    cos, sin = jnp.cos(ang), jnp.sin(ang)
    cos_full = jnp.concatenate([cos, cos, cos, cos], axis=-1)
    sin_signed = jnp.concatenate([-sin, sin, -sin, sin], axis=-1)
    return cos_full, sin_signed


def kernel(x, norm1_gain, w_in, b_gates, q_norm_gain, k_norm_gain, lambda_q1, lambda_k1, lambda_q2, lambda_k2, attn_subln_gain, w_attn_o, conv_w, conv_b, lru_wa, lru_ba, lru_wi, lru_bi, lru_lambda, w_lru_o, w_out, norm2_gain, w_group_router, b_group_router, w_expert_router, b_expert_router, w_expert_gate, w_expert_up, w_expert_down):
    b, s, d = x.shape
    n = b * s
    depth = w_in.shape[0]
    assert depth == 1 and d == D_MODEL
    cos, sin_signed = _rope_tables(s)
    x2 = x.reshape(n, d)
    l = 0

    sub_heads = ATTN_WIDTH // SUB_HEAD
    q_gain = jnp.tile(q_norm_gain[l], sub_heads)[None, :]
    k_gain = jnp.tile(k_norm_gain[l], sub_heads)[None, :]
    q_hm, kt_hm, v_hm, rest = _inproj(
        x, norm1_gain[l][None, :], w_in[l].astype(BF16), q_gain, k_gain, cos, sin_signed)
    proj3 = rest.reshape(b, s, REST_WIDTH)

    lam_params = jnp.stack([lambda_q1[l], lambda_k1[l], lambda_q2[l], lambda_k2[l]])
    attn_o = _attention(q_hm, kt_hm, v_hm, lam_params, attn_subln_gain[l][None, :],
                        q_gain, k_gain)

    w_cat = (0.5 * jnp.concatenate([lru_wa[l, 0], lru_wi[l, 0], lru_wa[l, 1], lru_wi[l, 1]],
                                   axis=-1)).astype(BF16)
    blk = lambda v: v.reshape(LRU_BLOCKS, 1, LRU_BLOCK_DIM)
    b_cat = 0.5 * jnp.concatenate([blk(lru_ba[l, 0]), blk(lru_bi[l, 0]),
                                   blk(lru_ba[l, 1]), blk(lru_bi[l, 1])], axis=-1)
    y_lru = _lru(proj3, conv_w[l], conv_b[l][None, :], w_cat, b_cat, lru_lambda[l])

    pad = ROUTER_LANES - N_GROUPS - N_EXPERTS
    w_router = jnp.concatenate([w_group_router[l], w_expert_router[l],
                                jnp.zeros((d, pad), F32)], axis=-1)
    b_router = jnp.concatenate([b_group_router[l], b_expert_router[l],
                                jnp.zeros((pad,), F32)])[None, :]
    w_router_hi = w_router.astype(BF16)
    w_router = jnp.concatenate(
        [w_router_hi, (w_router - w_router_hi.astype(F32)).astype(BF16)], axis=-1)
    x1, h2, cw = _merge(
        x2, attn_o, y_lru.reshape(n, LRU_WIDTH), rest,
        b_gates[l][None, :], w_attn_o[l].astype(BF16), w_lru_o[l].astype(BF16),
        w_out[l].astype(BF16), norm2_gain[l][None, :], w_router, b_router)

    out = _moe(h2, cw, x1, w_expert_gate[l].astype(BF16), w_expert_up[l].astype(BF16),
               w_expert_down[l].astype(BF16))
    return out.reshape(b, s, d)
```

```python
import functools
import math

import jax
import jax.numpy as jnp
from jax import lax
from jax.experimental import pallas as pl
from jax.experimental.pallas import tpu as pltpu

F32 = jnp.float32
BF16 = jnp.bfloat16

D_MODEL = 1024
N_HEADS = 8
SUB_HEAD = 64
V_HEAD = 128
ATTN_WIDTH = 1024
LRU_WIDTH = 1024
LRU_BLOCKS = 8
LRU_BLOCK_DIM = 128
LRU_C = 8.0
PROJ_WIDTH = 7168
N_GROUPS = 4
EXPERTS_PER_GROUP = 4
N_EXPERTS = 16
EXPERT_HIDDEN = 512
ROPE_THETA = 10000.0
RMS_EPS = 1e-6
LAMBDA_INIT = 0.8 - 0.6 * math.exp(-0.3 * 0)

LOG2_E = math.log2(math.e)
LANES = 128
SUBLANES = 8
VMEM_LIMIT = 56 * 1024 * 1024

REST_WIDTH = PROJ_WIDTH - 3 * ATTN_WIDTH
LRUX_COL, LRUG_COL = 0, 8
GATE_COL_1024 = 2

ROUTER_LANES = 128
EXPERT_LANE0 = N_GROUPS
SORT_CLASS_LANE = N_GROUPS + N_EXPERTS
CLASSES_PER_GROUP = 1
N_SORT_CLASSES = N_GROUPS * CLASSES_PER_GROUP


def _rms(x, gain):
    ms = jnp.mean(x * x, axis=-1, keepdims=True)
    return x * lax.rsqrt(ms + RMS_EPS) * gain


MXU_TILE = 256


def _subhead_norm_rope(x, gain, cos, sin_signed):
    width = x.shape[1]
    r = lax.broadcasted_iota(jnp.int32, (MXU_TILE, MXU_TILE), 0) // SUB_HEAD
    c = lax.broadcasted_iota(jnp.int32, (MXU_TILE, MXU_TILE), 1) // SUB_HEAD
    group_mean = jnp.where(r == c, 1.0 / SUB_HEAD, 0.0).astype(BF16)
    xx = (x * x).astype(BF16)
    ms = jnp.concatenate(
        [jnp.dot(xx[:, t * MXU_TILE:(t + 1) * MXU_TILE], group_mean, preferred_element_type=F32)
         for t in range(width // MXU_TILE)], axis=1)
    xn = x * lax.rsqrt(ms + RMS_EPS) * gain
    lane = lax.broadcasted_iota(jnp.int32, xn.shape, 1)
    first_half = (lane % SUB_HEAD) < (SUB_HEAD // 2)
    partner = jnp.where(first_half,
                        pltpu.roll(xn, width - SUB_HEAD // 2, axis=1),
                        pltpu.roll(xn, SUB_HEAD // 2, axis=1))
    reps = width // LANES
    return (xn * jnp.concatenate([cos] * reps, axis=1)
            + partner * jnp.concatenate([sin_signed] * reps, axis=1))


def _inproj_kernel(x_ref, g_ref, w_ref, qg_ref, kg_ref, cos_ref, sin_ref,
                   q_ref, kt_ref, v_ref, rest_ref):
    h = _rms(x_ref[...], g_ref[...]).astype(BF16)
    cos, sin_signed = cos_ref[...], sin_ref[...]

    def proj(j):
        return jnp.dot(h, w_ref[:, j * 1024:(j + 1) * 1024], preferred_element_type=F32)

    q = _subhead_norm_rope(proj(0), qg_ref[...], cos, sin_signed) * (SUB_HEAD ** -0.5 * LOG2_E)
    k = _subhead_norm_rope(proj(1), kg_ref[...], cos, sin_signed)
    v = proj(2).astype(BF16)
    for hd in range(N_HEADS):
        cols = slice(hd * LANES, (hd + 1) * LANES)
        q_ref[0, hd] = q[:, cols].astype(BF16)
        kt_ref[0, hd] = k[:, cols].T.astype(BF16)
        v_ref[0, hd] = v[:, cols]
    for j in range(3, PROJ_WIDTH // 1024):
        rest_ref[:, (j - 3) * 1024:(j - 2) * 1024] = proj(j).astype(BF16)


def _inproj(x3, gain, w_bf16, q_gain, k_gain, cos, sin_signed, tm=512):
    b, s, d = x3.shape
    n = b * s
    bps = s // tm
    const = lambda shape: pl.BlockSpec(shape, lambda i: (0, 0))
    table = pl.BlockSpec((tm, LANES), lambda i: (i % bps, 0))
    heads = pl.BlockSpec((1, N_HEADS, tm, LANES), lambda i: (i // bps, 0, i % bps, 0))
    return pl.pallas_call(
        _inproj_kernel,
        out_shape=(jax.ShapeDtypeStruct((b, N_HEADS, s, LANES), BF16),
                   jax.ShapeDtypeStruct((b, N_HEADS, LANES, s), BF16),
                   jax.ShapeDtypeStruct((b, N_HEADS, s, V_HEAD), BF16),
                   jax.ShapeDtypeStruct((n, REST_WIDTH), BF16)),
        grid=(n // tm,),
        in_specs=[
            pl.BlockSpec((tm, D_MODEL), lambda i: (i, 0)),
            const((1, D_MODEL)),
            const((D_MODEL, PROJ_WIDTH)),
            const((1, ATTN_WIDTH)), const((1, ATTN_WIDTH)), table, table,
        ],
        out_specs=(heads,
                   pl.BlockSpec((1, N_HEADS, LANES, tm), lambda i: (i // bps, 0, 0, i % bps)),
                   heads,
                   pl.BlockSpec((tm, REST_WIDTH), lambda i: (i, 0))),
        compiler_params=pltpu.CompilerParams(
            dimension_semantics=("arbitrary",), vmem_limit_bytes=VMEM_LIMIT),
        name="inproj",
    )(x3.reshape(n, d), gain, w_bf16, q_gain, k_gain, cos, sin_signed)


ATTN_SUB_ROWS = 128


SCORE_BOUND_LOG2 = 100.0


def _attn_kernel(lam_ref, sg_ref, qg_ref, kg_ref, q_ref, kt_ref, v_ref, o_ref, *, tq):
    lp = lam_ref[...]
    lam = (jnp.exp(jnp.sum(lp[0:1] * lp[1:2], axis=-1, keepdims=True))
           - jnp.exp(jnp.sum(lp[2:3] * lp[3:4], axis=-1, keepdims=True)) + LAMBDA_INIT)

    score_bound = (SUB_HEAD * (SUB_HEAD ** -0.5 * LOG2_E) * 1.01
                   * jnp.max(jnp.abs(qg_ref[...])) * jnp.max(jnp.abs(kg_ref[...])))
    ts = ATTN_SUB_ROWS

    def chains(subtract_max):
        for t in range(tq // ts):
            q = q_ref[0, 0, t * ts:(t + 1) * ts, :]
            lane = lax.broadcasted_iota(jnp.int32, q.shape, 1)
            zero = jnp.zeros_like(q)
            qq = jnp.concatenate([jnp.where(lane < SUB_HEAD, q, zero),
                                  jnp.where(lane >= SUB_HEAD, q, zero)], axis=0)
            s = jnp.dot(qq, kt_ref[0, 0], preferred_element_type=F32)
            if subtract_max:
                s = s - jnp.max(s, axis=-1, keepdims=True)
            p = jnp.exp2(s)
            l = jnp.sum(p, axis=-1, keepdims=True)
            acc = jnp.dot(p.astype(BF16), v_ref[0, 0], preferred_element_type=F32)
            o = acc / l
            o = o[0:ts] - lam * o[ts:2 * ts]
            o = _rms(o, sg_ref[...]) * (1.0 - LAMBDA_INIT)
            o_ref[0, 0, t * ts:(t + 1) * ts, :] = o.astype(BF16)

    @pl.when(score_bound <= SCORE_BOUND_LOG2)
    def _bounded_scores():
        chains(subtract_max=False)

    @pl.when(jnp.logical_not(score_bound <= SCORE_BOUND_LOG2))
    def _any_scores():
        chains(subtract_max=True)


def _attention(q_hm, kt_hm, v_hm, lam_params, subln_gain, q_gain, k_gain, tq=1024):
    b, h, s, _ = q_hm.shape
    const = lambda shape: pl.BlockSpec(shape, lambda bi, hi, qi: (0, 0))
    return pl.pallas_call(
        functools.partial(_attn_kernel, tq=tq),
        out_shape=jax.ShapeDtypeStruct((b, h, s, V_HEAD), BF16),
        grid=(b, h, s // tq),
        in_specs=[
            const((4, SUB_HEAD)), const((1, V_HEAD)), const((1, ATTN_WIDTH)), const((1, ATTN_WIDTH)),
            pl.BlockSpec((1, 1, tq, LANES), lambda bi, hi, qi: (bi, hi, qi, 0)),
            pl.BlockSpec((1, 1, LANES, s), lambda bi, hi, qi: (bi, hi, 0, 0)),
            pl.BlockSpec((1, 1, s, V_HEAD), lambda bi, hi, qi: (bi, hi, 0, 0)),
        ],
        out_specs=pl.BlockSpec((1, 1, tq, V_HEAD), lambda bi, hi, qi: (bi, hi, qi, 0)),
        compiler_params=pltpu.CompilerParams(
            dimension_semantics=("arbitrary", "arbitrary", "arbitrary"),
            vmem_limit_bytes=VMEM_LIMIT),
        name="diff_attn",
    )(lam_params, subln_gain, q_gain, k_gain, q_hm, kt_hm, v_hm)


def _softplus(x):
    return jnp.maximum(x, 0.0) + jnp.log1p(jnp.exp(-jnp.abs(x)))


def _gelu_tanh(x):
    return 0.5 * x * (1.0 + jnp.tanh(math.sqrt(2.0 / math.pi) * (x + 0.044715 * (x * x * x))))


LRU_BLOCKS_PER_STEP = 2


def _lru_kernel(x_ref, g_ref, cw_ref, cb_ref, w_ref, b_ref, lam_ref, o_ref,
                xs_ref, a0_ref, u0_ref, a1_ref, u1_ref, *, seq, tc):
    pad = SUBLANES
    width = LRU_BLOCKS_PER_STEP * LANES
    zeros_pad = jnp.zeros((pad, width), F32)
    xs_ref[0:pad, :] = zeros_pad
    xs_ref[pad + seq:pad + seq + pad, :] = zeros_pad
    xs_ref[pad:pad + seq, :] = x_ref[0].astype(F32)

    k_all = (-LRU_C * 0.5 * LOG2_E) * _softplus(-lam_ref[...])
    cw_all = cw_ref[...]
    cb_all = cb_ref[...]

    n_seg = SUBLANES
    seg = seq // n_seg
    assert tc == seg
    for c in range(n_seg):
        base = pad + c * tc
        for j in range(LRU_BLOCKS_PER_STEP):
            cols = slice(j * LANES, (j + 1) * LANES)
            cw, cb = cw_all[:, cols], cb_all[:, cols]
            win = xs_ref[base - pad:base + tc + pad, cols]
            n_win = tc + 2 * pad
            taps = (pltpu.roll(win, 1, axis=0), win,
                    pltpu.roll(win, n_win - 1, axis=0), pltpu.roll(win, n_win - 2, axis=0))
            xr = cb + sum(cw[t:t + 1] * taps[t][pad:pad + tc] for t in range(4))
            th = jnp.tanh(jnp.dot(xr.astype(BF16), w_ref[j], preferred_element_type=F32)
                          + b_ref[j])
            for d, (a_ref, u_ref) in enumerate(((a0_ref, u0_ref), (a1_ref, u1_ref))):
                k = k_all[d:d + 1, cols]
                a = jnp.exp2(k * th[:, (2 * d) * LANES:(2 * d + 1) * LANES] + k)
                gate_i = 0.5 * th[:, (2 * d + 1) * LANES:(2 * d + 2) * LANES] + 0.5
                v = 1.0 - a * a
                mult = jnp.where(v > 0.0, v * lax.rsqrt(v), 0.0)
                a_ref[j, pl.ds(c, seg, stride=SUBLANES), :] = a
                u_ref[j, pl.ds(c, seg, stride=SUBLANES), :] = mult * (gate_i * xr)

    chains = [(a0_ref, u0_ref, j, False) for j in range(LRU_BLOCKS_PER_STEP)] \
        + [(a1_ref, u1_ref, j, True) for j in range(LRU_BLOCKS_PER_STEP)]

    def step(t, carry):
        out = []
        for (a_ref, u_ref, j, reverse), (h, p) in zip(chains, carry):
            r0 = pl.multiple_of((seg - 1 - t if reverse else t) * SUBLANES, SUBLANES)
            a = a_ref[j, pl.ds(r0, SUBLANES), :]
            h = a * h + u_ref[j, pl.ds(r0, SUBLANES), :]
            p = a * p
            u_ref[j, pl.ds(r0, SUBLANES), :] = h
            a_ref[j, pl.ds(r0, SUBLANES), :] = p
            out.append((h, p))
        return tuple(out)

    zero = jnp.zeros((SUBLANES, LANES), F32)
    one = jnp.ones((SUBLANES, LANES), F32)
    ends = lax.fori_loop(0, seg, step, ((zero, one),) * len(chains), unroll=8)

    row = lax.broadcasted_iota(jnp.int32, (SUBLANES, LANES), 0)
    entering = []
    for (_, _, _, reverse), (h_end, p_end) in zip(chains, ends):
        state = zero
        order = range(SUBLANES - 2, -1, -1) if reverse else range(1, SUBLANES)
        for s in order:
            nxt = pltpu.roll(h_end + p_end * state, SUBLANES - 1 if reverse else 1, axis=0)
            state = jnp.where(row == s, nxt, state)
        entering.append(state)

    for c in range(n_seg):
        rows = slice(c * tc, (c + 1) * tc)
        for j in range(LRU_BLOCKS_PER_STEP):
            cols = slice(j * LANES, (j + 1) * LANES)
            hs = 0.0
            for (a_ref, u_ref, jj, _), state in zip(chains, entering):
                if jj == j:
                    hs = hs + (u_ref[j, pl.ds(c, seg, stride=SUBLANES), :]
                               + a_ref[j, pl.ds(c, seg, stride=SUBLANES), :] * state[c:c + 1, :])
            y = hs * _gelu_tanh(g_ref[0, rows, cols].astype(F32))
            o_ref[0, rows, cols] = y.astype(BF16)


def _lru(proj3, conv_w, conv_b, w_cat, b_cat, lam):
    b, s, _ = proj3.shape
    tc = s // SUBLANES
    kernel = functools.partial(_lru_kernel, seq=s, tc=tc)
    nb = LRU_BLOCKS_PER_STEP
    width = nb * LANES
    return pl.pallas_call(
        kernel,
        out_shape=jax.ShapeDtypeStruct((b, s, LRU_WIDTH), BF16),
        grid=(b, LRU_BLOCKS // nb),
        in_specs=[
            pl.BlockSpec((1, s, width), lambda bi, ni: (bi, 0, LRUX_COL // nb + ni)),
            pl.BlockSpec((1, s, width), lambda bi, ni: (bi, 0, LRUG_COL // nb + ni)),
            pl.BlockSpec((4, width), lambda bi, ni: (0, ni)),
            pl.BlockSpec((1, width), lambda bi, ni: (0, ni)),
            pl.BlockSpec((nb, LRU_BLOCK_DIM, 4 * LRU_BLOCK_DIM), lambda bi, ni: (ni, 0, 0)),
            pl.BlockSpec((nb, 1, 4 * LRU_BLOCK_DIM), lambda bi, ni: (ni, 0, 0)),
            pl.BlockSpec((2, width), lambda bi, ni: (0, ni)),
        ],
        out_specs=pl.BlockSpec((1, s, width), lambda bi, ni: (bi, 0, ni)),
        scratch_shapes=[pltpu.VMEM((s + 2 * SUBLANES, width), F32)]
        + [pltpu.VMEM((nb, s, LANES), F32)] * 4,
        compiler_params=pltpu.CompilerParams(
            dimension_semantics=("arbitrary", "arbitrary"), vmem_limit_bytes=VMEM_LIMIT),
        name="rglru",
    )(proj3, proj3, conv_w, conv_b, w_cat, b_cat, lam)


def _merge_kernel(x_ref, ao_ref, yl_ref, ga_ref, gl_ref, bg_ref, wa_ref, wl_ref, wo_ref,
                  n2_ref, wr_ref, br_ref, x1_ref, h2_ref, cw_ref):
    attn_o = jnp.concatenate([ao_ref[0, h] for h in range(N_HEADS)], axis=1)
    attn_d = jnp.dot(attn_o, wa_ref[...], preferred_element_type=F32)
    lru_d = jnp.dot(yl_ref[...], wl_ref[...], preferred_element_type=F32)
    bg = bg_ref[...]
    g_attn = jax.nn.sigmoid(ga_ref[...].astype(F32) + bg[:, 0:D_MODEL])
    g_lru = jax.nn.sigmoid(gl_ref[...].astype(F32) + bg[:, D_MODEL:2 * D_MODEL])
    merged = g_attn * attn_d + g_lru * lru_d
    x1 = x_ref[...] + jnp.dot(merged.astype(BF16), wo_ref[...], preferred_element_type=F32)
    x1_ref[...] = x1
    h2 = _rms(x1, n2_ref[...])
    h2_ref[...] = h2.astype(BF16)

    h2_hi = h2.astype(BF16)
    h2_lo = (h2 - h2_hi.astype(F32)).astype(BF16)
    wr = wr_ref[...]
    part = jnp.dot(h2_hi, wr, preferred_element_type=F32)
    logits = (part[:, 0:ROUTER_LANES] + part[:, ROUTER_LANES:]
              + jnp.dot(h2_lo, wr[:, 0:ROUTER_LANES], preferred_element_type=F32) + br_ref[...])
    lane = lax.broadcasted_iota(jnp.int32, logits.shape, 1)
    neg = jnp.full_like(logits, -jnp.inf)
    big = jnp.full_like(lane, ROUTER_LANES)

    def masked_max(mask):
        return jnp.max(jnp.where(mask, logits, neg), axis=-1, keepdims=True)

    def first_lane(mask, value):
        return jnp.min(jnp.where(mask & (logits == value), lane, big), axis=-1, keepdims=True)

    g_mask = lane < N_GROUPS
    g_max = masked_max(g_mask)
    g_sel = first_lane(g_mask, g_max)
    g_w = 1.0 / jnp.sum(jnp.where(g_mask, jnp.exp(logits - g_max), 0.0), axis=-1, keepdims=True)
    e_lo = EXPERT_LANE0 + g_sel * EXPERTS_PER_GROUP
    e_mask = (lane >= e_lo) & (lane < e_lo + EXPERTS_PER_GROUP)
    v1 = masked_max(e_mask)
    i1 = first_lane(e_mask, v1)
    e_mask2 = e_mask & (lane != i1)
    v2 = masked_max(e_mask2)
    i2 = first_lane(e_mask2, v2)
    t = jnp.exp(v2 - v1)
    w1 = g_w / (1.0 + t)
    w2 = g_w * t / (1.0 + t)
    cw_ref[...] = (jnp.where(lane == i1, w1, 0.0) + jnp.where(lane == i2, w2, 0.0)
                   + jnp.where(lane == SORT_CLASS_LANE, g_sel.astype(F32), 0.0))


def _merge(x2, attn_o, y_lru, proj, b_gates, wa, wl, wo, n2_gain, w_router, b_router, tm=512):
    n = x2.shape[0]
    blocks_per_seq = attn_o.shape[2] // tm
    row = lambda cols, col_blk=0: pl.BlockSpec((tm, cols), lambda i: (i, col_blk))
    const = lambda shape: pl.BlockSpec(shape, lambda i: (0, 0))
    heads = pl.BlockSpec((1, N_HEADS, tm, V_HEAD),
                         lambda i: (i // blocks_per_seq, 0, i % blocks_per_seq, 0))
    return pl.pallas_call(
        _merge_kernel,
        out_shape=(jax.ShapeDtypeStruct((n, D_MODEL), F32),
                   jax.ShapeDtypeStruct((n, D_MODEL), BF16),
                   jax.ShapeDtypeStruct((n, ROUTER_LANES), F32)),
        grid=(n // tm,),
        in_specs=[
            row(D_MODEL), heads, row(LRU_WIDTH),
            row(D_MODEL, GATE_COL_1024), row(D_MODEL, GATE_COL_1024 + 1),
            const((1, 2 * D_MODEL)),
            const((ATTN_WIDTH, D_MODEL)), const((LRU_WIDTH, D_MODEL)), const((D_MODEL, D_MODEL)),
            const((1, D_MODEL)), const((D_MODEL, 2 * ROUTER_LANES)), const((1, ROUTER_LANES)),
        ],
        out_specs=(row(D_MODEL), row(D_MODEL), row(ROUTER_LANES)),
        compiler_params=pltpu.CompilerParams(
            dimension_semantics=("arbitrary",), vmem_limit_bytes=VMEM_LIMIT),
        name="merge_router",
    )(x2, attn_o, y_lru, proj, proj, b_gates, wa, wl, wo, n2_gain, w_router, b_router)


MOE_ALIGN = 16
MOE_ALIGN_SHIFT = 4
MOE_WINDOW_ROWS = (256, 384, 512)
MOE_LARGEST_SHIFT = 9
MOE_ROW_PAD = MOE_WINDOW_ROWS[0]
MOE_TILES_PER_PASS = 1
MOE_RNG_STRIDE = 32
MOE_VMEM_LIMIT = VMEM_LIMIT


def _moe_kernel(h_ref, rt_ref, x1_ref, wg_ref, wu_ref, wd_ref, o_ref,
                xs_ref, ys_ref, pos_ref, rng_ref):
    g = pl.program_id(1)
    sub = pl.program_id(2)
    t_tile = h_ref.shape[0]
    rng0 = sub * MOE_RNG_STRIDE

    @pl.when(g == 0)
    def _sort_rows():
        rt = rt_ref[...]
        lane = lax.broadcasted_iota(jnp.int32, rt.shape, 1)
        cls = jnp.sum(jnp.where(lane == SORT_CLASS_LANE, rt, 0.0), axis=-1, keepdims=True)
        onehot = lane == cls.astype(jnp.int32)
        r = lax.broadcasted_iota(jnp.int32, (t_tile, t_tile), 0)
        c = lax.broadcasted_iota(jnp.int32, (t_tile, t_tile), 1)
        earlier = jnp.dot((r > c).astype(BF16), onehot.astype(BF16),
                          preferred_element_type=F32)
        cnt = jnp.sum(onehot.astype(F32), axis=0, keepdims=True)
        lane1 = lax.broadcasted_iota(jnp.int32, cnt.shape, 1)
        incl = cnt
        for d in (1, 2, 4, 8, 16):
            incl = incl + jnp.where(lane1 >= d, pltpu.roll(incl, d, axis=1), 0.0)
        off = incl - cnt
        pos = jnp.sum(jnp.where(onehot, earlier + off, 0.0), axis=-1, keepdims=True)
        pos_b = jnp.broadcast_to(pos, rt.shape)
        pos_ref[sub] = pos_b
        pos_row = pos_b.T[0:1, :]
        perm = (r.astype(F32) == pos_row).astype(BF16)
        rt_hi = rt.astype(BF16)
        rt_lo = (rt - rt_hi.astype(F32)).astype(BF16)
        cat = jnp.concatenate([h_ref[...], rt_hi, rt_lo], axis=1)
        xs_ref[sub, 0:t_tile, :] = jnp.dot(perm, cat, preferred_element_type=F32).astype(BF16)
        xs_ref[sub, t_tile:, :] = jnp.zeros((xs_ref.shape[1] - t_tile, xs_ref.shape[2]), BF16)
        ys_ref[sub] = jnp.zeros(ys_ref.shape[1:], F32)
        for k in range(N_SORT_CLASSES + 1):
            rng_ref[rng0 + k] = jnp.sum(jnp.where(lane1 == k, off, 0.0)).astype(jnp.int32)

    def window(r0, rows):
        r0 = pl.multiple_of(r0, MOE_ALIGN)
        xc = xs_ref[sub, pl.ds(r0, rows), 0:D_MODEL]
        rs = (xs_ref[sub, pl.ds(r0, rows), D_MODEL:D_MODEL + ROUTER_LANES].astype(F32)
              + xs_ref[sub, pl.ds(r0, rows), D_MODEL + ROUTER_LANES:].astype(F32))
        lane = lax.broadcasted_iota(jnp.int32, rs.shape, 1)
        y = jnp.zeros((rows, D_MODEL), F32)
        for e in range(EXPERTS_PER_GROUP):
            a = jnp.dot(xc, wg_ref[e], preferred_element_type=F32)
            u = jnp.dot(xc, wu_ref[e], preferred_element_type=F32)
            ce = jnp.sum(jnp.where(lane == EXPERT_LANE0 + g * EXPERTS_PER_GROUP + e, rs, 0.0),
                         axis=-1, keepdims=True)
            hid = a * jax.nn.sigmoid(a) * u * ce
            y = y + jnp.dot(hid.astype(BF16), wd_ref[e], preferred_element_type=F32)
        ys_ref[sub, pl.ds(r0, rows), :] += y

    largest = MOE_WINDOW_ROWS[-1]
    start = rng_ref[rng0 + g * CLASSES_PER_GROUP]
    stop = rng_ref[rng0 + (g + 1) * CLASSES_PER_GROUP]
    first = lax.shift_left(lax.shift_right_logical(start, MOE_ALIGN_SHIFT), MOE_ALIGN_SHIFT)
    span = jnp.where(stop > start, stop - first, 0)
    n_full = lax.shift_right_logical(span, MOE_LARGEST_SHIFT)

    def full_window(ci, carry):
        window(first + ci * largest, largest)
        return carry

    lax.fori_loop(0, n_full, full_window, 0)
    rem = span - lax.shift_left(n_full, MOE_LARGEST_SHIFT)
    rem_start = first + lax.shift_left(n_full, MOE_LARGEST_SHIFT)
    for lower, rows in zip((0,) + MOE_WINDOW_ROWS[:-1], MOE_WINDOW_ROWS):
        @pl.when((rem > lower) & (rem <= rows))
        def _remainder_window(rows=rows):
            window(rem_start, rows)

    @pl.when(g == N_GROUPS - 1)
    def _unsort_rows():
        pos_b = pos_ref[sub]
        c = lax.broadcasted_iota(jnp.int32, (t_tile, t_tile), 1).astype(F32)
        unperm = (jnp.concatenate([pos_b] * (t_tile // LANES), axis=1) == c).astype(BF16)
        o_ref[...] = x1_ref[...] + jnp.dot(unperm, ys_ref[sub, 0:t_tile, :].astype(BF16),
                                           preferred_element_type=F32)


def _moe(h2, route, x1, wg, wu, wd, tm=1024):
    n = h2.shape[0]
    last = N_GROUPS - 1
    group_w = lambda rows, cols: pl.BlockSpec((EXPERTS_PER_GROUP, rows, cols),
                                              lambda pr, g, sub: (g, 0, 0))
    sort_rows = lambda pr, g, sub: (MOE_TILES_PER_PASS * pr + jnp.where(g == 0, sub, MOE_TILES_PER_PASS - 1), 0)
    unsort_rows = lambda pr, g, sub: (MOE_TILES_PER_PASS * pr + jnp.where(g == last, sub, 0), 0)
    return pl.pallas_call(
        _moe_kernel,
        out_shape=jax.ShapeDtypeStruct((n, D_MODEL), F32),
        grid=(n // (MOE_TILES_PER_PASS * tm), N_GROUPS, MOE_TILES_PER_PASS),
        in_specs=[
            pl.BlockSpec((tm, D_MODEL), sort_rows),
            pl.BlockSpec((tm, ROUTER_LANES), sort_rows),
            pl.BlockSpec((tm, D_MODEL), unsort_rows, pipeline_mode=pl.Buffered(1)),
            group_w(D_MODEL, EXPERT_HIDDEN), group_w(D_MODEL, EXPERT_HIDDEN),
            group_w(EXPERT_HIDDEN, D_MODEL),
        ],
        out_specs=pl.BlockSpec((tm, D_MODEL), unsort_rows),
        scratch_shapes=[
            pltpu.VMEM((MOE_TILES_PER_PASS, tm + MOE_ROW_PAD, D_MODEL + 2 * ROUTER_LANES), BF16),
            pltpu.VMEM((MOE_TILES_PER_PASS, tm + MOE_ROW_PAD, D_MODEL), F32),
            pltpu.VMEM((MOE_TILES_PER_PASS, tm, ROUTER_LANES), F32),
            pltpu.SMEM((MOE_TILES_PER_PASS * MOE_RNG_STRIDE,), jnp.int32),
        ],
        compiler_params=pltpu.CompilerParams(
            dimension_semantics=("arbitrary", "arbitrary", "arbitrary"),
            vmem_limit_bytes=MOE_VMEM_LIMIT),
        name="moe",
    )(h2, route, x1, wg, wu, wd)


def _rope_tables(seq):
    pos = jnp.arange(seq, dtype=F32)
    inv_freq = ROPE_THETA ** (-jnp.arange(0, SUB_HEAD, 2, dtype=F32) / SUB_HEAD)
    ang = pos[:, None] * inv_freq[None, :]
    cos, sin = jnp.cos(ang), jnp.sin(ang)
    cos_full = jnp.concatenate([cos, cos, cos, cos], axis=-1)
    sin_signed = jnp.concatenate([-sin, sin, -sin, sin], axis=-1)
    return cos_full, sin_signed


def kernel(x, norm1_gain, w_in, b_gates, q_norm_gain, k_norm_gain, lambda_q1, lambda_k1, lambda_q2, lambda_k2, attn_subln_gain, w_attn_o, conv_w, conv_b, lru_wa, lru_ba, lru_wi, lru_bi, lru_lambda, w_lru_o, w_out, norm2_gain, w_group_router, b_group_router, w_expert_router, b_expert_router, w_expert_gate, w_expert_up, w_expert_down):
    b, s, d = x.shape
    n = b * s
    depth = w_in.shape[0]
    assert depth == 1 and d == D_MODEL
    cos, sin_signed = _rope_tables(s)
    x2 = x.reshape(n, d)
    l = 0

    sub_heads = ATTN_WIDTH // SUB_HEAD
    q_gain = jnp.tile(q_norm_gain[l], sub_heads)[None, :]
    k_gain = jnp.tile(k_norm_gain[l], sub_heads)[None, :]
    q_hm, kt_hm, v_hm, rest = _inproj(
        x, norm1_gain[l][None, :], w_in[l].astype(BF16), q_gain, k_gain, cos, sin_signed)
    proj3 = rest.reshape(b, s, REST_WIDTH)

    lam_params = jnp.stack([lambda_q1[l], lambda_k1[l], lambda_q2[l], lambda_k2[l]])
    attn_o = _attention(q_hm, kt_hm, v_hm, lam_params, attn_subln_gain[l][None, :],
                        q_gain, k_gain)

    w_cat = (0.5 * jnp.concatenate([lru_wa[l, 0], lru_wi[l, 0], lru_wa[l, 1], lru_wi[l, 1]],
                                   axis=-1)).astype(BF16)
    blk = lambda v: v.reshape(LRU_BLOCKS, 1, LRU_BLOCK_DIM)
    b_cat = 0.5 * jnp.concatenate([blk(lru_ba[l, 0]), blk(lru_bi[l, 0]),
                                   blk(lru_ba[l, 1]), blk(lru_bi[l, 1])], axis=-1)
    y_lru = _lru(proj3, conv_w[l], conv_b[l][None, :], w_cat, b_cat, lru_lambda[l])

    pad = ROUTER_LANES - N_GROUPS - N_EXPERTS
    w_router = jnp.concatenate([w_group_router[l], w_expert_router[l],
                                jnp.zeros((d, pad), F32)], axis=-1)
    b_router = jnp.concatenate([b_group_router[l], b_expert_router[l],
                                jnp.zeros((pad,), F32)])[None, :]
    w_router_hi = w_router.astype(BF16)
    w_router = jnp.concatenate(
        [w_router_hi, (w_router - w_router_hi.astype(F32)).astype(BF16)], axis=-1)
    x1, h2, cw = _merge(
        x2, attn_o, y_lru.reshape(n, LRU_WIDTH), rest,
        b_gates[l][None, :], w_attn_o[l].astype(BF16), w_lru_o[l].astype(BF16),
        w_out[l].astype(BF16), norm2_gain[l][None, :], w_router, b_router)

    out = _moe(h2, cw, x1, w_expert_gate[l].astype(BF16), w_expert_up[l].astype(BF16),
               w_expert_down[l].astype(BF16))
    return out.reshape(b, s, d)
```

```python
import functools
import math

import jax
import jax.numpy as jnp
from jax import lax
from jax.experimental import pallas as pl
from jax.experimental.pallas import tpu as pltpu

F32 = jnp.float32
BF16 = jnp.bfloat16

D_MODEL = 1024
N_HEADS = 8
SUB_HEAD = 64
V_HEAD = 128
ATTN_WIDTH = 1024
LRU_WIDTH = 1024
LRU_BLOCKS = 8
LRU_BLOCK_DIM = 128
LRU_C = 8.0
PROJ_WIDTH = 7168
N_GROUPS = 4
EXPERTS_PER_GROUP = 4
N_EXPERTS = 16
EXPERT_HIDDEN = 512
ROPE_THETA = 10000.0
RMS_EPS = 1e-6
LAMBDA_INIT = 0.8 - 0.6 * math.exp(-0.3 * 0)

LOG2_E = math.log2(math.e)
LANES = 128
SUBLANES = 8
VMEM_LIMIT = 56 * 1024 * 1024

REST_WIDTH = PROJ_WIDTH - 3 * ATTN_WIDTH
LRUX_COL, LRUG_COL = 0, 8
GATE_COL_1024 = 2

ROUTER_LANES = 128
EXPERT_LANE0 = N_GROUPS
SORT_CLASS_LANE = N_GROUPS + N_EXPERTS
CLASSES_PER_GROUP = 1
N_SORT_CLASSES = N_GROUPS * CLASSES_PER_GROUP


def _rms(x, gain):
    ms = jnp.mean(x * x, axis=-1, keepdims=True)
    return x * lax.rsqrt(ms + RMS_EPS) * gain


MXU_TILE = 256


def _subhead_norm_rope(x, gain, cos, sin_signed):
    width = x.shape[1]
    r = lax.broadcasted_iota(jnp.int32, (MXU_TILE, MXU_TILE), 0) // SUB_HEAD
    c = lax.broadcasted_iota(jnp.int32, (MXU_TILE, MXU_TILE), 1) // SUB_HEAD
    group_mean = jnp.where(r == c, 1.0 / SUB_HEAD, 0.0).astype(BF16)
    xx = (x * x).astype(BF16)
    ms = jnp.concatenate(
        [jnp.dot(xx[:, t * MXU_TILE:(t + 1) * MXU_TILE], group_mean, preferred_element_type=F32)
         for t in range(width // MXU_TILE)], axis=1)
    xn = x * lax.rsqrt(ms + RMS_EPS) * gain
    lane = lax.broadcasted_iota(jnp.int32, xn.shape, 1)
    first_half = (lane % SUB_HEAD) < (SUB_HEAD // 2)
    partner = jnp.where(first_half,
                        pltpu.roll(xn, width - SUB_HEAD // 2, axis=1),
                        pltpu.roll(xn, SUB_HEAD // 2, axis=1))
    reps = width // LANES
    return (xn * jnp.concatenate([cos] * reps, axis=1)
            + partner * jnp.concatenate([sin_signed] * reps, axis=1))


def _inproj_kernel(x_ref, g_ref, w_ref, qg_ref, kg_ref, cos_ref, sin_ref,
                   q_ref, kt_ref, v_ref, rest_ref):
    h = _rms(x_ref[...], g_ref[...]).astype(BF16)
    cos, sin_signed = cos_ref[...], sin_ref[...]

    def proj(j):
        return jnp.dot(h, w_ref[:, j * 1024:(j + 1) * 1024], preferred_element_type=F32)

    q = _subhead_norm_rope(proj(0), qg_ref[...], cos, sin_signed) * (SUB_HEAD ** -0.5 * LOG2_E)
    k = _subhead_norm_rope(proj(1), kg_ref[...], cos, sin_signed)
    v = proj(2).astype(BF16)
    for hd in range(N_HEADS):
        cols = slice(hd * LANES, (hd + 1) * LANES)
        q_ref[0, hd] = q[:, cols].astype(BF16)
        kt_ref[0, hd] = k[:, cols].T.astype(BF16)
        v_ref[0, hd] = v[:, cols]
    for j in range(3, PROJ_WIDTH // 1024):
        rest_ref[:, (j - 3) * 1024:(j - 2) * 1024] = proj(j).astype(BF16)


def _inproj(x3, gain, w_bf16, q_gain, k_gain, cos, sin_signed, tm=512):
    b, s, d = x3.shape
    n = b * s
    bps = s // tm
    const = lambda shape: pl.BlockSpec(shape, lambda i: (0, 0))
    table = pl.BlockSpec((tm, LANES), lambda i: (i % bps, 0))
    heads = pl.BlockSpec((1, N_HEADS, tm, LANES), lambda i: (i // bps, 0, i % bps, 0))
    return pl.pallas_call(
        _inproj_kernel,
        out_shape=(jax.ShapeDtypeStruct((b, N_HEADS, s, LANES), BF16),
                   jax.ShapeDtypeStruct((b, N_HEADS, LANES, s), BF16),
                   jax.ShapeDtypeStruct((b, N_HEADS, s, V_HEAD), BF16),
                   jax.ShapeDtypeStruct((n, REST_WIDTH), BF16)),
        grid=(n // tm,),
        in_specs=[
            pl.BlockSpec((tm, D_MODEL), lambda i: (i, 0)),
            const((1, D_MODEL)),
            const((D_MODEL, PROJ_WIDTH)),
            const((1, ATTN_WIDTH)), const((1, ATTN_WIDTH)), table, table,
        ],
        out_specs=(heads,
                   pl.BlockSpec((1, N_HEADS, LANES, tm), lambda i: (i // bps, 0, 0, i % bps)),
                   heads,
                   pl.BlockSpec((tm, REST_WIDTH), lambda i: (i, 0))),
        compiler_params=pltpu.CompilerParams(
            dimension_semantics=("arbitrary",), vmem_limit_bytes=VMEM_LIMIT),
        name="inproj",
    )(x3.reshape(n, d), gain, w_bf16, q_gain, k_gain, cos, sin_signed)


ATTN_SUB_ROWS = 128


SCORE_BOUND_LOG2 = 100.0


def _attn_kernel(lam_ref, sg_ref, qg_ref, kg_ref, q_ref, kt_ref, v_ref, o_ref, *, tq):
    lp = lam_ref[...]
    lam = (jnp.exp(jnp.sum(lp[0:1] * lp[1:2], axis=-1, keepdims=True))
           - jnp.exp(jnp.sum(lp[2:3] * lp[3:4], axis=-1, keepdims=True)) + LAMBDA_INIT)

    score_bound = (SUB_HEAD * (SUB_HEAD ** -0.5 * LOG2_E) * 1.01
                   * jnp.max(jnp.abs(qg_ref[...])) * jnp.max(jnp.abs(kg_ref[...])))
    ts = ATTN_SUB_ROWS

    def chains(subtract_max):
        for t in range(tq // ts):
            q = q_ref[0, 0, t * ts:(t + 1) * ts, :]
            lane = lax.broadcasted_iota(jnp.int32, q.shape, 1)
            zero = jnp.zeros_like(q)
            qq = jnp.concatenate([jnp.where(lane < SUB_HEAD, q, zero),
                                  jnp.where(lane >= SUB_HEAD, q, zero)], axis=0)
            s = jnp.dot(qq, kt_ref[0, 0], preferred_element_type=F32)
            if subtract_max:
                s = s - jnp.max(s, axis=-1, keepdims=True)
            p = jnp.exp2(s)
            l = jnp.sum(p, axis=-1, keepdims=True)
            acc = jnp.dot(p.astype(BF16), v_ref[0, 0], preferred_element_type=F32)
            o = acc / l
            o = o[0:ts] - lam * o[ts:2 * ts]
            o = _rms(o, sg_ref[...]) * (1.0 - LAMBDA_INIT)
            o_ref[0, 0, t * ts:(t + 1) * ts, :] = o.astype(BF16)

    @pl.when(score_bound <= SCORE_BOUND_LOG2)
    def _bounded_scores():
        chains(subtract_max=False)

    @pl.when(jnp.logical_not(score_bound <= SCORE_BOUND_LOG2))
    def _any_scores():
        chains(subtract_max=True)


def _attention(q_hm, kt_hm, v_hm, lam_params, subln_gain, q_gain, k_gain, tq=1024):
    b, h, s, _ = q_hm.shape
    const = lambda shape: pl.BlockSpec(shape, lambda bi, hi, qi: (0, 0))
    return pl.pallas_call(
        functools.partial(_attn_kernel, tq=tq),
        out_shape=jax.ShapeDtypeStruct((b, h, s, V_HEAD), BF16),
        grid=(b, h, s // tq),
        in_specs=[
            const((4, SUB_HEAD)), const((1, V_HEAD)), const((1, ATTN_WIDTH)), const((1, ATTN_WIDTH)),
            pl.BlockSpec((1, 1, tq, LANES), lambda bi, hi, qi: (bi, hi, qi, 0)),
            pl.BlockSpec((1, 1, LANES, s), lambda bi, hi, qi: (bi, hi, 0, 0)),
            pl.BlockSpec((1, 1, s, V_HEAD), lambda bi, hi, qi: (bi, hi, 0, 0)),
        ],
        out_specs=pl.BlockSpec((1, 1, tq, V_HEAD), lambda bi, hi, qi: (bi, hi, qi, 0)),
        compiler_params=pltpu.CompilerParams(
            dimension_semantics=("arbitrary", "arbitrary", "arbitrary"),
            vmem_limit_bytes=VMEM_LIMIT),
        name="diff_attn",
    )(lam_params, subln_gain, q_gain, k_gain, q_hm, kt_hm, v_hm)


def _softplus(x):
    return jnp.maximum(x, 0.0) + jnp.log1p(jnp.exp(-jnp.abs(x)))


def _gelu_tanh(x):
    return 0.5 * x * (1.0 + jnp.tanh(math.sqrt(2.0 / math.pi) * (x + 0.044715 * (x * x * x))))


LRU_BLOCKS_PER_STEP = 2


def _lru_kernel(x_ref, g_ref, cw_ref, cb_ref, w_ref, b_ref, lam_ref, o_ref,
                xs_ref, a0_ref, u0_ref, a1_ref, u1_ref, *, seq, tc):
    pad = SUBLANES
    width = LRU_BLOCKS_PER_STEP * LANES
    zeros_pad = jnp.zeros((pad, width), F32)
    xs_ref[0:pad, :] = zeros_pad
    xs_ref[pad + seq:pad + seq + pad, :] = zeros_pad
    xs_ref[pad:pad + seq, :] = x_ref[0].astype(F32)

    k_all = (-LRU_C * 0.5 * LOG2_E) * _softplus(-lam_ref[...])
    cw_all = cw_ref[...]
    cb_all = cb_ref[...]

    n_seg = SUBLANES
    seg = seq // n_seg
    assert tc == seg
    for c in range(n_seg):
        base = pad + c * tc
        for j in range(LRU_BLOCKS_PER_STEP):
            cols = slice(j * LANES, (j + 1) * LANES)
            cw, cb = cw_all[:, cols], cb_all[:, cols]
            win = xs_ref[base - pad:base + tc + pad, cols]
            n_win = tc + 2 * pad
            taps = (pltpu.roll(win, 1, axis=0), win,
                    pltpu.roll(win, n_win - 1, axis=0), pltpu.roll(win, n_win - 2, axis=0))
            xr = cb + sum(cw[t:t + 1] * taps[t][pad:pad + tc] for t in range(4))
            th = jnp.tanh(jnp.dot(xr.astype(BF16), w_ref[j], preferred_element_type=F32)
                          + b_ref[j])
            for d, (a_ref, u_ref) in enumerate(((a0_ref, u0_ref), (a1_ref, u1_ref))):
                k = k_all[d:d + 1, cols]
                a = jnp.exp2(k * th[:, (2 * d) * LANES:(2 * d + 1) * LANES] + k)
                gate_i = 0.5 * th[:, (2 * d + 1) * LANES:(2 * d + 2) * LANES] + 0.5
                v = 1.0 - a * a
                mult = jnp.where(v > 0.0, v * lax.rsqrt(v), 0.0)
                a_ref[j, pl.ds(c, seg, stride=SUBLANES), :] = a
                u_ref[j, pl.ds(c, seg, stride=SUBLANES), :] = mult * (gate_i * xr)

    chains = [(a0_ref, u0_ref, j, False) for j in range(LRU_BLOCKS_PER_STEP)] \
        + [(a1_ref, u1_ref, j, True) for j in range(LRU_BLOCKS_PER_STEP)]

    def step(t, carry):
        out = []
        for (a_ref, u_ref, j, reverse), (h, p) in zip(chains, carry):
            r0 = pl.multiple_of((seg - 1 - t if reverse else t) * SUBLANES, SUBLANES)
            a = a_ref[j, pl.ds(r0, SUBLANES), :]
            h = a * h + u_ref[j, pl.ds(r0, SUBLANES), :]
            p = a * p
            u_ref[j, pl.ds(r0, SUBLANES), :] = h
            a_ref[j, pl.ds(r0, SUBLANES), :] = p
            out.append((h, p))
        return tuple(out)

    zero = jnp.zeros((SUBLANES, LANES), F32)
    one = jnp.ones((SUBLANES, LANES), F32)
    ends = lax.fori_loop(0, seg, step, ((zero, one),) * len(chains), unroll=8)

    row = lax.broadcasted_iota(jnp.int32, (SUBLANES, LANES), 0)
    entering = []
    for (_, _, _, reverse), (h_end, p_end) in zip(chains, ends):
        state = zero
        order = range(SUBLANES - 2, -1, -1) if reverse else range(1, SUBLANES)
        for s in order:
            nxt = pltpu.roll(h_end + p_end * state, SUBLANES - 1 if reverse else 1, axis=0)
            state = jnp.where(row == s, nxt, state)
        entering.append(state)

    for c in range(n_seg):
        rows = slice(c * tc, (c + 1) * tc)
        for j in range(LRU_BLOCKS_PER_STEP):
            cols = slice(j * LANES, (j + 1) * LANES)
            hs = 0.0
            for (a_ref, u_ref, jj, _), state in zip(chains, entering):
                if jj == j:
                    hs = hs + (u_ref[j, pl.ds(c, seg, stride=SUBLANES), :]
                               + a_ref[j, pl.ds(c, seg, stride=SUBLANES), :] * state[c:c + 1, :])
            y = hs * _gelu_tanh(g_ref[0, rows, cols].astype(F32))
            o_ref[0, rows, cols] = y.astype(BF16)


def _lru(proj3, conv_w, conv_b, w_cat, b_cat, lam):
    b, s, _ = proj3.shape
    tc = s // SUBLANES
    kernel = functools.partial(_lru_kernel, seq=s, tc=tc)
    nb = LRU_BLOCKS_PER_STEP
    width = nb * LANES
    return pl.pallas_call(
        kernel,
        out_shape=jax.ShapeDtypeStruct((b, s, LRU_WIDTH), BF16),
        grid=(b, LRU_BLOCKS // nb),
        in_specs=[
            pl.BlockSpec((1, s, width), lambda bi, ni: (bi, 0, LRUX_COL // nb + ni)),
            pl.BlockSpec((1, s, width), lambda bi, ni: (bi, 0, LRUG_COL // nb + ni)),
            pl.BlockSpec((4, width), lambda bi, ni: (0, ni)),
            pl.BlockSpec((1, width), lambda bi, ni: (0, ni)),
            pl.BlockSpec((nb, LRU_BLOCK_DIM, 4 * LRU_BLOCK_DIM), lambda bi, ni: (ni, 0, 0)),
            pl.BlockSpec((nb, 1, 4 * LRU_BLOCK_DIM), lambda bi, ni: (ni, 0, 0)),
            pl.BlockSpec((2, width), lambda bi, ni: (0, ni)),
        ],
        out_specs=pl.BlockSpec((1, s, width), lambda bi, ni: (bi, 0, ni)),
        scratch_shapes=[pltpu.VMEM((s + 2 * SUBLANES, width), F32)]
        + [pltpu.VMEM((nb, s, LANES), F32)] * 4,
        compiler_params=pltpu.CompilerParams(
            dimension_semantics=("arbitrary", "arbitrary"), vmem_limit_bytes=VMEM_LIMIT),
        name="rglru",
    )(proj3, proj3, conv_w, conv_b, w_cat, b_cat, lam)


def _merge_kernel(x_ref, ao_ref, yl_ref, ga_ref, gl_ref, bg_ref, wa_ref, wl_ref, wo_ref,
                  n2_ref, wr_ref, br_ref, x1_ref, h2_ref, cw_ref):
    attn_o = jnp.concatenate([ao_ref[0, h] for h in range(N_HEADS)], axis=1)
    attn_d = jnp.dot(attn_o, wa_ref[...], preferred_element_type=F32)
    lru_d = jnp.dot(yl_ref[...], wl_ref[...], preferred_element_type=F32)
    bg = bg_ref[...]
    g_attn = jax.nn.sigmoid(ga_ref[...].astype(F32) + bg[:, 0:D_MODEL])
    g_lru = jax.nn.sigmoid(gl_ref[...].astype(F32) + bg[:, D_MODEL:2 * D_MODEL])
    merged = g_attn * attn_d + g_lru * lru_d
    x1 = x_ref[...] + jnp.dot(merged.astype(BF16), wo_ref[...], preferred_element_type=F32)
    x1_ref[...] = x1
    h2 = _rms(x1, n2_ref[...])
    h2_ref[...] = h2.astype(BF16)

    h2_hi = h2.astype(BF16)
    h2_lo = (h2 - h2_hi.astype(F32)).astype(BF16)
    wr = wr_ref[...]
    part = jnp.dot(h2_hi, wr, preferred_element_type=F32)
    logits = (part[:, 0:ROUTER_LANES] + part[:, ROUTER_LANES:]
              + jnp.dot(h2_lo, wr[:, 0:ROUTER_LANES], preferred_element_type=F32) + br_ref[...])
    lane = lax.broadcasted_iota(jnp.int32, logits.shape, 1)
    neg = jnp.full_like(logits, -jnp.inf)
    big = jnp.full_like(lane, ROUTER_LANES)

    def masked_max(mask):
        return jnp.max(jnp.where(mask, logits, neg), axis=-1, keepdims=True)

    def first_lane(mask, value):
        return jnp.min(jnp.where(mask & (logits == value), lane, big), axis=-1, keepdims=True)

    g_mask = lane < N_GROUPS
    g_max = masked_max(g_mask)
    g_sel = first_lane(g_mask, g_max)
    g_w = 1.0 / jnp.sum(jnp.where(g_mask, jnp.exp(logits - g_max), 0.0), axis=-1, keepdims=True)
    e_lo = EXPERT_LANE0 + g_sel * EXPERTS_PER_GROUP
    e_mask = (lane >= e_lo) & (lane < e_lo + EXPERTS_PER_GROUP)
    v1 = masked_max(e_mask)
    i1 = first_lane(e_mask, v1)
    e_mask2 = e_mask & (lane != i1)
    v2 = masked_max(e_mask2)
    i2 = first_lane(e_mask2, v2)
    t = jnp.exp(v2 - v1)
    w1 = g_w / (1.0 + t)
    w2 = g_w * t / (1.0 + t)
    cw_ref[...] = (jnp.where(lane == i1, w1, 0.0) + jnp.where(lane == i2, w2, 0.0)
                   + jnp.where(lane == SORT_CLASS_LANE, g_sel.astype(F32), 0.0))


def _merge(x2, attn_o, y_lru, proj, b_gates, wa, wl, wo, n2_gain, w_router, b_router, tm=512):
    n = x2.shape[0]
    blocks_per_seq = attn_o.shape[2] // tm
    row = lambda cols, col_blk=0: pl.BlockSpec((tm, cols), lambda i: (i, col_blk))
    const = lambda shape: pl.BlockSpec(shape, lambda i: (0, 0))
    heads = pl.BlockSpec((1, N_HEADS, tm, V_HEAD),
                         lambda i: (i // blocks_per_seq, 0, i % blocks_per_seq, 0))
    return pl.pallas_call(
        _merge_kernel,
        out_shape=(jax.ShapeDtypeStruct((n, D_MODEL), F32),
                   jax.ShapeDtypeStruct((n, D_MODEL), BF16),
                   jax.ShapeDtypeStruct((n, ROUTER_LANES), F32)),
        grid=(n // tm,),
        in_specs=[
            row(D_MODEL), heads, row(LRU_WIDTH),
            row(D_MODEL, GATE_COL_1024), row(D_MODEL, GATE_COL_1024 + 1),
            const((1, 2 * D_MODEL)),
            const((ATTN_WIDTH, D_MODEL)), const((LRU_WIDTH, D_MODEL)), const((D_MODEL, D_MODEL)),
            const((1, D_MODEL)), const((D_MODEL, 2 * ROUTER_LANES)), const((1, ROUTER_LANES)),
        ],
        out_specs=(row(D_MODEL), row(D_MODEL), row(ROUTER_LANES)),
        compiler_params=pltpu.CompilerParams(
            dimension_semantics=("arbitrary",), vmem_limit_bytes=VMEM_LIMIT),
        name="merge_router",
    )(x2, attn_o, y_lru, proj, proj, b_gates, wa, wl, wo, n2_gain, w_router, b_router)


MOE_ALIGN = 16
MOE_ALIGN_SHIFT = 4
MOE_WINDOW_ROWS = (256, 320, 384, 512)
MOE_LARGEST_SHIFT = 9
MOE_ROW_PAD = MOE_WINDOW_ROWS[0]
MOE_TILES_PER_PASS = 1
MOE_RNG_STRIDE = 32
MOE_VMEM_LIMIT = VMEM_LIMIT


def _moe_kernel(h_ref, rt_ref, x1_ref, wg_ref, wu_ref, wd_ref, o_ref,
                xs_ref, ys_ref, pos_ref, rng_ref):
    g = pl.program_id(1)
    sub = pl.program_id(2)
    t_tile = h_ref.shape[0]
    rng0 = sub * MOE_RNG_STRIDE

    @pl.when(g == 0)
    def _sort_rows():
        rt = rt_ref[...]
        lane = lax.broadcasted_iota(jnp.int32, rt.shape, 1)
        cls = jnp.sum(jnp.where(lane == SORT_CLASS_LANE, rt, 0.0), axis=-1, keepdims=True)
        onehot = lane == cls.astype(jnp.int32)
        r = lax.broadcasted_iota(jnp.int32, (t_tile, t_tile), 0)
        c = lax.broadcasted_iota(jnp.int32, (t_tile, t_tile), 1)
        earlier = jnp.dot((r > c).astype(BF16), onehot.astype(BF16),
                          preferred_element_type=F32)
        cnt = jnp.sum(onehot.astype(F32), axis=0, keepdims=True)
        lane1 = lax.broadcasted_iota(jnp.int32, cnt.shape, 1)
        incl = cnt
        for d in (1, 2, 4, 8, 16):
            incl = incl + jnp.where(lane1 >= d, pltpu.roll(incl, d, axis=1), 0.0)
        off = incl - cnt
        pos = jnp.sum(jnp.where(onehot, earlier + off, 0.0), axis=-1, keepdims=True)
        pos_b = jnp.broadcast_to(pos, rt.shape)
        pos_ref[sub] = pos_b
        pos_row = pos_b.T[0:1, :]
        perm = (r.astype(F32) == pos_row).astype(BF16)
        rt_hi = rt.astype(BF16)
        rt_lo = (rt - rt_hi.astype(F32)).astype(BF16)
        cat = jnp.concatenate([h_ref[...], rt_hi, rt_lo], axis=1)
        xs_ref[sub, 0:t_tile, :] = jnp.dot(perm, cat, preferred_element_type=F32).astype(BF16)
        xs_ref[sub, t_tile:, :] = jnp.zeros((xs_ref.shape[1] - t_tile, xs_ref.shape[2]), BF16)
        ys_ref[sub] = jnp.zeros(ys_ref.shape[1:], F32)
        for k in range(N_SORT_CLASSES + 1):
            rng_ref[rng0 + k] = jnp.sum(jnp.where(lane1 == k, off, 0.0)).astype(jnp.int32)

    def window(r0, rows):
        r0 = pl.multiple_of(r0, MOE_ALIGN)
        xc = xs_ref[sub, pl.ds(r0, rows), 0:D_MODEL]
        rs = (xs_ref[sub, pl.ds(r0, rows), D_MODEL:D_MODEL + ROUTER_LANES].astype(F32)
              + xs_ref[sub, pl.ds(r0, rows), D_MODEL + ROUTER_LANES:].astype(F32))
        lane = lax.broadcasted_iota(jnp.int32, rs.shape, 1)
        y = jnp.zeros((rows, D_MODEL), F32)
        for e in range(EXPERTS_PER_GROUP):
            a = jnp.dot(xc, wg_ref[e], preferred_element_type=F32)
            u = jnp.dot(xc, wu_ref[e], preferred_element_type=F32)
            ce = jnp.sum(jnp.where(lane == EXPERT_LANE0 + g * EXPERTS_PER_GROUP + e, rs, 0.0),
                         axis=-1, keepdims=True)
            hid = a * jax.nn.sigmoid(a) * u * ce
            y = y + jnp.dot(hid.astype(BF16), wd_ref[e], preferred_element_type=F32)
        ys_ref[sub, pl.ds(r0, rows), :] += y

    largest = MOE_WINDOW_ROWS[-1]
    start = rng_ref[rng0 + g * CLASSES_PER_GROUP]
    stop = rng_ref[rng0 + (g + 1) * CLASSES_PER_GROUP]
    first = lax.shift_left(lax.shift_right_logical(start, MOE_ALIGN_SHIFT), MOE_ALIGN_SHIFT)
    span = jnp.where(stop > start, stop - first, 0)
    n_full = lax.shift_right_logical(span, MOE_LARGEST_SHIFT)

    def full_window(ci, carry):
        window(first + ci * largest, largest)
        return carry

    lax.fori_loop(0, n_full, full_window, 0)
    rem = span - lax.shift_left(n_full, MOE_LARGEST_SHIFT)
    rem_start = first + lax.shift_left(n_full, MOE_LARGEST_SHIFT)
    for lower, rows in zip((0,) + MOE_WINDOW_ROWS[:-1], MOE_WINDOW_ROWS):
        @pl.when((rem > lower) & (rem <= rows))
        def _remainder_window(rows=rows):
            window(rem_start, rows)

    @pl.when(g == N_GROUPS - 1)
    def _unsort_rows():
        pos_b = pos_ref[sub]
        c = lax.broadcasted_iota(jnp.int32, (t_tile, t_tile), 1).astype(F32)
        unperm = (jnp.concatenate([pos_b] * (t_tile // LANES), axis=1) == c).astype(BF16)
        o_ref[...] = x1_ref[...] + jnp.dot(unperm, ys_ref[sub, 0:t_tile, :].astype(BF16),
                                           preferred_element_type=F32)


def _moe(h2, route, x1, wg, wu, wd, tm=1024):
    n = h2.shape[0]
    last = N_GROUPS - 1
    group_w = lambda rows, cols: pl.BlockSpec((EXPERTS_PER_GROUP, rows, cols),
                                              lambda pr, g, sub: (g, 0, 0))
    sort_rows = lambda pr, g, sub: (MOE_TILES_PER_PASS * pr + jnp.where(g == 0, sub, MOE_TILES_PER_PASS - 1), 0)
    unsort_rows = lambda pr, g, sub: (MOE_TILES_PER_PASS * pr + jnp.where(g == last, sub, 0), 0)
    return pl.pallas_call(
        _moe_kernel,
        out_shape=jax.ShapeDtypeStruct((n, D_MODEL), F32),
        grid=(n // (MOE_TILES_PER_PASS * tm), N_GROUPS, MOE_TILES_PER_PASS),
        in_specs=[
            pl.BlockSpec((tm, D_MODEL), sort_rows),
            pl.BlockSpec((tm, ROUTER_LANES), sort_rows),
            pl.BlockSpec((tm, D_MODEL), unsort_rows, pipeline_mode=pl.Buffered(1)),
            group_w(D_MODEL, EXPERT_HIDDEN), group_w(D_MODEL, EXPERT_HIDDEN),
            group_w(EXPERT_HIDDEN, D_MODEL),
        ],
        out_specs=pl.BlockSpec((tm, D_MODEL), unsort_rows),
        scratch_shapes=[
            pltpu.VMEM((MOE_TILES_PER_PASS, tm + MOE_ROW_PAD, D_MODEL + 2 * ROUTER_LANES), BF16),
            pltpu.VMEM((MOE_TILES_PER_PASS, tm + MOE_ROW_PAD, D_MODEL), F32),
            pltpu.VMEM((MOE_TILES_PER_PASS, tm, ROUTER_LANES), F32),
            pltpu.SMEM((MOE_TILES_PER_PASS * MOE_RNG_STRIDE,), jnp.int32),
        ],
        compiler_params=pltpu.CompilerParams(
            dimension_semantics=("arbitrary", "arbitrary", "arbitrary"),
            vmem_limit_bytes=MOE_VMEM_LIMIT),
        name="moe",
    )(h2, route, x1, wg, wu, wd)


def _rope_tables(seq):
    pos = jnp.arange(seq, dtype=F32)
    inv_freq = ROPE_THETA ** (-jnp.arange(0, SUB_HEAD, 2, dtype=F32) / SUB_HEAD)
    ang = pos[:, None] * inv_freq[None, :]
    cos, sin = jnp.cos(ang), jnp.sin(ang)
    cos_full = jnp.concatenate([cos, cos, cos, cos], axis=-1)
    sin_signed = jnp.concatenate([-sin, sin, -sin, sin], axis=-1)
    return cos_full, sin_signed


def kernel(x, norm1_gain, w_in, b_gates, q_norm_gain, k_norm_gain, lambda_q1, lambda_k1, lambda_q2, lambda_k2, attn_subln_gain, w_attn_o, conv_w, conv_b, lru_wa, lru_ba, lru_wi, lru_bi, lru_lambda, w_lru_o, w_out, norm2_gain, w_group_router, b_group_router, w_expert_router, b_expert_router, w_expert_gate, w_expert_up, w_expert_down):
    b, s, d = x.shape
    n = b * s
    depth = w_in.shape[0]
    assert depth == 1 and d == D_MODEL
    cos, sin_signed = _rope_tables(s)
    x2 = x.reshape(n, d)
    l = 0

    sub_heads = ATTN_WIDTH // SUB_HEAD
    q_gain = jnp.tile(q_norm_gain[l], sub_heads)[None, :]
    k_gain = jnp.tile(k_norm_gain[l], sub_heads)[None, :]
    q_hm, kt_hm, v_hm, rest = _inproj(
        x, norm1_gain[l][None, :], w_in[l].astype(BF16), q_gain, k_gain, cos, sin_signed)
    proj3 = rest.reshape(b, s, REST_WIDTH)

    lam_params = jnp.stack([lambda_q1[l], lambda_k1[l], lambda_q2[l], lambda_k2[l]])
    attn_o = _attention(q_hm, kt_hm, v_hm, lam_params, attn_subln_gain[l][None, :],
                        q_gain, k_gain)

    w_cat = (0.5 * jnp.concatenate([lru_wa[l, 0], lru_wi[l, 0], lru_wa[l, 1], lru_wi[l, 1]],
                                   axis=-1)).astype(BF16)
    blk = lambda v: v.reshape(LRU_BLOCKS, 1, LRU_BLOCK_DIM)
    b_cat = 0.5 * jnp.concatenate([blk(lru_ba[l, 0]), blk(lru_bi[l, 0]),
                                   blk(lru_ba[l, 1]), blk(lru_bi[l, 1])], axis=-1)
    y_lru = _lru(proj3, conv_w[l], conv_b[l][None, :], w_cat, b_cat, lru_lambda[l])

    pad = ROUTER_LANES - N_GROUPS - N_EXPERTS
    w_router = jnp.concatenate([w_group_router[l], w_expert_router[l],
                                jnp.zeros((d, pad), F32)], axis=-1)
    b_router = jnp.concatenate([b_group_router[l], b_expert_router[l],
                                jnp.zeros((pad,), F32)])[None, :]
    w_router_hi = w_router.astype(BF16)
    w_router = jnp.concatenate(
        [w_router_hi, (w_router - w_router_hi.astype(F32)).astype(BF16)], axis=-1)
    x1, h2, cw = _merge(
        x2, attn_o, y_lru.reshape(n, LRU_WIDTH), rest,
        b_gates[l][None, :], w_attn_o[l].astype(BF16), w_lru_o[l].astype(BF16),
        w_out[l].astype(BF16), norm2_gain[l][None, :], w_router, b_router)

    out = _moe(h2, cw, x1, w_expert_gate[l].astype(BF16), w_expert_up[l].astype(BF16),
               w_expert_down[l].astype(BF16))
    return out.reshape(b, s, d)
```

```python
import functools
import math

import jax
import jax.numpy as jnp
from jax import lax
from jax.experimental import pallas as pl
from jax.experimental.pallas import tpu as pltpu

F32 = jnp.float32
BF16 = jnp.bfloat16

D_MODEL = 1024
N_HEADS = 8
SUB_HEAD = 64
V_HEAD = 128
ATTN_WIDTH = 1024
LRU_WIDTH = 1024
LRU_BLOCKS = 8
LRU_BLOCK_DIM = 128
LRU_C = 8.0
PROJ_WIDTH = 7168
N_GROUPS = 4
EXPERTS_PER_GROUP = 4
N_EXPERTS = 16
EXPERT_HIDDEN = 512
ROPE_THETA = 10000.0
RMS_EPS = 1e-6
LAMBDA_INIT = 0.8 - 0.6 * math.exp(-0.3 * 0)

LOG2_E = math.log2(math.e)
LANES = 128
SUBLANES = 8
VMEM_LIMIT = 56 * 1024 * 1024

REST_WIDTH = PROJ_WIDTH - 3 * ATTN_WIDTH
LRUX_COL, LRUG_COL = 0, 8
GATE_COL_1024 = 2

ROUTER_LANES = 128
EXPERT_LANE0 = N_GROUPS
SORT_CLASS_LANE = N_GROUPS + N_EXPERTS
CLASSES_PER_GROUP = 1
N_SORT_CLASSES = N_GROUPS * CLASSES_PER_GROUP


def _rms(x, gain):
    ms = jnp.mean(x * x, axis=-1, keepdims=True)
    return x * lax.rsqrt(ms + RMS_EPS) * gain


MXU_TILE = 256


def _subhead_norm_rope(x, gain, cos, sin_signed):
    width = x.shape[1]
    r = lax.broadcasted_iota(jnp.int32, (MXU_TILE, MXU_TILE), 0) // SUB_HEAD
    c = lax.broadcasted_iota(jnp.int32, (MXU_TILE, MXU_TILE), 1) // SUB_HEAD
    group_mean = jnp.where(r == c, 1.0 / SUB_HEAD, 0.0).astype(BF16)
    xx = (x * x).astype(BF16)
    ms = jnp.concatenate(
        [jnp.dot(xx[:, t * MXU_TILE:(t + 1) * MXU_TILE], group_mean, preferred_element_type=F32)
         for t in range(width // MXU_TILE)], axis=1)
    xn = x * lax.rsqrt(ms + RMS_EPS) * gain
    lane = lax.broadcasted_iota(jnp.int32, xn.shape, 1)
    first_half = (lane % SUB_HEAD) < (SUB_HEAD // 2)
    partner = jnp.where(first_half,
                        pltpu.roll(xn, width - SUB_HEAD // 2, axis=1),
                        pltpu.roll(xn, SUB_HEAD // 2, axis=1))
    reps = width // LANES
    return (xn * jnp.concatenate([cos] * reps, axis=1)
            + partner * jnp.concatenate([sin_signed] * reps, axis=1))


def _inproj_kernel(x_ref, g_ref, w_ref, qg_ref, kg_ref, cos_ref, sin_ref,
                   q_ref, kt_ref, v_ref, rest_ref):
    h = _rms(x_ref[...], g_ref[...]).astype(BF16)
    cos, sin_signed = cos_ref[...], sin_ref[...]

    def proj(j):
        return jnp.dot(h, w_ref[:, j * 1024:(j + 1) * 1024], preferred_element_type=F32)

    q = _subhead_norm_rope(proj(0), qg_ref[...], cos, sin_signed) * (SUB_HEAD ** -0.5 * LOG2_E)
    k = _subhead_norm_rope(proj(1), kg_ref[...], cos, sin_signed)
    v = proj(2).astype(BF16)
    for hd in range(N_HEADS):
        cols = slice(hd * LANES, (hd + 1) * LANES)
        q_ref[0, hd] = q[:, cols].astype(BF16)
        kt_ref[0, hd] = k[:, cols].T.astype(BF16)
        v_ref[0, hd] = v[:, cols]
    for j in range(3, PROJ_WIDTH // 1024):
        rest_ref[:, (j - 3) * 1024:(j - 2) * 1024] = proj(j).astype(BF16)


def _inproj(x3, gain, w_bf16, q_gain, k_gain, cos, sin_signed, tm=512):
    b, s, d = x3.shape
    n = b * s
    bps = s // tm
    const = lambda shape: pl.BlockSpec(shape, lambda i: (0, 0))
    table = pl.BlockSpec((tm, LANES), lambda i: (i % bps, 0))
    heads = pl.BlockSpec((1, N_HEADS, tm, LANES), lambda i: (i // bps, 0, i % bps, 0))
    return pl.pallas_call(
        _inproj_kernel,
        out_shape=(jax.ShapeDtypeStruct((b, N_HEADS, s, LANES), BF16),
                   jax.ShapeDtypeStruct((b, N_HEADS, LANES, s), BF16),
                   jax.ShapeDtypeStruct((b, N_HEADS, s, V_HEAD), BF16),
                   jax.ShapeDtypeStruct((n, REST_WIDTH), BF16)),
        grid=(n // tm,),
        in_specs=[
            pl.BlockSpec((tm, D_MODEL), lambda i: (i, 0)),
            const((1, D_MODEL)),
            const((D_MODEL, PROJ_WIDTH)),
            const((1, ATTN_WIDTH)), const((1, ATTN_WIDTH)), table, table,
        ],
        out_specs=(heads,
                   pl.BlockSpec((1, N_HEADS, LANES, tm), lambda i: (i // bps, 0, 0, i % bps)),
                   heads,
                   pl.BlockSpec((tm, REST_WIDTH), lambda i: (i, 0))),
        compiler_params=pltpu.CompilerParams(
            dimension_semantics=("arbitrary",), vmem_limit_bytes=VMEM_LIMIT),
        name="inproj",
    )(x3.reshape(n, d), gain, w_bf16, q_gain, k_gain, cos, sin_signed)


ATTN_SUB_ROWS = 128


SCORE_BOUND_LOG2 = 100.0


def _attn_kernel(lam_ref, sg_ref, qg_ref, kg_ref, q_ref, kt_ref, v_ref, o_ref, *, tq):
    lp = lam_ref[...]
    lam = (jnp.exp(jnp.sum(lp[0:1] * lp[1:2], axis=-1, keepdims=True))
           - jnp.exp(jnp.sum(lp[2:3] * lp[3:4], axis=-1, keepdims=True)) + LAMBDA_INIT)

    score_bound = (SUB_HEAD * (SUB_HEAD ** -0.5 * LOG2_E) * 1.01
                   * jnp.max(jnp.abs(qg_ref[...])) * jnp.max(jnp.abs(kg_ref[...])))
    ts = ATTN_SUB_ROWS

    def chains(subtract_max):
        for t in range(tq // ts):
            q = q_ref[0, 0, t * ts:(t + 1) * ts, :]
            lane = lax.broadcasted_iota(jnp.int32, q.shape, 1)
            zero = jnp.zeros_like(q)
            qq = jnp.concatenate([jnp.where(lane < SUB_HEAD, q, zero),
                                  jnp.where(lane >= SUB_HEAD, q, zero)], axis=0)
            s = jnp.dot(qq, kt_ref[0, 0], preferred_element_type=F32)
            if subtract_max:
                s = s - jnp.max(s, axis=-1, keepdims=True)
            p = jnp.exp2(s)
            l = jnp.sum(p, axis=-1, keepdims=True)
            acc = jnp.dot(p.astype(BF16), v_ref[0, 0], preferred_element_type=F32)
            o = acc / l
            o = o[0:ts] - lam * o[ts:2 * ts]
            o = _rms(o, sg_ref[...]) * (1.0 - LAMBDA_INIT)
            o_ref[0, 0, t * ts:(t + 1) * ts, :] = o.astype(BF16)

    @pl.when(score_bound <= SCORE_BOUND_LOG2)
    def _bounded_scores():
        chains(subtract_max=False)

    @pl.when(jnp.logical_not(score_bound <= SCORE_BOUND_LOG2))
    def _any_scores():
        chains(subtract_max=True)


def _attention(q_hm, kt_hm, v_hm, lam_params, subln_gain, q_gain, k_gain, tq=1024):
    b, h, s, _ = q_hm.shape
    const = lambda shape: pl.BlockSpec(shape, lambda bi, hi, qi: (0, 0))
    return pl.pallas_call(
        functools.partial(_attn_kernel, tq=tq),
        out_shape=jax.ShapeDtypeStruct((b, h, s, V_HEAD), BF16),
        grid=(b, h, s // tq),
        in_specs=[
            const((4, SUB_HEAD)), const((1, V_HEAD)), const((1, ATTN_WIDTH)), const((1, ATTN_WIDTH)),
            pl.BlockSpec((1, 1, tq, LANES), lambda bi, hi, qi: (bi, hi, qi, 0)),
            pl.BlockSpec((1, 1, LANES, s), lambda bi, hi, qi: (bi, hi, 0, 0)),
            pl.BlockSpec((1, 1, s, V_HEAD), lambda bi, hi, qi: (bi, hi, 0, 0)),
        ],
        out_specs=pl.BlockSpec((1, 1, tq, V_HEAD), lambda bi, hi, qi: (bi, hi, qi, 0)),
        compiler_params=pltpu.CompilerParams(
            dimension_semantics=("arbitrary", "arbitrary", "arbitrary"),
            vmem_limit_bytes=VMEM_LIMIT),
        name="diff_attn",
    )(lam_params, subln_gain, q_gain, k_gain, q_hm, kt_hm, v_hm)


def _softplus(x):
    return jnp.maximum(x, 0.0) + jnp.log1p(jnp.exp(-jnp.abs(x)))


def _gelu_tanh(x):
    return 0.5 * x * (1.0 + jnp.tanh(math.sqrt(2.0 / math.pi) * (x + 0.044715 * (x * x * x))))


LRU_BLOCKS_PER_STEP = 2


def _lru_kernel(x_ref, g_ref, cw_ref, cb_ref, w_ref, b_ref, lam_ref, o_ref,
                xs_ref, a0_ref, u0_ref, a1_ref, u1_ref, *, seq, tc):
    pad = SUBLANES
    width = LRU_BLOCKS_PER_STEP * LANES
    zeros_pad = jnp.zeros((pad, width), F32)
    xs_ref[0:pad, :] = zeros_pad
    xs_ref[pad + seq:pad + seq + pad, :] = zeros_pad
    xs_ref[pad:pad + seq, :] = x_ref[0].astype(F32)

    k_all = (-LRU_C * 0.5 * LOG2_E) * _softplus(-lam_ref[...])
    cw_all = cw_ref[...]
    cb_all = cb_ref[...]

    n_seg = SUBLANES
    seg = seq // n_seg
    assert tc == seg
    for c in range(n_seg):
        base = pad + c * tc
        for j in range(LRU_BLOCKS_PER_STEP):
            cols = slice(j * LANES, (j + 1) * LANES)
            cw, cb = cw_all[:, cols], cb_all[:, cols]
            win = xs_ref[base - pad:base + tc + pad, cols]
            n_win = tc + 2 * pad
            taps = (pltpu.roll(win, 1, axis=0), win,
                    pltpu.roll(win, n_win - 1, axis=0), pltpu.roll(win, n_win - 2, axis=0))
            xr = cb + sum(cw[t:t + 1] * taps[t][pad:pad + tc] for t in range(4))
            th = jnp.tanh(jnp.dot(xr.astype(BF16), w_ref[j], preferred_element_type=F32)
                          + b_ref[j])
            for d, (a_ref, u_ref) in enumerate(((a0_ref, u0_ref), (a1_ref, u1_ref))):
                k = k_all[d:d + 1, cols]
                a = jnp.exp2(k * th[:, (2 * d) * LANES:(2 * d + 1) * LANES] + k)
                gate_i = 0.5 * th[:, (2 * d + 1) * LANES:(2 * d + 2) * LANES] + 0.5
                v = 1.0 - a * a
                mult = jnp.where(v > 0.0, v * lax.rsqrt(v), 0.0)
                a_ref[j, pl.ds(c, seg, stride=SUBLANES), :] = a
                u_ref[j, pl.ds(c, seg, stride=SUBLANES), :] = mult * (gate_i * xr)

    chains = [(a0_ref, u0_ref, j, False) for j in range(LRU_BLOCKS_PER_STEP)] \
        + [(a1_ref, u1_ref, j, True) for j in range(LRU_BLOCKS_PER_STEP)]

    def step(t, carry):
        out = []
        for (a_ref, u_ref, j, reverse), (h, p) in zip(chains, carry):
            r0 = pl.multiple_of((seg - 1 - t if reverse else t) * SUBLANES, SUBLANES)
            a = a_ref[j, pl.ds(r0, SUBLANES), :]
            h = a * h + u_ref[j, pl.ds(r0, SUBLANES), :]
            p = a * p
            u_ref[j, pl.ds(r0, SUBLANES), :] = h
            a_ref[j, pl.ds(r0, SUBLANES), :] = p
            out.append((h, p))
        return tuple(out)

    zero = jnp.zeros((SUBLANES, LANES), F32)
    one = jnp.ones((SUBLANES, LANES), F32)
    ends = lax.fori_loop(0, seg, step, ((zero, one),) * len(chains), unroll=8)

    row = lax.broadcasted_iota(jnp.int32, (SUBLANES, LANES), 0)
    entering = []
    for (_, _, _, reverse), (h_end, p_end) in zip(chains, ends):
        state = zero
        order = range(SUBLANES - 2, -1, -1) if reverse else range(1, SUBLANES)
        for s in order:
            nxt = pltpu.roll(h_end + p_end * state, SUBLANES - 1 if reverse else 1, axis=0)
            state = jnp.where(row == s, nxt, state)
        entering.append(state)

    for c in range(n_seg):
        rows = slice(c * tc, (c + 1) * tc)
        for j in range(LRU_BLOCKS_PER_STEP):
            cols = slice(j * LANES, (j + 1) * LANES)
            hs = 0.0
            for (a_ref, u_ref, jj, _), state in zip(chains, entering):
                if jj == j:
                    hs = hs + (u_ref[j, pl.ds(c, seg, stride=SUBLANES), :]
                               + a_ref[j, pl.ds(c, seg, stride=SUBLANES), :] * state[c:c + 1, :])
            y = hs * _gelu_tanh(g_ref[0, rows, cols].astype(F32))
            o_ref[0, rows, cols] = y.astype(BF16)


def _lru(proj3, conv_w, conv_b, w_cat, b_cat, lam):
    b, s, _ = proj3.shape
    tc = s // SUBLANES
    kernel = functools.partial(_lru_kernel, seq=s, tc=tc)
    nb = LRU_BLOCKS_PER_STEP
    width = nb * LANES
    return pl.pallas_call(
        kernel,
        out_shape=jax.ShapeDtypeStruct((b, s, LRU_WIDTH), BF16),
        grid=(b, LRU_BLOCKS // nb),
        in_specs=[
            pl.BlockSpec((1, s, width), lambda bi, ni: (bi, 0, LRUX_COL // nb + ni)),
            pl.BlockSpec((1, s, width), lambda bi, ni: (bi, 0, LRUG_COL // nb + ni)),
            pl.BlockSpec((4, width), lambda bi, ni: (0, ni)),
            pl.BlockSpec((1, width), lambda bi, ni: (0, ni)),
            pl.BlockSpec((nb, LRU_BLOCK_DIM, 4 * LRU_BLOCK_DIM), lambda bi, ni: (ni, 0, 0)),
            pl.BlockSpec((nb, 1, 4 * LRU_BLOCK_DIM), lambda bi, ni: (ni, 0, 0)),
            pl.BlockSpec((2, width), lambda bi, ni: (0, ni)),
        ],
        out_specs=pl.BlockSpec((1, s, width), lambda bi, ni: (bi, 0, ni)),
        scratch_shapes=[pltpu.VMEM((s + 2 * SUBLANES, width), F32)]
        + [pltpu.VMEM((nb, s, LANES), F32)] * 4,
        compiler_params=pltpu.CompilerParams(
            dimension_semantics=("arbitrary", "arbitrary"), vmem_limit_bytes=VMEM_LIMIT),
        name="rglru",
    )(proj3, proj3, conv_w, conv_b, w_cat, b_cat, lam)


def _merge_kernel(x_ref, ao_ref, yl_ref, ga_ref, gl_ref, bg_ref, wa_ref, wl_ref, wo_ref,
                  n2_ref, wr_ref, br_ref, x1_ref, h2_ref, cw_ref):
    attn_o = jnp.concatenate([ao_ref[0, h] for h in range(N_HEADS)], axis=1)
    attn_d = jnp.dot(attn_o, wa_ref[...], preferred_element_type=F32)
    lru_d = jnp.dot(yl_ref[...], wl_ref[...], preferred_element_type=F32)
    bg = bg_ref[...]
    g_attn = jax.nn.sigmoid(ga_ref[...].astype(F32) + bg[:, 0:D_MODEL])
    g_lru = jax.nn.sigmoid(gl_ref[...].astype(F32) + bg[:, D_MODEL:2 * D_MODEL])
    merged = g_attn * attn_d + g_lru * lru_d
    x1 = x_ref[...] + jnp.dot(merged.astype(BF16), wo_ref[...], preferred_element_type=F32)
    x1_ref[...] = x1
    h2 = _rms(x1, n2_ref[...])
    h2_ref[...] = h2.astype(BF16)

    h2_hi = h2.astype(BF16)
    h2_lo = (h2 - h2_hi.astype(F32)).astype(BF16)
    wr = wr_ref[...]
    part = jnp.dot(h2_hi, wr, preferred_element_type=F32)
    logits = (part[:, 0:ROUTER_LANES] + part[:, ROUTER_LANES:]
              + jnp.dot(h2_lo, wr[:, 0:ROUTER_LANES], preferred_element_type=F32) + br_ref[...])
    lane = lax.broadcasted_iota(jnp.int32, logits.shape, 1)
    neg = jnp.full_like(logits, -jnp.inf)
    big = jnp.full_like(lane, ROUTER_LANES)

    def masked_max(mask):
        return jnp.max(jnp.where(mask, logits, neg), axis=-1, keepdims=True)

    def first_lane(mask, value):
        return jnp.min(jnp.where(mask & (logits == value), lane, big), axis=-1, keepdims=True)

    g_mask = lane < N_GROUPS
    g_max = masked_max(g_mask)
    g_sel = first_lane(g_mask, g_max)
    g_w = 1.0 / jnp.sum(jnp.where(g_mask, jnp.exp(logits - g_max), 0.0), axis=-1, keepdims=True)
    e_lo = EXPERT_LANE0 + g_sel * EXPERTS_PER_GROUP
    e_mask = (lane >= e_lo) & (lane < e_lo + EXPERTS_PER_GROUP)
    v1 = masked_max(e_mask)
    i1 = first_lane(e_mask, v1)
    e_mask2 = e_mask & (lane != i1)
    v2 = masked_max(e_mask2)
    i2 = first_lane(e_mask2, v2)
    t = jnp.exp(v2 - v1)
    w1 = g_w / (1.0 + t)
    w2 = g_w * t / (1.0 + t)
    cw_ref[...] = (jnp.where(lane == i1, w1, 0.0) + jnp.where(lane == i2, w2, 0.0)
                   + jnp.where(lane == SORT_CLASS_LANE, g_sel.astype(F32), 0.0))


def _merge(x2, attn_o, y_lru, proj, b_gates, wa, wl, wo, n2_gain, w_router, b_router, tm=512):
    n = x2.shape[0]
    blocks_per_seq = attn_o.shape[2] // tm
    row = lambda cols, col_blk=0: pl.BlockSpec((tm, cols), lambda i: (i, col_blk))
    const = lambda shape: pl.BlockSpec(shape, lambda i: (0, 0))
    heads = pl.BlockSpec((1, N_HEADS, tm, V_HEAD),
                         lambda i: (i // blocks_per_seq, 0, i % blocks_per_seq, 0))
    return pl.pallas_call(
        _merge_kernel,
        out_shape=(jax.ShapeDtypeStruct((n, D_MODEL), F32),
                   jax.ShapeDtypeStruct((n, D_MODEL), BF16),
                   jax.ShapeDtypeStruct((n, ROUTER_LANES), F32)),
        grid=(n // tm,),
        in_specs=[
            row(D_MODEL), heads, row(LRU_WIDTH),
            row(D_MODEL, GATE_COL_1024), row(D_MODEL, GATE_COL_1024 + 1),
            const((1, 2 * D_MODEL)),
            const((ATTN_WIDTH, D_MODEL)), const((LRU_WIDTH, D_MODEL)), const((D_MODEL, D_MODEL)),
            const((1, D_MODEL)), const((D_MODEL, 2 * ROUTER_LANES)), const((1, ROUTER_LANES)),
        ],
        out_specs=(row(D_MODEL), row(D_MODEL), row(ROUTER_LANES)),
        compiler_params=pltpu.CompilerParams(
            dimension_semantics=("arbitrary",), vmem_limit_bytes=VMEM_LIMIT),
        name="merge_router",
    )(x2, attn_o, y_lru, proj, proj, b_gates, wa, wl, wo, n2_gain, w_router, b_router)


MOE_ALIGN = 16
MOE_ALIGN_SHIFT = 4
MOE_WINDOW_ROWS = (256, 320, 384, 512)
MOE_LARGEST_SHIFT = 9
MOE_ROW_PAD = MOE_WINDOW_ROWS[0]
MOE_TILES_PER_PASS = 1
MOE_RNG_STRIDE = 32
MOE_VMEM_LIMIT = 60 * 1024 * 1024


def _moe_kernel(h_ref, rt_ref, x1_ref, wg_ref, wu_ref, wd_ref, o_ref,
                xs_ref, ys_ref, pos_ref, rng_ref):
    g = pl.program_id(1)
    sub = pl.program_id(2)
    t_tile = h_ref.shape[0]
    rng0 = sub * MOE_RNG_STRIDE

    @pl.when(g == 0)
    def _sort_rows():
        rt = rt_ref[...]
        lane = lax.broadcasted_iota(jnp.int32, rt.shape, 1)
        cls = jnp.sum(jnp.where(lane == SORT_CLASS_LANE, rt, 0.0), axis=-1, keepdims=True)
        onehot = lane == cls.astype(jnp.int32)
        r = lax.broadcasted_iota(jnp.int32, (t_tile, t_tile), 0)
        c = lax.broadcasted_iota(jnp.int32, (t_tile, t_tile), 1)
        earlier = jnp.dot((r > c).astype(BF16), onehot.astype(BF16),
                          preferred_element_type=F32)
        cnt = jnp.sum(onehot.astype(F32), axis=0, keepdims=True)
        lane1 = lax.broadcasted_iota(jnp.int32, cnt.shape, 1)
        incl = cnt
        for d in (1, 2, 4, 8, 16):
            incl = incl + jnp.where(lane1 >= d, pltpu.roll(incl, d, axis=1), 0.0)
        off = incl - cnt
        pos = jnp.sum(jnp.where(onehot, earlier + off, 0.0), axis=-1, keepdims=True)
        pos_b = jnp.broadcast_to(pos, rt.shape)
        pos_ref[sub] = pos_b
        pos_row = pos_b.T[0:1, :]
        perm = (r.astype(F32) == pos_row).astype(BF16)
        rt_hi = rt.astype(BF16)
        rt_lo = (rt - rt_hi.astype(F32)).astype(BF16)
        cat = jnp.concatenate([h_ref[...], rt_hi, rt_lo], axis=1)
        xs_ref[sub, 0:t_tile, :] = jnp.dot(perm, cat, preferred_element_type=F32).astype(BF16)
        xs_ref[sub, t_tile:, :] = jnp.zeros((xs_ref.shape[1] - t_tile, xs_ref.shape[2]), BF16)
        ys_ref[sub] = jnp.zeros(ys_ref.shape[1:], F32)
        for k in range(N_SORT_CLASSES + 1):
            rng_ref[rng0 + k] = jnp.sum(jnp.where(lane1 == k, off, 0.0)).astype(jnp.int32)

    def window(r0, rows):
        r0 = pl.multiple_of(r0, MOE_ALIGN)
        xc = xs_ref[sub, pl.ds(r0, rows), 0:D_MODEL]
        rs = (xs_ref[sub, pl.ds(r0, rows), D_MODEL:D_MODEL + ROUTER_LANES].astype(F32)
              + xs_ref[sub, pl.ds(r0, rows), D_MODEL + ROUTER_LANES:].astype(F32))
        lane = lax.broadcasted_iota(jnp.int32, rs.shape, 1)
        y = jnp.zeros((rows, D_MODEL), F32)
        for e in range(EXPERTS_PER_GROUP):
            a = jnp.dot(xc, wg_ref[e], preferred_element_type=F32)
            u = jnp.dot(xc, wu_ref[e], preferred_element_type=F32)
            ce = jnp.sum(jnp.where(lane == EXPERT_LANE0 + g * EXPERTS_PER_GROUP + e, rs, 0.0),
                         axis=-1, keepdims=True)
            hid = a * jax.nn.sigmoid(a) * u * ce
            y = y + jnp.dot(hid.astype(BF16), wd_ref[e], preferred_element_type=F32)
        ys_ref[sub, pl.ds(r0, rows), :] += y

    largest = MOE_WINDOW_ROWS[-1]
    start = rng_ref[rng0 + g * CLASSES_PER_GROUP]
    stop = rng_ref[rng0 + (g + 1) * CLASSES_PER_GROUP]
    first = lax.shift_left(lax.shift_right_logical(start, MOE_ALIGN_SHIFT), MOE_ALIGN_SHIFT)
    span = jnp.where(stop > start, stop - first, 0)
    n_full = lax.shift_right_logical(span, MOE_LARGEST_SHIFT)

    def full_window(ci, carry):
        window(first + ci * largest, largest)
        return carry

    lax.fori_loop(0, n_full, full_window, 0)
    rem = span - lax.shift_left(n_full, MOE_LARGEST_SHIFT)
    rem_start = first + lax.shift_left(n_full, MOE_LARGEST_SHIFT)
    for lower, rows in zip((0,) + MOE_WINDOW_ROWS[:-1], MOE_WINDOW_ROWS):
        @pl.when((rem > lower) & (rem <= rows))
        def _remainder_window(rows=rows):
            window(rem_start, rows)

    @pl.when(g == N_GROUPS - 1)
    def _unsort_rows():
        pos_b = pos_ref[sub]
        c = lax.broadcasted_iota(jnp.int32, (t_tile, t_tile), 1).astype(F32)
        unperm = (jnp.concatenate([pos_b] * (t_tile // LANES), axis=1) == c).astype(BF16)
        o_ref[...] = x1_ref[...] + jnp.dot(unperm, ys_ref[sub, 0:t_tile, :].astype(BF16),
                                           preferred_element_type=F32)


def _moe(h2, route, x1, wg, wu, wd, tm=1024):
    n = h2.shape[0]
    last = N_GROUPS - 1
    group_w = lambda rows, cols: pl.BlockSpec((EXPERTS_PER_GROUP, rows, cols),
                                              lambda pr, g, sub: (g, 0, 0))
    sort_rows = lambda pr, g, sub: (MOE_TILES_PER_PASS * pr + jnp.where(g == 0, sub, MOE_TILES_PER_PASS - 1), 0)
    unsort_rows = lambda pr, g, sub: (MOE_TILES_PER_PASS * pr + jnp.where(g == last, sub, 0), 0)
    return pl.pallas_call(
        _moe_kernel,
        out_shape=jax.ShapeDtypeStruct((n, D_MODEL), F32),
        grid=(n // (MOE_TILES_PER_PASS * tm), N_GROUPS, MOE_TILES_PER_PASS),
        in_specs=[
            pl.BlockSpec((tm, D_MODEL), sort_rows),
            pl.BlockSpec((tm, ROUTER_LANES), sort_rows),
            pl.BlockSpec((tm, D_MODEL), unsort_rows),
            group_w(D_MODEL, EXPERT_HIDDEN), group_w(D_MODEL, EXPERT_HIDDEN),
            group_w(EXPERT_HIDDEN, D_MODEL),
        ],
        out_specs=pl.BlockSpec((tm, D_MODEL), unsort_rows),
        scratch_shapes=[
            pltpu.VMEM((MOE_TILES_PER_PASS, tm + MOE_ROW_PAD, D_MODEL + 2 * ROUTER_LANES), BF16),
            pltpu.VMEM((MOE_TILES_PER_PASS, tm + MOE_ROW_PAD, D_MODEL), F32),
            pltpu.VMEM((MOE_TILES_PER_PASS, tm, ROUTER_LANES), F32),
            pltpu.SMEM((MOE_TILES_PER_PASS * MOE_RNG_STRIDE,), jnp.int32),
        ],
        compiler_params=pltpu.CompilerParams(
            dimension_semantics=("arbitrary", "arbitrary", "arbitrary"),
            vmem_limit_bytes=MOE_VMEM_LIMIT),
        name="moe",
    )(h2, route, x1, wg, wu, wd)


def _rope_tables(seq):
    pos = jnp.arange(seq, dtype=F32)
    inv_freq = ROPE_THETA ** (-jnp.arange(0, SUB_HEAD, 2, dtype=F32) / SUB_HEAD)
    ang = pos[:, None] * inv_freq[None, :]
    cos, sin = jnp.cos(ang), jnp.sin(ang)
    cos_full = jnp.concatenate([cos, cos, cos, cos], axis=-1)
    sin_signed = jnp.concatenate([-sin, sin, -sin, sin], axis=-1)
    return cos_full, sin_signed


def kernel(x, norm1_gain, w_in, b_gates, q_norm_gain, k_norm_gain, lambda_q1, lambda_k1, lambda_q2, lambda_k2, attn_subln_gain, w_attn_o, conv_w, conv_b, lru_wa, lru_ba, lru_wi, lru_bi, lru_lambda, w_lru_o, w_out, norm2_gain, w_group_router, b_group_router, w_expert_router, b_expert_router, w_expert_gate, w_expert_up, w_expert_down):
    b, s, d = x.shape
    n = b * s
    depth = w_in.shape[0]
    assert depth == 1 and d == D_MODEL
    cos, sin_signed = _rope_tables(s)
    x2 = x.reshape(n, d)
    l = 0

    sub_heads = ATTN_WIDTH // SUB_HEAD
    q_gain = jnp.tile(q_norm_gain[l], sub_heads)[None, :]
    k_gain = jnp.tile(k_norm_gain[l], sub_heads)[None, :]
    q_hm, kt_hm, v_hm, rest = _inproj(
        x, norm1_gain[l][None, :], w_in[l].astype(BF16), q_gain, k_gain, cos, sin_signed)
    proj3 = rest.reshape(b, s, REST_WIDTH)

    lam_params = jnp.stack([lambda_q1[l], lambda_k1[l], lambda_q2[l], lambda_k2[l]])
    attn_o = _attention(q_hm, kt_hm, v_hm, lam_params, attn_subln_gain[l][None, :],
                        q_gain, k_gain)

    w_cat = (0.5 * jnp.concatenate([lru_wa[l, 0], lru_wi[l, 0], lru_wa[l, 1], lru_wi[l, 1]],
                                   axis=-1)).astype(BF16)
    blk = lambda v: v.reshape(LRU_BLOCKS, 1, LRU_BLOCK_DIM)
    b_cat = 0.5 * jnp.concatenate([blk(lru_ba[l, 0]), blk(lru_bi[l, 0]),
                                   blk(lru_ba[l, 1]), blk(lru_bi[l, 1])], axis=-1)
    y_lru = _lru(proj3, conv_w[l], conv_b[l][None, :], w_cat, b_cat, lru_lambda[l])

    pad = ROUTER_LANES - N_GROUPS - N_EXPERTS
    w_router = jnp.concatenate([w_group_router[l], w_expert_router[l],
                                jnp.zeros((d, pad), F32)], axis=-1)
    b_router = jnp.concatenate([b_group_router[l], b_expert_router[l],
                                jnp.zeros((pad,), F32)])[None, :]
    w_router_hi = w_router.astype(BF16)
    w_router = jnp.concatenate(
        [w_router_hi, (w_router - w_router_hi.astype(F32)).astype(BF16)], axis=-1)
    x1, h2, cw = _merge(
        x2, attn_o, y_lru.reshape(n, LRU_WIDTH), rest,
        b_gates[l][None, :], w_attn_o[l].astype(BF16), w_lru_o[l].astype(BF16),
        w_out[l].astype(BF16), norm2_gain[l][None, :], w_router, b_router)

    out = _moe(h2, cw, x1, w_expert_gate[l].astype(BF16), w_expert_up[l].astype(BF16),
               w_expert_down[l].astype(BF16))
    return out.reshape(b, s, d)
```

```python
import functools
import math

import jax
import jax.numpy as jnp
from jax import lax
from jax.experimental import pallas as pl
from jax.experimental.pallas import tpu as pltpu

F32 = jnp.float32
BF16 = jnp.bfloat16

D_MODEL = 1024
N_HEADS = 8
SUB_HEAD = 64
V_HEAD = 128
ATTN_WIDTH = 1024
LRU_WIDTH = 1024
LRU_BLOCKS = 8
LRU_BLOCK_DIM = 128
LRU_C = 8.0
PROJ_WIDTH = 7168
N_GROUPS = 4
EXPERTS_PER_GROUP = 4
N_EXPERTS = 16
EXPERT_HIDDEN = 512
ROPE_THETA = 10000.0
RMS_EPS = 1e-6
LAMBDA_INIT = 0.8 - 0.6 * math.exp(-0.3 * 0)

LOG2_E = math.log2(math.e)
LANES = 128
SUBLANES = 8
VMEM_LIMIT = 56 * 1024 * 1024

REST_WIDTH = PROJ_WIDTH - 3 * ATTN_WIDTH
LRUX_COL, LRUG_COL = 0, 8
GATE_COL_1024 = 2

ROUTER_LANES = 128
EXPERT_LANE0 = N_GROUPS
SORT_CLASS_LANE = N_GROUPS + N_EXPERTS
CLASSES_PER_GROUP = 1
N_SORT_CLASSES = N_GROUPS * CLASSES_PER_GROUP


def _rms(x, gain):
    ms = jnp.mean(x * x, axis=-1, keepdims=True)
    return x * lax.rsqrt(ms + RMS_EPS) * gain


MXU_TILE = 256


def _subhead_norm_rope(x, gain, cos, sin_signed):
    width = x.shape[1]
    r = lax.broadcasted_iota(jnp.int32, (MXU_TILE, MXU_TILE), 0) // SUB_HEAD
    c = lax.broadcasted_iota(jnp.int32, (MXU_TILE, MXU_TILE), 1) // SUB_HEAD
    group_mean = jnp.where(r == c, 1.0 / SUB_HEAD, 0.0).astype(BF16)
    xx = (x * x).astype(BF16)
    ms = jnp.concatenate(
        [jnp.dot(xx[:, t * MXU_TILE:(t + 1) * MXU_TILE], group_mean, preferred_element_type=F32)
         for t in range(width // MXU_TILE)], axis=1)
    xn = x * lax.rsqrt(ms + RMS_EPS) * gain
    lane = lax.broadcasted_iota(jnp.int32, xn.shape, 1)
    first_half = (lane % SUB_HEAD) < (SUB_HEAD // 2)
    partner = jnp.where(first_half,
                        pltpu.roll(xn, width - SUB_HEAD // 2, axis=1),
                        pltpu.roll(xn, SUB_HEAD // 2, axis=1))
    reps = width // LANES
    return (xn * jnp.concatenate([cos] * reps, axis=1)
            + partner * jnp.concatenate([sin_signed] * reps, axis=1))


def _inproj_kernel(x_ref, g_ref, w_ref, qg_ref, kg_ref, cos_ref, sin_ref,
                   q_ref, kt_ref, v_ref, rest_ref):
    h = _rms(x_ref[...], g_ref[...]).astype(BF16)
    cos, sin_signed = cos_ref[...], sin_ref[...]

    def proj(j):
        return jnp.dot(h, w_ref[:, j * 1024:(j + 1) * 1024], preferred_element_type=F32)

    q = _subhead_norm_rope(proj(0), qg_ref[...], cos, sin_signed) * (SUB_HEAD ** -0.5 * LOG2_E)
    k = _subhead_norm_rope(proj(1), kg_ref[...], cos, sin_signed)
    v = proj(2).astype(BF16)
    for hd in range(N_HEADS):
        cols = slice(hd * LANES, (hd + 1) * LANES)
        q_ref[0, hd] = q[:, cols].astype(BF16)
        kt_ref[0, hd] = k[:, cols].T.astype(BF16)
        v_ref[0, hd] = v[:, cols]
    for j in range(3, PROJ_WIDTH // 1024):
        rest_ref[:, (j - 3) * 1024:(j - 2) * 1024] = proj(j).astype(BF16)


def _inproj(x3, gain, w_bf16, q_gain, k_gain, cos, sin_signed, tm=512):
    b, s, d = x3.shape
    n = b * s
    bps = s // tm
    const = lambda shape: pl.BlockSpec(shape, lambda i: (0, 0))
    table = pl.BlockSpec((tm, LANES), lambda i: (i % bps, 0))
    heads = pl.BlockSpec((1, N_HEADS, tm, LANES), lambda i: (i // bps, 0, i % bps, 0))
    return pl.pallas_call(
        _inproj_kernel,
        out_shape=(jax.ShapeDtypeStruct((b, N_HEADS, s, LANES), BF16),
                   jax.ShapeDtypeStruct((b, N_HEADS, LANES, s), BF16),
                   jax.ShapeDtypeStruct((b, N_HEADS, s, V_HEAD), BF16),
                   jax.ShapeDtypeStruct((n, REST_WIDTH), BF16)),
        grid=(n // tm,),
        in_specs=[
            pl.BlockSpec((tm, D_MODEL), lambda i: (i, 0)),
            const((1, D_MODEL)),
            const((D_MODEL, PROJ_WIDTH)),
            const((1, ATTN_WIDTH)), const((1, ATTN_WIDTH)), table, table,
        ],
        out_specs=(heads,
                   pl.BlockSpec((1, N_HEADS, LANES, tm), lambda i: (i // bps, 0, 0, i % bps)),
                   heads,
                   pl.BlockSpec((tm, REST_WIDTH), lambda i: (i, 0))),
        compiler_params=pltpu.CompilerParams(
            dimension_semantics=("arbitrary",), vmem_limit_bytes=VMEM_LIMIT),
        name="inproj",
    )(x3.reshape(n, d), gain, w_bf16, q_gain, k_gain, cos, sin_signed)


ATTN_SUB_ROWS = 128


SCORE_BOUND_LOG2 = 100.0


def _attn_kernel(lam_ref, sg_ref, qg_ref, kg_ref, q_ref, kt_ref, v_ref, o_ref, *, tq):
    lp = lam_ref[...]
    lam = (jnp.exp(jnp.sum(lp[0:1] * lp[1:2], axis=-1, keepdims=True))
           - jnp.exp(jnp.sum(lp[2:3] * lp[3:4], axis=-1, keepdims=True)) + LAMBDA_INIT)

    score_bound = (SUB_HEAD * (SUB_HEAD ** -0.5 * LOG2_E) * 1.01
                   * jnp.max(jnp.abs(qg_ref[...])) * jnp.max(jnp.abs(kg_ref[...])))
    ts = ATTN_SUB_ROWS

    def chains(subtract_max):
        for t in range(tq // ts):
            q = q_ref[0, 0, t * ts:(t + 1) * ts, :]
            lane = lax.broadcasted_iota(jnp.int32, q.shape, 1)
            zero = jnp.zeros_like(q)
            qq = jnp.concatenate([jnp.where(lane < SUB_HEAD, q, zero),
                                  jnp.where(lane >= SUB_HEAD, q, zero)], axis=0)
            s = jnp.dot(qq, kt_ref[0, 0], preferred_element_type=F32)
            if subtract_max:
                s = s - jnp.max(s, axis=-1, keepdims=True)
            p = jnp.exp2(s)
            l = jnp.sum(p, axis=-1, keepdims=True)
            acc = jnp.dot(p.astype(BF16), v_ref[0, 0], preferred_element_type=F32)
            o = acc / l
            o = o[0:ts] - lam * o[ts:2 * ts]
            o = _rms(o, sg_ref[...]) * (1.0 - LAMBDA_INIT)
            o_ref[0, 0, t * ts:(t + 1) * ts, :] = o.astype(BF16)

    @pl.when(score_bound <= SCORE_BOUND_LOG2)
    def _bounded_scores():
        chains(subtract_max=False)

    @pl.when(jnp.logical_not(score_bound <= SCORE_BOUND_LOG2))
    def _any_scores():
        chains(subtract_max=True)


def _attention(q_hm, kt_hm, v_hm, lam_params, subln_gain, q_gain, k_gain, tq=1024):
    b, h, s, _ = q_hm.shape
    const = lambda shape: pl.BlockSpec(shape, lambda bi, hi, qi: (0, 0))
    return pl.pallas_call(
        functools.partial(_attn_kernel, tq=tq),
        out_shape=jax.ShapeDtypeStruct((b, h, s, V_HEAD), BF16),
        grid=(b, h, s // tq),
        in_specs=[
            const((4, SUB_HEAD)), const((1, V_HEAD)), const((1, ATTN_WIDTH)), const((1, ATTN_WIDTH)),
            pl.BlockSpec((1, 1, tq, LANES), lambda bi, hi, qi: (bi, hi, qi, 0)),
            pl.BlockSpec((1, 1, LANES, s), lambda bi, hi, qi: (bi, hi, 0, 0)),
            pl.BlockSpec((1, 1, s, V_HEAD), lambda bi, hi, qi: (bi, hi, 0, 0)),
        ],
        out_specs=pl.BlockSpec((1, 1, tq, V_HEAD), lambda bi, hi, qi: (bi, hi, qi, 0)),
        compiler_params=pltpu.CompilerParams(
            dimension_semantics=("arbitrary", "arbitrary", "arbitrary"),
            vmem_limit_bytes=VMEM_LIMIT),
        name="diff_attn",
    )(lam_params, subln_gain, q_gain, k_gain, q_hm, kt_hm, v_hm)


def _softplus(x):
    return jnp.maximum(x, 0.0) + jnp.log1p(jnp.exp(-jnp.abs(x)))


def _gelu_tanh(x):
    return 0.5 * x * (1.0 + jnp.tanh(math.sqrt(2.0 / math.pi) * (x + 0.044715 * (x * x * x))))


LRU_BLOCKS_PER_STEP = 2


def _lru_kernel(x_ref, g_ref, cw_ref, cb_ref, w_ref, b_ref, lam_ref, o_ref,
                xs_ref, a0_ref, u0_ref, a1_ref, u1_ref, *, seq, tc):
    pad = SUBLANES
    width = LRU_BLOCKS_PER_STEP * LANES
    zeros_pad = jnp.zeros((pad, width), F32)
    xs_ref[0:pad, :] = zeros_pad
    xs_ref[pad + seq:pad + seq + pad, :] = zeros_pad
    xs_ref[pad:pad + seq, :] = x_ref[0].astype(F32)

    k_all = (-LRU_C * 0.5 * LOG2_E) * _softplus(-lam_ref[...])
    cw_all = cw_ref[...]
    cb_all = cb_ref[...]

    n_seg = SUBLANES
    seg = seq // n_seg
    assert tc == seg
    for c in range(n_seg):
        base = pad + c * tc
        for j in range(LRU_BLOCKS_PER_STEP):
            cols = slice(j * LANES, (j + 1) * LANES)
            cw, cb = cw_all[:, cols], cb_all[:, cols]
            win = xs_ref[base - pad:base + tc + pad, cols]
            n_win = tc + 2 * pad
            taps = (pltpu.roll(win, 1, axis=0), win,
                    pltpu.roll(win, n_win - 1, axis=0), pltpu.roll(win, n_win - 2, axis=0))
            xr = cb + sum(cw[t:t + 1] * taps[t][pad:pad + tc] for t in range(4))
            th = jnp.tanh(jnp.dot(xr.astype(BF16), w_ref[j], preferred_element_type=F32)
                          + b_ref[j])
            for d, (a_ref, u_ref) in enumerate(((a0_ref, u0_ref), (a1_ref, u1_ref))):
                k = k_all[d:d + 1, cols]
                a = jnp.exp2(k * th[:, (2 * d) * LANES:(2 * d + 1) * LANES] + k)
                gate_i = 0.5 * th[:, (2 * d + 1) * LANES:(2 * d + 2) * LANES] + 0.5
                v = 1.0 - a * a
                mult = jnp.where(v > 0.0, v * lax.rsqrt(v), 0.0)
                a_ref[j, pl.ds(c, seg, stride=SUBLANES), :] = a
                u_ref[j, pl.ds(c, seg, stride=SUBLANES), :] = mult * (gate_i * xr)

    chains = [(a0_ref, u0_ref, j, False) for j in range(LRU_BLOCKS_PER_STEP)] \
        + [(a1_ref, u1_ref, j, True) for j in range(LRU_BLOCKS_PER_STEP)]

    def step(t, carry):
        out = []
        for (a_ref, u_ref, j, reverse), (h, p) in zip(chains, carry):
            r0 = pl.multiple_of((seg - 1 - t if reverse else t) * SUBLANES, SUBLANES)
            a = a_ref[j, pl.ds(r0, SUBLANES), :]
            h = a * h + u_ref[j, pl.ds(r0, SUBLANES), :]
            p = a * p
            u_ref[j, pl.ds(r0, SUBLANES), :] = h
            a_ref[j, pl.ds(r0, SUBLANES), :] = p
            out.append((h, p))
        return tuple(out)

    zero = jnp.zeros((SUBLANES, LANES), F32)
    one = jnp.ones((SUBLANES, LANES), F32)
    ends = lax.fori_loop(0, seg, step, ((zero, one),) * len(chains), unroll=8)

    row = lax.broadcasted_iota(jnp.int32, (SUBLANES, LANES), 0)
    entering = []
    for (_, _, _, reverse), (h_end, p_end) in zip(chains, ends):
        state = zero
        order = range(SUBLANES - 2, -1, -1) if reverse else range(1, SUBLANES)
        for s in order:
            nxt = pltpu.roll(h_end + p_end * state, SUBLANES - 1 if reverse else 1, axis=0)
            state = jnp.where(row == s, nxt, state)
        entering.append(state)

    for c in range(n_seg):
        rows = slice(c * tc, (c + 1) * tc)
        for j in range(LRU_BLOCKS_PER_STEP):
            cols = slice(j * LANES, (j + 1) * LANES)
            hs = 0.0
            for (a_ref, u_ref, jj, _), state in zip(chains, entering):
                if jj == j:
                    hs = hs + (u_ref[j, pl.ds(c, seg, stride=SUBLANES), :]
                               + a_ref[j, pl.ds(c, seg, stride=SUBLANES), :] * state[c:c + 1, :])
            y = hs * _gelu_tanh(g_ref[0, rows, cols].astype(F32))
            o_ref[0, rows, cols] = y.astype(BF16)


def _lru(proj3, conv_w, conv_b, w_cat, b_cat, lam):
    b, s, _ = proj3.shape
    tc = s // SUBLANES
    kernel = functools.partial(_lru_kernel, seq=s, tc=tc)
    nb = LRU_BLOCKS_PER_STEP
    width = nb * LANES
    return pl.pallas_call(
        kernel,
        out_shape=jax.ShapeDtypeStruct((b, s, LRU_WIDTH), BF16),
        grid=(b, LRU_BLOCKS // nb),
        in_specs=[
            pl.BlockSpec((1, s, width), lambda bi, ni: (bi, 0, LRUX_COL // nb + ni)),
            pl.BlockSpec((1, s, width), lambda bi, ni: (bi, 0, LRUG_COL // nb + ni)),
            pl.BlockSpec((4, width), lambda bi, ni: (0, ni)),
            pl.BlockSpec((1, width), lambda bi, ni: (0, ni)),
            pl.BlockSpec((nb, LRU_BLOCK_DIM, 4 * LRU_BLOCK_DIM), lambda bi, ni: (ni, 0, 0)),
            pl.BlockSpec((nb, 1, 4 * LRU_BLOCK_DIM), lambda bi, ni: (ni, 0, 0)),
            pl.BlockSpec((2, width), lambda bi, ni: (0, ni)),
        ],
        out_specs=pl.BlockSpec((1, s, width), lambda bi, ni: (bi, 0, ni)),
        scratch_shapes=[pltpu.VMEM((s + 2 * SUBLANES, width), F32)]
        + [pltpu.VMEM((nb, s, LANES), F32)] * 4,
        compiler_params=pltpu.CompilerParams(
            dimension_semantics=("arbitrary", "arbitrary"), vmem_limit_bytes=VMEM_LIMIT),
        name="rglru",
    )(proj3, proj3, conv_w, conv_b, w_cat, b_cat, lam)


def _merge_kernel(x_ref, ao_ref, yl_ref, ga_ref, gl_ref, bg_ref, wa_ref, wl_ref, wo_ref,
                  n2_ref, wr_ref, br_ref, x1_ref, h2_ref, cw_ref):
    attn_o = jnp.concatenate([ao_ref[0, h] for h in range(N_HEADS)], axis=1)
    attn_d = jnp.dot(attn_o, wa_ref[...], preferred_element_type=F32)
    lru_d = jnp.dot(yl_ref[...], wl_ref[...], preferred_element_type=F32)
    bg = bg_ref[...]
    g_attn = jax.nn.sigmoid(ga_ref[...].astype(F32) + bg[:, 0:D_MODEL])
    g_lru = jax.nn.sigmoid(gl_ref[...].astype(F32) + bg[:, D_MODEL:2 * D_MODEL])
    merged = g_attn * attn_d + g_lru * lru_d
    x1 = x_ref[...] + jnp.dot(merged.astype(BF16), wo_ref[...], preferred_element_type=F32)
    x1_ref[...] = x1
    h2 = _rms(x1, n2_ref[...])
    h2_ref[...] = h2.astype(BF16)

    h2_hi = h2.astype(BF16)
    h2_lo = (h2 - h2_hi.astype(F32)).astype(BF16)
    wr = wr_ref[...]
    part = jnp.dot(h2_hi, wr, preferred_element_type=F32)
    logits = (part[:, 0:ROUTER_LANES] + part[:, ROUTER_LANES:]
              + jnp.dot(h2_lo, wr[:, 0:ROUTER_LANES], preferred_element_type=F32) + br_ref[...])
    lane = lax.broadcasted_iota(jnp.int32, logits.shape, 1)
    neg = jnp.full_like(logits, -jnp.inf)
    big = jnp.full_like(lane, ROUTER_LANES)

    def masked_max(mask):
        return jnp.max(jnp.where(mask, logits, neg), axis=-1, keepdims=True)

    def first_lane(mask, value):
        return jnp.min(jnp.where(mask & (logits == value), lane, big), axis=-1, keepdims=True)

    g_mask = lane < N_GROUPS
    g_max = masked_max(g_mask)
    g_sel = first_lane(g_mask, g_max)
    g_w = 1.0 / jnp.sum(jnp.where(g_mask, jnp.exp(logits - g_max), 0.0), axis=-1, keepdims=True)
    e_lo = EXPERT_LANE0 + g_sel * EXPERTS_PER_GROUP
    e_mask = (lane >= e_lo) & (lane < e_lo + EXPERTS_PER_GROUP)
    v1 = masked_max(e_mask)
    i1 = first_lane(e_mask, v1)
    e_mask2 = e_mask & (lane != i1)
    v2 = masked_max(e_mask2)
    i2 = first_lane(e_mask2, v2)
    t = jnp.exp(v2 - v1)
    w1 = g_w / (1.0 + t)
    w2 = g_w * t / (1.0 + t)
    cw_ref[...] = (jnp.where(lane == i1, w1, 0.0) + jnp.where(lane == i2, w2, 0.0)
                   + jnp.where(lane == SORT_CLASS_LANE, g_sel.astype(F32), 0.0))


def _merge(x2, attn_o, y_lru, proj, b_gates, wa, wl, wo, n2_gain, w_router, b_router, tm=512):
    n = x2.shape[0]
    blocks_per_seq = attn_o.shape[2] // tm
    row = lambda cols, col_blk=0: pl.BlockSpec((tm, cols), lambda i: (i, col_blk))
    const = lambda shape: pl.BlockSpec(shape, lambda i: (0, 0))
    heads = pl.BlockSpec((1, N_HEADS, tm, V_HEAD),
                         lambda i: (i // blocks_per_seq, 0, i % blocks_per_seq, 0))
    return pl.pallas_call(
        _merge_kernel,
        out_shape=(jax.ShapeDtypeStruct((n, D_MODEL), F32),
                   jax.ShapeDtypeStruct((n, D_MODEL), BF16),
                   jax.ShapeDtypeStruct((n, ROUTER_LANES), F32)),
        grid=(n // tm,),
        in_specs=[
            row(D_MODEL), heads, row(LRU_WIDTH),
            row(D_MODEL, GATE_COL_1024), row(D_MODEL, GATE_COL_1024 + 1),
            const((1, 2 * D_MODEL)),
            const((ATTN_WIDTH, D_MODEL)), const((LRU_WIDTH, D_MODEL)), const((D_MODEL, D_MODEL)),
            const((1, D_MODEL)), const((D_MODEL, 2 * ROUTER_LANES)), const((1, ROUTER_LANES)),
        ],
        out_specs=(row(D_MODEL), row(D_MODEL), row(ROUTER_LANES)),
        compiler_params=pltpu.CompilerParams(
            dimension_semantics=("arbitrary",), vmem_limit_bytes=VMEM_LIMIT),
        name="merge_router",
    )(x2, attn_o, y_lru, proj, proj, b_gates, wa, wl, wo, n2_gain, w_router, b_router)


MOE_ALIGN = 16
MOE_ALIGN_SHIFT = 4
MOE_WINDOW_ROWS = (256, 288, 320, 352, 384, 512)
MOE_LARGEST_SHIFT = 9
MOE_ROW_PAD = MOE_WINDOW_ROWS[0]
MOE_VMEM_LIMIT = 60 * 1024 * 1024


def _moe_kernel(h_ref, rt_ref, x1_ref, wg_ref, wu_ref, wd_ref, o_ref,
                xs_ref, ys_ref, pos_ref, rng_ref):
    g = pl.program_id(1)
    t_tile = h_ref.shape[0]

    @pl.when(g == 0)
    def _sort_rows():
        rt = rt_ref[...]
        lane = lax.broadcasted_iota(jnp.int32, rt.shape, 1)
        cls = jnp.sum(jnp.where(lane == SORT_CLASS_LANE, rt, 0.0), axis=-1, keepdims=True)
        onehot = lane == cls.astype(jnp.int32)
        r = lax.broadcasted_iota(jnp.int32, (t_tile, t_tile), 0)
        c = lax.broadcasted_iota(jnp.int32, (t_tile, t_tile), 1)
        earlier = jnp.dot((r > c).astype(BF16), onehot.astype(BF16),
                          preferred_element_type=F32)
        cnt = jnp.sum(onehot.astype(F32), axis=0, keepdims=True)
        lane1 = lax.broadcasted_iota(jnp.int32, cnt.shape, 1)
        incl = cnt
        for d in (1, 2, 4, 8, 16):
            incl = incl + jnp.where(lane1 >= d, pltpu.roll(incl, d, axis=1), 0.0)
        off = incl - cnt
        pos = jnp.sum(jnp.where(onehot, earlier + off, 0.0), axis=-1, keepdims=True)
        pos_b = jnp.broadcast_to(pos, rt.shape)
        pos_ref[...] = pos_b
        pos_row = pos_b.T[0:1, :]
        perm = (r.astype(F32) == pos_row).astype(BF16)
        rt_hi = rt.astype(BF16)
        rt_lo = (rt - rt_hi.astype(F32)).astype(BF16)
        cat = jnp.concatenate([h_ref[...], rt_hi, rt_lo], axis=1)
        xs_ref[0:t_tile, :] = jnp.dot(perm, cat, preferred_element_type=F32).astype(BF16)
        xs_ref[t_tile:, :] = jnp.zeros((xs_ref.shape[0] - t_tile, xs_ref.shape[1]), BF16)
        ys_ref[...] = jnp.zeros(ys_ref.shape, F32)
        for k in range(N_SORT_CLASSES + 1):
            rng_ref[k] = jnp.sum(jnp.where(lane1 == k, off, 0.0)).astype(jnp.int32)

    def window(r0, rows):
        r0 = pl.multiple_of(r0, MOE_ALIGN)
        xc = xs_ref[pl.ds(r0, rows), 0:D_MODEL]
        rs = (xs_ref[pl.ds(r0, rows), D_MODEL:D_MODEL + ROUTER_LANES].astype(F32)
              + xs_ref[pl.ds(r0, rows), D_MODEL + ROUTER_LANES:].astype(F32))
        lane = lax.broadcasted_iota(jnp.int32, rs.shape, 1)
        y = jnp.zeros((rows, D_MODEL), F32)
        for e in range(EXPERTS_PER_GROUP):
            a = jnp.dot(xc, wg_ref[e], preferred_element_type=F32)
            u = jnp.dot(xc, wu_ref[e], preferred_element_type=F32)
            ce = jnp.sum(jnp.where(lane == EXPERT_LANE0 + g * EXPERTS_PER_GROUP + e, rs, 0.0),
                         axis=-1, keepdims=True)
            hid = a * jax.nn.sigmoid(a) * u * ce
            y = y + jnp.dot(hid.astype(BF16), wd_ref[e], preferred_element_type=F32)
        ys_ref[pl.ds(r0, rows), :] += y

    largest = MOE_WINDOW_ROWS[-1]
    start = rng_ref[g * CLASSES_PER_GROUP]
    stop = rng_ref[(g + 1) * CLASSES_PER_GROUP]
    first = lax.shift_left(lax.shift_right_logical(start, MOE_ALIGN_SHIFT), MOE_ALIGN_SHIFT)
    span = jnp.where(stop > start, stop - first, 0)
    n_full = lax.shift_right_logical(span, MOE_LARGEST_SHIFT)

    def full_window(ci, carry):
        window(first + ci * largest, largest)
        return carry

    lax.fori_loop(0, n_full, full_window, 0)
    rem = span - lax.shift_left(n_full, MOE_LARGEST_SHIFT)
    rem_start = first + lax.shift_left(n_full, MOE_LARGEST_SHIFT)
    for lower, rows in zip((0,) + MOE_WINDOW_ROWS[:-1], MOE_WINDOW_ROWS):
        @pl.when((rem > lower) & (rem <= rows))
        def _remainder_window(rows=rows):
            window(rem_start, rows)

    @pl.when(g == N_GROUPS - 1)
    def _unsort_rows():
        pos_b = pos_ref[...]
        c = lax.broadcasted_iota(jnp.int32, (t_tile, t_tile), 1).astype(F32)
        unperm = (jnp.concatenate([pos_b] * (t_tile // LANES), axis=1) == c).astype(BF16)
        o_ref[...] = x1_ref[...] + jnp.dot(unperm, ys_ref[0:t_tile, :].astype(BF16),
                                           preferred_element_type=F32)


def _moe(h2, route, x1, wg, wu, wd, tm=1024):
    n = h2.shape[0]
    rows = lambda i, g: (i, 0)
    group_w = lambda r, c: pl.BlockSpec((EXPERTS_PER_GROUP, r, c), lambda i, g: (g, 0, 0))
    return pl.pallas_call(
        _moe_kernel,
        out_shape=jax.ShapeDtypeStruct((n, D_MODEL), F32),
        grid=(n // tm, N_GROUPS),
        in_specs=[
            pl.BlockSpec((tm, D_MODEL), rows),
            pl.BlockSpec((tm, ROUTER_LANES), rows),
            pl.BlockSpec((tm, D_MODEL), rows),
            group_w(D_MODEL, EXPERT_HIDDEN), group_w(D_MODEL, EXPERT_HIDDEN),
            group_w(EXPERT_HIDDEN, D_MODEL),
        ],
        out_specs=pl.BlockSpec((tm, D_MODEL), rows),
        scratch_shapes=[
            pltpu.VMEM((tm + MOE_ROW_PAD, D_MODEL + 2 * ROUTER_LANES), BF16),
            pltpu.VMEM((tm + MOE_ROW_PAD, D_MODEL), F32),
            pltpu.VMEM((tm, ROUTER_LANES), F32),
            pltpu.SMEM((N_SORT_CLASSES + 1,), jnp.int32),
        ],
        compiler_params=pltpu.CompilerParams(
            dimension_semantics=("arbitrary", "arbitrary"), vmem_limit_bytes=MOE_VMEM_LIMIT),
        name="moe",
    )(h2, route, x1, wg, wu, wd)


def _rope_tables(seq):
    pos = jnp.arange(seq, dtype=F32)
    inv_freq = ROPE_THETA ** (-jnp.arange(0, SUB_HEAD, 2, dtype=F32) / SUB_HEAD)
    ang = pos[:, None] * inv_freq[None, :]
    cos, sin = jnp.cos(ang), jnp.sin(ang)
    cos_full = jnp.concatenate([cos, cos, cos, cos], axis=-1)
    sin_signed = jnp.concatenate([-sin, sin, -sin, sin], axis=-1)
    return cos_full, sin_signed


def kernel(x, norm1_gain, w_in, b_gates, q_norm_gain, k_norm_gain, lambda_q1, lambda_k1, lambda_q2, lambda_k2, attn_subln_gain, w_attn_o, conv_w, conv_b, lru_wa, lru_ba, lru_wi, lru_bi, lru_lambda, w_lru_o, w_out, norm2_gain, w_group_router, b_group_router, w_expert_router, b_expert_router, w_expert_gate, w_expert_up, w_expert_down):
    b, s, d = x.shape
    n = b * s
    depth = w_in.shape[0]
    assert depth == 1 and d == D_MODEL
    cos, sin_signed = _rope_tables(s)
    x2 = x.reshape(n, d)
    l = 0

    sub_heads = ATTN_WIDTH // SUB_HEAD
    q_gain = jnp.tile(q_norm_gain[l], sub_heads)[None, :]
    k_gain = jnp.tile(k_norm_gain[l], sub_heads)[None, :]
    q_hm, kt_hm, v_hm, rest = _inproj(
        x, norm1_gain[l][None, :], w_in[l].astype(BF16), q_gain, k_gain, cos, sin_signed)
    proj3 = rest.reshape(b, s, REST_WIDTH)

    lam_params = jnp.stack([lambda_q1[l], lambda_k1[l], lambda_q2[l], lambda_k2[l]])
    attn_o = _attention(q_hm, kt_hm, v_hm, lam_params, attn_subln_gain[l][None, :],
                        q_gain, k_gain)

    w_cat = (0.5 * jnp.concatenate([lru_wa[l, 0], lru_wi[l, 0], lru_wa[l, 1], lru_wi[l, 1]],
                                   axis=-1)).astype(BF16)
    blk = lambda v: v.reshape(LRU_BLOCKS, 1, LRU_BLOCK_DIM)
    b_cat = 0.5 * jnp.concatenate([blk(lru_ba[l, 0]), blk(lru_bi[l, 0]),
                                   blk(lru_ba[l, 1]), blk(lru_bi[l, 1])], axis=-1)
    y_lru = _lru(proj3, conv_w[l], conv_b[l][None, :], w_cat, b_cat, lru_lambda[l])

    pad = ROUTER_LANES - N_GROUPS - N_EXPERTS
    w_router = jnp.concatenate([w_group_router[l], w_expert_router[l],
                                jnp.zeros((d, pad), F32)], axis=-1)
    b_router = jnp.concatenate([b_group_router[l], b_expert_router[l],
                                jnp.zeros((pad,), F32)])[None, :]
    w_router_hi = w_router.astype(BF16)
    w_router = jnp.concatenate(
        [w_router_hi, (w_router - w_router_hi.astype(F32)).astype(BF16)], axis=-1)
    x1, h2, cw = _merge(
        x2, attn_o, y_lru.reshape(n, LRU_WIDTH), rest,
        b_gates[l][None, :], w_attn_o[l].astype(BF16), w_lru_o[l].astype(BF16),
        w_out[l].astype(BF16), norm2_gain[l][None, :], w_router, b_router)

    out = _moe(h2, cw, x1, w_expert_gate[l].astype(BF16), w_expert_up[l].astype(BF16),
               w_expert_down[l].astype(BF16))
    return out.reshape(b, s, d)
```

```python
import functools
import math

import jax
import jax.numpy as jnp
from jax import lax
from jax.experimental import pallas as pl
from jax.experimental.pallas import tpu as pltpu

F32 = jnp.float32
BF16 = jnp.bfloat16

D_MODEL = 1024
N_HEADS = 8
SUB_HEAD = 64
V_HEAD = 128
ATTN_WIDTH = 1024
LRU_WIDTH = 1024
LRU_BLOCKS = 8
LRU_BLOCK_DIM = 128
LRU_C = 8.0
PROJ_WIDTH = 7168
N_GROUPS = 4
EXPERTS_PER_GROUP = 4
N_EXPERTS = 16
EXPERT_HIDDEN = 512
ROPE_THETA = 10000.0
RMS_EPS = 1e-6
LAMBDA_INIT = 0.8 - 0.6 * math.exp(-0.3 * 0)

LOG2_E = math.log2(math.e)
LANES = 128
SUBLANES = 8
VMEM_LIMIT = 56 * 1024 * 1024

REST_WIDTH = PROJ_WIDTH - 3 * ATTN_WIDTH
LRUX_COL, LRUG_COL = 0, 8
GATE_COL_1024 = 2

ROUTER_LANES = 128
EXPERT_LANE0 = N_GROUPS
SORT_CLASS_LANE = N_GROUPS + N_EXPERTS
CLASSES_PER_GROUP = 1
N_SORT_CLASSES = N_GROUPS * CLASSES_PER_GROUP


def _rms(x, gain):
    ms = jnp.mean(x * x, axis=-1, keepdims=True)
    return x * lax.rsqrt(ms + RMS_EPS) * gain


MXU_TILE = 256


def _subhead_norm_rope(x, gain, cos, sin_signed):
    width = x.shape[1]
    r = lax.broadcasted_iota(jnp.int32, (MXU_TILE, MXU_TILE), 0) // SUB_HEAD
    c = lax.broadcasted_iota(jnp.int32, (MXU_TILE, MXU_TILE), 1) // SUB_HEAD
    group_mean = jnp.where(r == c, 1.0 / SUB_HEAD, 0.0).astype(BF16)
    xx = (x * x).astype(BF16)
    ms = jnp.concatenate(
        [jnp.dot(xx[:, t * MXU_TILE:(t + 1) * MXU_TILE], group_mean, preferred_element_type=F32)
         for t in range(width // MXU_TILE)], axis=1)
    xn = x * lax.rsqrt(ms + RMS_EPS) * gain
    lane = lax.broadcasted_iota(jnp.int32, xn.shape, 1)
    first_half = (lane % SUB_HEAD) < (SUB_HEAD // 2)
    partner = jnp.where(first_half,
                        pltpu.roll(xn, width - SUB_HEAD // 2, axis=1),
                        pltpu.roll(xn, SUB_HEAD // 2, axis=1))
    reps = width // LANES
    return (xn * jnp.concatenate([cos] * reps, axis=1)
            + partner * jnp.concatenate([sin_signed] * reps, axis=1))


def _inproj_kernel(x_ref, g_ref, w_ref, qg_ref, kg_ref, cos_ref, sin_ref,
                   q_ref, kt_ref, v_ref, rest_ref):
    h = _rms(x_ref[...], g_ref[...]).astype(BF16)
    cos, sin_signed = cos_ref[...], sin_ref[...]

    def proj(j):
        return jnp.dot(h, w_ref[:, j * 1024:(j + 1) * 1024], preferred_element_type=F32)

    q = _subhead_norm_rope(proj(0), qg_ref[...], cos, sin_signed) * (SUB_HEAD ** -0.5 * LOG2_E)
    k = _subhead_norm_rope(proj(1), kg_ref[...], cos, sin_signed)
    v = proj(2).astype(BF16)
    for hd in range(N_HEADS):
        cols = slice(hd * LANES, (hd + 1) * LANES)
        q_ref[0, hd] = q[:, cols].astype(BF16)
        kt_ref[0, hd] = k[:, cols].T.astype(BF16)
        v_ref[0, hd] = v[:, cols]
    for j in range(3, PROJ_WIDTH // 1024):
        rest_ref[:, (j - 3) * 1024:(j - 2) * 1024] = proj(j).astype(BF16)


def _inproj(x3, gain, w_bf16, q_gain, k_gain, cos, sin_signed, tm=512):
    b, s, d = x3.shape
    n = b * s
    bps = s // tm
    const = lambda shape: pl.BlockSpec(shape, lambda i: (0, 0))
    table = pl.BlockSpec((tm, LANES), lambda i: (i % bps, 0))
    heads = pl.BlockSpec((1, N_HEADS, tm, LANES), lambda i: (i // bps, 0, i % bps, 0))
    return pl.pallas_call(
        _inproj_kernel,
        out_shape=(jax.ShapeDtypeStruct((b, N_HEADS, s, LANES), BF16),
                   jax.ShapeDtypeStruct((b, N_HEADS, LANES, s), BF16),
                   jax.ShapeDtypeStruct((b, N_HEADS, s, V_HEAD), BF16),
                   jax.ShapeDtypeStruct((n, REST_WIDTH), BF16)),
        grid=(n // tm,),
        in_specs=[
            pl.BlockSpec((tm, D_MODEL), lambda i: (i, 0)),
            const((1, D_MODEL)),
            const((D_MODEL, PROJ_WIDTH)),
            const((1, ATTN_WIDTH)), const((1, ATTN_WIDTH)), table, table,
        ],
        out_specs=(heads,
                   pl.BlockSpec((1, N_HEADS, LANES, tm), lambda i: (i // bps, 0, 0, i % bps)),
                   heads,
                   pl.BlockSpec((tm, REST_WIDTH), lambda i: (i, 0))),
        compiler_params=pltpu.CompilerParams(
            dimension_semantics=("arbitrary",), vmem_limit_bytes=VMEM_LIMIT),
        name="inproj",
    )(x3.reshape(n, d), gain, w_bf16, q_gain, k_gain, cos, sin_signed)


ATTN_SUB_ROWS = 128


SCORE_BOUND_LOG2 = 100.0


def _attn_kernel(lam_ref, sg_ref, qg_ref, kg_ref, q_ref, kt_ref, v_ref, o_ref, *, tq):
    lp = lam_ref[...]
    lam = (jnp.exp(jnp.sum(lp[0:1] * lp[1:2], axis=-1, keepdims=True))
           - jnp.exp(jnp.sum(lp[2:3] * lp[3:4], axis=-1, keepdims=True)) + LAMBDA_INIT)

    score_bound = (SUB_HEAD * (SUB_HEAD ** -0.5 * LOG2_E) * 1.01
                   * jnp.max(jnp.abs(qg_ref[...])) * jnp.max(jnp.abs(kg_ref[...])))
    ts = ATTN_SUB_ROWS

    def chains(subtract_max):
        for t in range(tq // ts):
            q = q_ref[0, 0, t * ts:(t + 1) * ts, :]
            lane = lax.broadcasted_iota(jnp.int32, q.shape, 1)
            zero = jnp.zeros_like(q)
            qq = jnp.concatenate([jnp.where(lane < SUB_HEAD, q, zero),
                                  jnp.where(lane >= SUB_HEAD, q, zero)], axis=0)
            s = jnp.dot(qq, kt_ref[0, 0], preferred_element_type=F32)
            if subtract_max:
                s = s - jnp.max(s, axis=-1, keepdims=True)
            p = jnp.exp2(s)
            l = jnp.sum(p, axis=-1, keepdims=True)
            acc = jnp.dot(p.astype(BF16), v_ref[0, 0], preferred_element_type=F32)
            o = acc / l
            o = o[0:ts] - lam * o[ts:2 * ts]
            o = _rms(o, sg_ref[...]) * (1.0 - LAMBDA_INIT)
            o_ref[0, 0, t * ts:(t + 1) * ts, :] = o.astype(BF16)

    @pl.when(score_bound <= SCORE_BOUND_LOG2)
    def _bounded_scores():
        chains(subtract_max=False)

    @pl.when(jnp.logical_not(score_bound <= SCORE_BOUND_LOG2))
    def _any_scores():
        chains(subtract_max=True)


def _attention(q_hm, kt_hm, v_hm, lam_params, subln_gain, q_gain, k_gain, tq=1024):
    b, h, s, _ = q_hm.shape
    const = lambda shape: pl.BlockSpec(shape, lambda bi, hi, qi: (0, 0))
    return pl.pallas_call(
        functools.partial(_attn_kernel, tq=tq),
        out_shape=jax.ShapeDtypeStruct((b, h, s, V_HEAD), BF16),
        grid=(b, h, s // tq),
        in_specs=[
            const((4, SUB_HEAD)), const((1, V_HEAD)), const((1, ATTN_WIDTH)), const((1, ATTN_WIDTH)),
            pl.BlockSpec((1, 1, tq, LANES), lambda bi, hi, qi: (bi, hi, qi, 0)),
            pl.BlockSpec((1, 1, LANES, s), lambda bi, hi, qi: (bi, hi, 0, 0)),
            pl.BlockSpec((1, 1, s, V_HEAD), lambda bi, hi, qi: (bi, hi, 0, 0)),
        ],
        out_specs=pl.BlockSpec((1, 1, tq, V_HEAD), lambda bi, hi, qi: (bi, hi, qi, 0)),
        compiler_params=pltpu.CompilerParams(
            dimension_semantics=("arbitrary", "arbitrary", "arbitrary"),
            vmem_limit_bytes=VMEM_LIMIT),
        name="diff_attn",
    )(lam_params, subln_gain, q_gain, k_gain, q_hm, kt_hm, v_hm)


def _softplus(x):
    return jnp.maximum(x, 0.0) + jnp.log1p(jnp.exp(-jnp.abs(x)))


def _gelu_tanh(x):
    return 0.5 * x * (1.0 + jnp.tanh(math.sqrt(2.0 / math.pi) * (x + 0.044715 * (x * x * x))))


LRU_BLOCKS_PER_STEP = 2


def _lru_kernel(x_ref, g_ref, cw_ref, cb_ref, w_ref, b_ref, lam_ref, o_ref,
                xs_ref, a0_ref, u0_ref, a1_ref, u1_ref, *, seq, tc):
    pad = SUBLANES
    width = LRU_BLOCKS_PER_STEP * LANES
    zeros_pad = jnp.zeros((pad, width), F32)
    xs_ref[0:pad, :] = zeros_pad
    xs_ref[pad + seq:pad + seq + pad, :] = zeros_pad
    xs_ref[pad:pad + seq, :] = x_ref[0].astype(F32)

    k_all = (-LRU_C * 0.5 * LOG2_E) * _softplus(-lam_ref[...])
    cw_all = cw_ref[...]
    cb_all = cb_ref[...]

    n_seg = SUBLANES
    seg = seq // n_seg
    assert tc == seg
    for c in range(n_seg):
        base = pad + c * tc
        for j in range(LRU_BLOCKS_PER_STEP):
            cols = slice(j * LANES, (j + 1) * LANES)
            cw, cb = cw_all[:, cols], cb_all[:, cols]
            win = xs_ref[base - pad:base + tc + pad, cols]
            n_win = tc + 2 * pad
            taps = (pltpu.roll(win, 1, axis=0), win,
                    pltpu.roll(win, n_win - 1, axis=0), pltpu.roll(win, n_win - 2, axis=0))
            xr = cb + sum(cw[t:t + 1] * taps[t][pad:pad + tc] for t in range(4))
            th = jnp.tanh(jnp.dot(xr.astype(BF16), w_ref[j], preferred_element_type=F32)
                          + b_ref[j])
            for d, (a_ref, u_ref) in enumerate(((a0_ref, u0_ref), (a1_ref, u1_ref))):
                k = k_all[d:d + 1, cols]
                a = jnp.exp2(k * th[:, (2 * d) * LANES:(2 * d + 1) * LANES] + k)
                gate_i = 0.5 * th[:, (2 * d + 1) * LANES:(2 * d + 2) * LANES] + 0.5
                v = 1.0 - a * a
                mult = jnp.where(v > 0.0, v * lax.rsqrt(v), 0.0)
                a_ref[j, pl.ds(c, seg, stride=SUBLANES), :] = a
                u_ref[j, pl.ds(c, seg, stride=SUBLANES), :] = mult * (gate_i * xr)

    chains = [(a0_ref, u0_ref, j, False) for j in range(LRU_BLOCKS_PER_STEP)] \
        + [(a1_ref, u1_ref, j, True) for j in range(LRU_BLOCKS_PER_STEP)]

    def step(t, carry):
        out = []
        for (a_ref, u_ref, j, reverse), (h, p) in zip(chains, carry):
            r0 = pl.multiple_of((seg - 1 - t if reverse else t) * SUBLANES, SUBLANES)
            a = a_ref[j, pl.ds(r0, SUBLANES), :]
            h = a * h + u_ref[j, pl.ds(r0, SUBLANES), :]
            p = a * p
            u_ref[j, pl.ds(r0, SUBLANES), :] = h
            a_ref[j, pl.ds(r0, SUBLANES), :] = p
            out.append((h, p))
        return tuple(out)

    zero = jnp.zeros((SUBLANES, LANES), F32)
    one = jnp.ones((SUBLANES, LANES), F32)
    ends = lax.fori_loop(0, seg, step, ((zero, one),) * len(chains), unroll=8)

    row = lax.broadcasted_iota(jnp.int32, (SUBLANES, LANES), 0)
    entering = []
    for (_, _, _, reverse), (h_end, p_end) in zip(chains, ends):
        state = zero
        order = range(SUBLANES - 2, -1, -1) if reverse else range(1, SUBLANES)
        for s in order:
            nxt = pltpu.roll(h_end + p_end * state, SUBLANES - 1 if reverse else 1, axis=0)
            state = jnp.where(row == s, nxt, state)
        entering.append(state)

    for c in range(n_seg):
        rows = slice(c * tc, (c + 1) * tc)
        for j in range(LRU_BLOCKS_PER_STEP):
            cols = slice(j * LANES, (j + 1) * LANES)
            hs = 0.0
            for (a_ref, u_ref, jj, _), state in zip(chains, entering):
                if jj == j:
                    hs = hs + (u_ref[j, pl.ds(c, seg, stride=SUBLANES), :]
                               + a_ref[j, pl.ds(c, seg, stride=SUBLANES), :] * state[c:c + 1, :])
            y = hs * _gelu_tanh(g_ref[0, rows, cols].astype(F32))
            o_ref[0, rows, cols] = y.astype(BF16)


def _lru(proj3, conv_w, conv_b, w_cat, b_cat, lam):
    b, s, _ = proj3.shape
    tc = s // SUBLANES
    kernel = functools.partial(_lru_kernel, seq=s, tc=tc)
    nb = LRU_BLOCKS_PER_STEP
    width = nb * LANES
    return pl.pallas_call(
        kernel,
        out_shape=jax.ShapeDtypeStruct((b, s, LRU_WIDTH), BF16),
        grid=(b, LRU_BLOCKS // nb),
        in_specs=[
            pl.BlockSpec((1, s, width), lambda bi, ni: (bi, 0, LRUX_COL // nb + ni)),
            pl.BlockSpec((1, s, width), lambda bi, ni: (bi, 0, LRUG_COL // nb + ni)),
            pl.BlockSpec((4, width), lambda bi, ni: (0, ni)),
            pl.BlockSpec((1, width), lambda bi, ni: (0, ni)),
            pl.BlockSpec((nb, LRU_BLOCK_DIM, 4 * LRU_BLOCK_DIM), lambda bi, ni: (ni, 0, 0)),
            pl.BlockSpec((nb, 1, 4 * LRU_BLOCK_DIM), lambda bi, ni: (ni, 0, 0)),
            pl.BlockSpec((2, width), lambda bi, ni: (0, ni)),
        ],
        out_specs=pl.BlockSpec((1, s, width), lambda bi, ni: (bi, 0, ni)),
        scratch_shapes=[pltpu.VMEM((s + 2 * SUBLANES, width), F32)]
        + [pltpu.VMEM((nb, s, LANES), F32)] * 4,
        compiler_params=pltpu.CompilerParams(
            dimension_semantics=("arbitrary", "arbitrary"), vmem_limit_bytes=VMEM_LIMIT),
        name="rglru",
    )(proj3, proj3, conv_w, conv_b, w_cat, b_cat, lam)


def _merge_kernel(x_ref, ao_ref, yl_ref, ga_ref, gl_ref, bg_ref, wa_ref, wl_ref, wo_ref,
                  n2_ref, wr_ref, br_ref, x1_ref, h2_ref, cw_ref):
    attn_o = jnp.concatenate([ao_ref[0, h] for h in range(N_HEADS)], axis=1)
    attn_d = jnp.dot(attn_o, wa_ref[...], preferred_element_type=F32)
    lru_d = jnp.dot(yl_ref[...], wl_ref[...], preferred_element_type=F32)
    bg = bg_ref[...]
    g_attn = jax.nn.sigmoid(ga_ref[...].astype(F32) + bg[:, 0:D_MODEL])
    g_lru = jax.nn.sigmoid(gl_ref[...].astype(F32) + bg[:, D_MODEL:2 * D_MODEL])
    merged = g_attn * attn_d + g_lru * lru_d
    x1 = x_ref[...] + jnp.dot(merged.astype(BF16), wo_ref[...], preferred_element_type=F32)
    x1_ref[...] = x1
    h2 = _rms(x1, n2_ref[...])
    h2_ref[...] = h2.astype(BF16)

    h2_hi = h2.astype(BF16)
    h2_lo = (h2 - h2_hi.astype(F32)).astype(BF16)
    wr = wr_ref[...]
    part = jnp.dot(h2_hi, wr, preferred_element_type=F32)
    logits = (part[:, 0:ROUTER_LANES] + part[:, ROUTER_LANES:]
              + jnp.dot(h2_lo, wr[:, 0:ROUTER_LANES], preferred_element_type=F32) + br_ref[...])
    lane = lax.broadcasted_iota(jnp.int32, logits.shape, 1)
    neg = jnp.full_like(logits, -jnp.inf)
    big = jnp.full_like(lane, ROUTER_LANES)

    def masked_max(mask):
        return jnp.max(jnp.where(mask, logits, neg), axis=-1, keepdims=True)

    def first_lane(mask, value):
        return jnp.min(jnp.where(mask & (logits == value), lane, big), axis=-1, keepdims=True)

    g_mask = lane < N_GROUPS
    g_max = masked_max(g_mask)
    g_sel = first_lane(g_mask, g_max)
    g_w = 1.0 / jnp.sum(jnp.where(g_mask, jnp.exp(logits - g_max), 0.0), axis=-1, keepdims=True)
    e_lo = EXPERT_LANE0 + g_sel * EXPERTS_PER_GROUP
    e_mask = (lane >= e_lo) & (lane < e_lo + EXPERTS_PER_GROUP)
    v1 = masked_max(e_mask)
    i1 = first_lane(e_mask, v1)
    e_mask2 = e_mask & (lane != i1)
    v2 = masked_max(e_mask2)
    i2 = first_lane(e_mask2, v2)
    t = jnp.exp(v2 - v1)
    w1 = g_w / (1.0 + t)
    w2 = g_w * t / (1.0 + t)
    cw_ref[...] = (jnp.where(lane == i1, w1, 0.0) + jnp.where(lane == i2, w2, 0.0)
                   + jnp.where(lane == SORT_CLASS_LANE, g_sel.astype(F32), 0.0))


def _merge(x2, attn_o, y_lru, proj, b_gates, wa, wl, wo, n2_gain, w_router, b_router, tm=512):
    n = x2.shape[0]
    blocks_per_seq = attn_o.shape[2] // tm
    row = lambda cols, col_blk=0: pl.BlockSpec((tm, cols), lambda i: (i, col_blk))
    const = lambda shape: pl.BlockSpec(shape, lambda i: (0, 0))
    heads = pl.BlockSpec((1, N_HEADS, tm, V_HEAD),
                         lambda i: (i // blocks_per_seq, 0, i % blocks_per_seq, 0))
    return pl.pallas_call(
        _merge_kernel,
        out_shape=(jax.ShapeDtypeStruct((n, D_MODEL), F32),
                   jax.ShapeDtypeStruct((n, D_MODEL), BF16),
                   jax.ShapeDtypeStruct((n, ROUTER_LANES), F32)),
        grid=(n // tm,),
        in_specs=[
            row(D_MODEL), heads, row(LRU_WIDTH),
            row(D_MODEL, GATE_COL_1024), row(D_MODEL, GATE_COL_1024 + 1),
            const((1, 2 * D_MODEL)),
            const((ATTN_WIDTH, D_MODEL)), const((LRU_WIDTH, D_MODEL)), const((D_MODEL, D_MODEL)),
            const((1, D_MODEL)), const((D_MODEL, 2 * ROUTER_LANES)), const((1, ROUTER_LANES)),
        ],
        out_specs=(row(D_MODEL), row(D_MODEL), row(ROUTER_LANES)),
        compiler_params=pltpu.CompilerParams(
            dimension_semantics=("arbitrary",), vmem_limit_bytes=VMEM_LIMIT),
        name="merge_router",
    )(x2, attn_o, y_lru, proj, proj, b_gates, wa, wl, wo, n2_gain, w_router, b_router)


MOE_ALIGN = 16
MOE_ALIGN_SHIFT = 4
MOE_WINDOW_ROWS = (256, 320, 384, 512)
MOE_LARGEST_SHIFT = 9
MOE_ROW_PAD = MOE_WINDOW_ROWS[0]
MOE_VMEM_LIMIT = 60 * 1024 * 1024


def _moe_kernel(h_ref, rt_ref, x1_ref, wg_ref, wu_ref, wd_ref, o_ref,
                xs_ref, ys_ref, pos_ref, rng_ref):
    g = pl.program_id(1)
    t_tile = h_ref.shape[0]

    @pl.when(g == 0)
    def _sort_rows():
        rt = rt_ref[...]
        lane = lax.broadcasted_iota(jnp.int32, rt.shape, 1)
        cls = jnp.sum(jnp.where(lane == SORT_CLASS_LANE, rt, 0.0), axis=-1, keepdims=True)
        onehot = lane == cls.astype(jnp.int32)
        r = lax.broadcasted_iota(jnp.int32, (t_tile, t_tile), 0)
        c = lax.broadcasted_iota(jnp.int32, (t_tile, t_tile), 1)
        earlier = jnp.dot((r > c).astype(BF16), onehot.astype(BF16),
                          preferred_element_type=F32)
        cnt = jnp.sum(onehot.astype(F32), axis=0, keepdims=True)
        lane1 = lax.broadcasted_iota(jnp.int32, cnt.shape, 1)
        incl = cnt
        for d in (1, 2, 4, 8, 16):
            incl = incl + jnp.where(lane1 >= d, pltpu.roll(incl, d, axis=1), 0.0)
        off = incl - cnt
        pos = jnp.sum(jnp.where(onehot, earlier + off, 0.0), axis=-1, keepdims=True)
        pos_b = jnp.broadcast_to(pos, rt.shape)
        pos_ref[...] = pos_b
        pos_row = pos_b.T[0:1, :]
        perm = (r.astype(F32) == pos_row).astype(BF16)
        rt_hi = rt.astype(BF16)
        rt_lo = (rt - rt_hi.astype(F32)).astype(BF16)
        cat = jnp.concatenate([h_ref[...], rt_hi, rt_lo], axis=1)
        xs_ref[0:t_tile, :] = jnp.dot(perm, cat, preferred_element_type=F32).astype(BF16)
        xs_ref[t_tile:, :] = jnp.zeros((xs_ref.shape[0] - t_tile, xs_ref.shape[1]), BF16)
        ys_ref[...] = jnp.zeros(ys_ref.shape, F32)
        for k in range(N_SORT_CLASSES + 1):
            rng_ref[k] = jnp.sum(jnp.where(lane1 == k, off, 0.0)).astype(jnp.int32)

    def window(r0, rows):
        r0 = pl.multiple_of(r0, MOE_ALIGN)
        xc = xs_ref[pl.ds(r0, rows), 0:D_MODEL]
        rs = (xs_ref[pl.ds(r0, rows), D_MODEL:D_MODEL + ROUTER_LANES].astype(F32)
              + xs_ref[pl.ds(r0, rows), D_MODEL + ROUTER_LANES:].astype(F32))
        lane = lax.broadcasted_iota(jnp.int32, rs.shape, 1)
        y = jnp.zeros((rows, D_MODEL), F32)
        for e in range(EXPERTS_PER_GROUP):
            a = jnp.dot(xc, wg_ref[e], preferred_element_type=F32)
            u = jnp.dot(xc, wu_ref[e], preferred_element_type=F32)
            ce = jnp.sum(jnp.where(lane == EXPERT_LANE0 + g * EXPERTS_PER_GROUP + e, rs, 0.0),
                         axis=-1, keepdims=True)
            hid = a * jax.nn.sigmoid(a) * u * ce
            y = y + jnp.dot(hid.astype(BF16), wd_ref[e], preferred_element_type=F32)
        ys_ref[pl.ds(r0, rows), :] += y

    largest = MOE_WINDOW_ROWS[-1]
    start = rng_ref[g * CLASSES_PER_GROUP]
    stop = rng_ref[(g + 1) * CLASSES_PER_GROUP]
    first = lax.shift_left(lax.shift_right_logical(start, MOE_ALIGN_SHIFT), MOE_ALIGN_SHIFT)
    span = jnp.where(stop > start, stop - first, 0)
    n_full = lax.shift_right_logical(span, MOE_LARGEST_SHIFT)

    def full_window(ci, carry):
        window(first + ci * largest, largest)
        return carry

    lax.fori_loop(0, n_full, full_window, 0)
    rem = span - lax.shift_left(n_full, MOE_LARGEST_SHIFT)
    rem_start = first + lax.shift_left(n_full, MOE_LARGEST_SHIFT)
    for lower, rows in zip((0,) + MOE_WINDOW_ROWS[:-1], MOE_WINDOW_ROWS):
        @pl.when((rem > lower) & (rem <= rows))
        def _remainder_window(rows=rows):
            window(rem_start, rows)

    @pl.when(g == N_GROUPS - 1)
    def _unsort_rows():
        pos_b = pos_ref[...]
        c = lax.broadcasted_iota(jnp.int32, (t_tile, t_tile), 1).astype(F32)
        unperm = (jnp.concatenate([pos_b] * (t_tile // LANES), axis=1) == c).astype(BF16)
        o_ref[...] = x1_ref[...] + jnp.dot(unperm, ys_ref[0:t_tile, :].astype(BF16),
                                           preferred_element_type=F32)


def _moe(h2, route, x1, wg, wu, wd, tm=1024):
    n = h2.shape[0]
    rows = lambda i, g: (i, 0)
    group_w = lambda r, c: pl.BlockSpec((EXPERTS_PER_GROUP, r, c), lambda i, g: (g, 0, 0))
    return pl.pallas_call(
        _moe_kernel,
        out_shape=jax.ShapeDtypeStruct((n, D_MODEL), F32),
        grid=(n // tm, N_GROUPS),
        in_specs=[
            pl.BlockSpec((tm, D_MODEL), rows),
            pl.BlockSpec((tm, ROUTER_LANES), rows),
            pl.BlockSpec((tm, D_MODEL), rows),
            group_w(D_MODEL, EXPERT_HIDDEN), group_w(D_MODEL, EXPERT_HIDDEN),
            group_w(EXPERT_HIDDEN, D_MODEL),
        ],
        out_specs=pl.BlockSpec((tm, D_MODEL), rows),
        scratch_shapes=[
            pltpu.VMEM((tm + MOE_ROW_PAD, D_MODEL + 2 * ROUTER_LANES), BF16),
            pltpu.VMEM((tm + MOE_ROW_PAD, D_MODEL), F32),
            pltpu.VMEM((tm, ROUTER_LANES), F32),
            pltpu.SMEM((N_SORT_CLASSES + 1,), jnp.int32),
        ],
        compiler_params=pltpu.CompilerParams(
            dimension_semantics=("arbitrary", "arbitrary"), vmem_limit_bytes=MOE_VMEM_LIMIT),
        name="moe",
    )(h2, route, x1, wg, wu, wd)


def _rope_tables(seq):
    pos = jnp.arange(seq, dtype=F32)
    inv_freq = ROPE_THETA ** (-jnp.arange(0, SUB_HEAD, 2, dtype=F32) / SUB_HEAD)
    ang = pos[:, None] * inv_freq[None, :]
    cos, sin = jnp.cos(ang), jnp.sin(ang)
    cos_full = jnp.concatenate([cos, cos, cos, cos], axis=-1)
    sin_signed = jnp.concatenate([-sin, sin, -sin, sin], axis=-1)
    return cos_full, sin_signed


def kernel(x, norm1_gain, w_in, b_gates, q_norm_gain, k_norm_gain, lambda_q1, lambda_k1, lambda_q2, lambda_k2, attn_subln_gain, w_attn_o, conv_w, conv_b, lru_wa, lru_ba, lru_wi, lru_bi, lru_lambda, w_lru_o, w_out, norm2_gain, w_group_router, b_group_router, w_expert_router, b_expert_router, w_expert_gate, w_expert_up, w_expert_down):
    b, s, d = x.shape
    n = b * s
    depth = w_in.shape[0]
    assert depth == 1 and d == D_MODEL
    cos, sin_signed = _rope_tables(s)
    x2 = x.reshape(n, d)
    l = 0

    sub_heads = ATTN_WIDTH // SUB_HEAD
    q_gain = jnp.tile(q_norm_gain[l], sub_heads)[None, :]
    k_gain = jnp.tile(k_norm_gain[l], sub_heads)[None, :]
    q_hm, kt_hm, v_hm, rest = _inproj(
        x, norm1_gain[l][None, :], w_in[l].astype(BF16), q_gain, k_gain, cos, sin_signed)
    proj3 = rest.reshape(b, s, REST_WIDTH)

    lam_params = jnp.stack([lambda_q1[l], lambda_k1[l], lambda_q2[l], lambda_k2[l]])
    attn_o = _attention(q_hm, kt_hm, v_hm, lam_params, attn_subln_gain[l][None, :],
                        q_gain, k_gain)

    w_cat = (0.5 * jnp.concatenate([lru_wa[l, 0], lru_wi[l, 0], lru_wa[l, 1], lru_wi[l, 1]],
                                   axis=-1)).astype(BF16)
    blk = lambda v: v.reshape(LRU_BLOCKS, 1, LRU_BLOCK_DIM)
    b_cat = 0.5 * jnp.concatenate([blk(lru_ba[l, 0]), blk(lru_bi[l, 0]),
                                   blk(lru_ba[l, 1]), blk(lru_bi[l, 1])], axis=-1)
    y_lru = _lru(proj3, conv_w[l], conv_b[l][None, :], w_cat, b_cat, lru_lambda[l])

    pad = ROUTER_LANES - N_GROUPS - N_EXPERTS
    w_router = jnp.concatenate([w_group_router[l], w_expert_router[l],
                                jnp.zeros((d, pad), F32)], axis=-1)
    b_router = jnp.concatenate([b_group_router[l], b_expert_router[l],
                                jnp.zeros((pad,), F32)])[None, :]
    w_router_hi = w_router.astype(BF16)
    w_router = jnp.concatenate(
        [w_router_hi, (w_router - w_router_hi.astype(F32)).astype(BF16)], axis=-1)
    x1, h2, cw = _merge(
        x2, attn_o, y_lru.reshape(n, LRU_WIDTH), rest,
        b_gates[l][None, :], w_attn_o[l].astype(BF16), w_lru_o[l].astype(BF16),
        w_out[l].astype(BF16), norm2_gain[l][None, :], w_router, b_router)

    out = _moe(h2, cw, x1, w_expert_gate[l].astype(BF16), w_expert_up[l].astype(BF16),
               w_expert_down[l].astype(BF16))
    return out.reshape(b, s, d)
```

```python
import functools
import math

import jax
import jax.numpy as jnp
from jax import lax
from jax.experimental import pallas as pl
from jax.experimental.pallas import tpu as pltpu

F32 = jnp.float32
BF16 = jnp.bfloat16

D_MODEL = 1024
N_HEADS = 8
SUB_HEAD = 64
V_HEAD = 128
ATTN_WIDTH = 1024
LRU_WIDTH = 1024
LRU_BLOCKS = 8
LRU_BLOCK_DIM = 128
LRU_C = 8.0
PROJ_WIDTH = 7168
N_GROUPS = 4
EXPERTS_PER_GROUP = 4
N_EXPERTS = 16
EXPERT_HIDDEN = 512
ROPE_THETA = 10000.0
RMS_EPS = 1e-6
LAMBDA_INIT = 0.8 - 0.6 * math.exp(-0.3 * 0)

LOG2_E = math.log2(math.e)
LANES = 128
SUBLANES = 8
VMEM_LIMIT = 56 * 1024 * 1024

REST_WIDTH = PROJ_WIDTH - 3 * ATTN_WIDTH
LRUX_COL, LRUG_COL = 0, 8
GATE_COL_1024 = 2

ROUTER_LANES = 128
EXPERT_LANE0 = N_GROUPS
SORT_CLASS_LANE = N_GROUPS + N_EXPERTS
CLASSES_PER_GROUP = 1
N_SORT_CLASSES = N_GROUPS * CLASSES_PER_GROUP


def _rms(x, gain):
    ms = jnp.mean(x * x, axis=-1, keepdims=True)
    return x * lax.rsqrt(ms + RMS_EPS) * gain


MXU_TILE = 256


def _subhead_norm_rope(x, gain, cos, sin_signed):
    width = x.shape[1]
    r = lax.broadcasted_iota(jnp.int32, (MXU_TILE, MXU_TILE), 0) // SUB_HEAD
    c = lax.broadcasted_iota(jnp.int32, (MXU_TILE, MXU_TILE), 1) // SUB_HEAD
    group_mean = jnp.where(r == c, 1.0 / SUB_HEAD, 0.0).astype(BF16)
    xx = (x * x).astype(BF16)
    ms = jnp.concatenate(
        [jnp.dot(xx[:, t * MXU_TILE:(t + 1) * MXU_TILE], group_mean, preferred_element_type=F32)
         for t in range(width // MXU_TILE)], axis=1)
    xn = x * lax.rsqrt(ms + RMS_EPS) * gain
    lane = lax.broadcasted_iota(jnp.int32, xn.shape, 1)
    first_half = (lane % SUB_HEAD) < (SUB_HEAD // 2)
    partner = jnp.where(first_half,
                        pltpu.roll(xn, width - SUB_HEAD // 2, axis=1),
                        pltpu.roll(xn, SUB_HEAD // 2, axis=1))
    reps = width // LANES
    return (xn * jnp.concatenate([cos] * reps, axis=1)
            + partner * jnp.concatenate([sin_signed] * reps, axis=1))


def _inproj_kernel(x_ref, g_ref, w_ref, qg_ref, kg_ref, cos_ref, sin_ref,
                   q_ref, kt_ref, v_ref, rest_ref):
    h = _rms(x_ref[...], g_ref[...]).astype(BF16)
    cos, sin_signed = cos_ref[...], sin_ref[...]

    def proj(j):
        return jnp.dot(h, w_ref[:, j * 1024:(j + 1) * 1024], preferred_element_type=F32)

    q = _subhead_norm_rope(proj(0), qg_ref[...], cos, sin_signed) * (SUB_HEAD ** -0.5 * LOG2_E)
    k = _subhead_norm_rope(proj(1), kg_ref[...], cos, sin_signed)
    v = proj(2).astype(BF16)
    for hd in range(N_HEADS):
        cols = slice(hd * LANES, (hd + 1) * LANES)
        q_ref[0, hd] = q[:, cols].astype(BF16)
        kt_ref[0, hd] = k[:, cols].T.astype(BF16)
        v_ref[0, hd] = v[:, cols]
    for j in range(3, PROJ_WIDTH // 1024):
        rest_ref[:, (j - 3) * 1024:(j - 2) * 1024] = proj(j).astype(BF16)


def _inproj(x3, gain, w_bf16, q_gain, k_gain, cos, sin_signed, tm=512):
    b, s, d = x3.shape
    n = b * s
    bps = s // tm
    const = lambda shape: pl.BlockSpec(shape, lambda i: (0, 0))
    table = pl.BlockSpec((tm, LANES), lambda i: (i % bps, 0))
    heads = pl.BlockSpec((1, N_HEADS, tm, LANES), lambda i: (i // bps, 0, i % bps, 0))
    return pl.pallas_call(
        _inproj_kernel,
        out_shape=(jax.ShapeDtypeStruct((b, N_HEADS, s, LANES), BF16),
                   jax.ShapeDtypeStruct((b, N_HEADS, LANES, s), BF16),
                   jax.ShapeDtypeStruct((b, N_HEADS, s, V_HEAD), BF16),
                   jax.ShapeDtypeStruct((n, REST_WIDTH), BF16)),
        grid=(n // tm,),
        in_specs=[
            pl.BlockSpec((tm, D_MODEL), lambda i: (i, 0)),
            const((1, D_MODEL)),
            const((D_MODEL, PROJ_WIDTH)),
            const((1, ATTN_WIDTH)), const((1, ATTN_WIDTH)), table, table,
        ],
        out_specs=(heads,
                   pl.BlockSpec((1, N_HEADS, LANES, tm), lambda i: (i // bps, 0, 0, i % bps)),
                   heads,
                   pl.BlockSpec((tm, REST_WIDTH), lambda i: (i, 0))),
        compiler_params=pltpu.CompilerParams(
            dimension_semantics=("arbitrary",), vmem_limit_bytes=VMEM_LIMIT),
        name="inproj",
    )(x3.reshape(n, d), gain, w_bf16, q_gain, k_gain, cos, sin_signed)


ATTN_SUB_ROWS = 128


SCORE_BOUND_LOG2 = 100.0


def _attn_kernel(lam_ref, sg_ref, qg_ref, kg_ref, q_ref, kt_ref, v_ref, o_ref, *, tq):
    lp = lam_ref[...]
    lam = (jnp.exp(jnp.sum(lp[0:1] * lp[1:2], axis=-1, keepdims=True))
           - jnp.exp(jnp.sum(lp[2:3] * lp[3:4], axis=-1, keepdims=True)) + LAMBDA_INIT)

    score_bound = (SUB_HEAD * (SUB_HEAD ** -0.5 * LOG2_E) * 1.01
                   * jnp.max(jnp.abs(qg_ref[...])) * jnp.max(jnp.abs(kg_ref[...])))
    ts = ATTN_SUB_ROWS

    def chains(subtract_max):
        for t in range(tq // ts):
            q = q_ref[0, 0, t * ts:(t + 1) * ts, :]
            lane = lax.broadcasted_iota(jnp.int32, q.shape, 1)
            zero = jnp.zeros_like(q)
            qq = jnp.concatenate([jnp.where(lane < SUB_HEAD, q, zero),
                                  jnp.where(lane >= SUB_HEAD, q, zero)], axis=0)
            s = jnp.dot(qq, kt_ref[0, 0], preferred_element_type=F32)
            if subtract_max:
                s = s - jnp.max(s, axis=-1, keepdims=True)
            p = jnp.exp2(s)
            l = jnp.sum(p, axis=-1, keepdims=True)
            acc = jnp.dot(p.astype(BF16), v_ref[0, 0], preferred_element_type=F32)
            o = acc / l
            o = o[0:ts] - lam * o[ts:2 * ts]
            o = _rms(o, sg_ref[...]) * (1.0 - LAMBDA_INIT)
            o_ref[0, 0, t * ts:(t + 1) * ts, :] = o.astype(BF16)

    @pl.when(score_bound <= SCORE_BOUND_LOG2)
    def _bounded_scores():
        chains(subtract_max=False)

    @pl.when(jnp.logical_not(score_bound <= SCORE_BOUND_LOG2))
    def _any_scores():
        chains(subtract_max=True)


def _attention(q_hm, kt_hm, v_hm, lam_params, subln_gain, q_gain, k_gain, tq=1024):
    b, h, s, _ = q_hm.shape
    const = lambda shape: pl.BlockSpec(shape, lambda bi, hi, qi: (0, 0))
    return pl.pallas_call(
        functools.partial(_attn_kernel, tq=tq),
        out_shape=jax.ShapeDtypeStruct((b, h, s, V_HEAD), BF16),
        grid=(b, h, s // tq),
        in_specs=[
            const((4, SUB_HEAD)), const((1, V_HEAD)), const((1, ATTN_WIDTH)), const((1, ATTN_WIDTH)),
            pl.BlockSpec((1, 1, tq, LANES), lambda bi, hi, qi: (bi, hi, qi, 0)),
            pl.BlockSpec((1, 1, LANES, s), lambda bi, hi, qi: (bi, hi, 0, 0)),
            pl.BlockSpec((1, 1, s, V_HEAD), lambda bi, hi, qi: (bi, hi, 0, 0)),
        ],
        out_specs=pl.BlockSpec((1, 1, tq, V_HEAD), lambda bi, hi, qi: (bi, hi, qi, 0)),
        compiler_params=pltpu.CompilerParams(
            dimension_semantics=("arbitrary", "arbitrary", "arbitrary"),
            vmem_limit_bytes=VMEM_LIMIT),
        name="diff_attn",
    )(lam_params, subln_gain, q_gain, k_gain, q_hm, kt_hm, v_hm)


def _softplus(x):
    return jnp.maximum(x, 0.0) + jnp.log1p(jnp.exp(-jnp.abs(x)))


def _gelu_tanh(x):
    return 0.5 * x * (1.0 + jnp.tanh(math.sqrt(2.0 / math.pi) * (x + 0.044715 * (x * x * x))))


LRU_BLOCKS_PER_STEP = 2


def _lru_kernel(x_ref, g_ref, cw_ref, cb_ref, w_ref, b_ref, lam_ref, o_ref,
                xs_ref, a0_ref, u0_ref, a1_ref, u1_ref, *, seq, tc):
    pad = SUBLANES
    width = LRU_BLOCKS_PER_STEP * LANES
    zeros_pad = jnp.zeros((pad, width), F32)
    xs_ref[0:pad, :] = zeros_pad
    xs_ref[pad + seq:pad + seq + pad, :] = zeros_pad
    xs_ref[pad:pad + seq, :] = x_ref[0].astype(F32)

    k_all = (-LRU_C * 0.5 * LOG2_E) * _softplus(-lam_ref[...])
    cw_all = cw_ref[...]
    cb_all = cb_ref[...]

    n_seg = SUBLANES
    seg = seq // n_seg
    assert tc == seg
    for c in range(n_seg):
        base = pad + c * tc
        for j in range(LRU_BLOCKS_PER_STEP):
            cols = slice(j * LANES, (j + 1) * LANES)
            cw, cb = cw_all[:, cols], cb_all[:, cols]
            win = xs_ref[base - pad:base + tc + pad, cols]
            n_win = tc + 2 * pad
            taps = (pltpu.roll(win, 1, axis=0), win,
                    pltpu.roll(win, n_win - 1, axis=0), pltpu.roll(win, n_win - 2, axis=0))
            xr = cb + sum(cw[t:t + 1] * taps[t][pad:pad + tc] for t in range(4))
            th = jnp.tanh(jnp.dot(xr.astype(BF16), w_ref[j], preferred_element_type=F32)
                          + b_ref[j])
            for d, (a_ref, u_ref) in enumerate(((a0_ref, u0_ref), (a1_ref, u1_ref))):
                k = k_all[d:d + 1, cols]
                a = jnp.exp2(k * th[:, (2 * d) * LANES:(2 * d + 1) * LANES] + k)
                gate_i = 0.5 * th[:, (2 * d + 1) * LANES:(2 * d + 2) * LANES] + 0.5
                v = 1.0 - a * a
                mult = jnp.where(v > 0.0, v * lax.rsqrt(v), 0.0)
                a_ref[j, pl.ds(c, seg, stride=SUBLANES), :] = a
                u_ref[j, pl.ds(c, seg, stride=SUBLANES), :] = mult * (gate_i * xr)

    chains = [(a0_ref, u0_ref, j, False) for j in range(LRU_BLOCKS_PER_STEP)] \
        + [(a1_ref, u1_ref, j, True) for j in range(LRU_BLOCKS_PER_STEP)]

    def step(t, carry):
        out = []
        for (a_ref, u_ref, j, reverse), (h, p) in zip(chains, carry):
            r0 = pl.multiple_of((seg - 1 - t if reverse else t) * SUBLANES, SUBLANES)
            a = a_ref[j, pl.ds(r0, SUBLANES), :]
            h = a * h + u_ref[j, pl.ds(r0, SUBLANES), :]
            p = a * p
            u_ref[j, pl.ds(r0, SUBLANES), :] = h
            a_ref[j, pl.ds(r0, SUBLANES), :] = p
            out.append((h, p))
        return tuple(out)

    zero = jnp.zeros((SUBLANES, LANES), F32)
    one = jnp.ones((SUBLANES, LANES), F32)
    ends = lax.fori_loop(0, seg, step, ((zero, one),) * len(chains), unroll=8)

    row = lax.broadcasted_iota(jnp.int32, (SUBLANES, LANES), 0)
    entering = []
    for (_, _, _, reverse), (h_end, p_end) in zip(chains, ends):
        state = zero
        order = range(SUBLANES - 2, -1, -1) if reverse else range(1, SUBLANES)
        for s in order:
            nxt = pltpu.roll(h_end + p_end * state, SUBLANES - 1 if reverse else 1, axis=0)
            state = jnp.where(row == s, nxt, state)
        entering.append(state)

    for c in range(n_seg):
        rows = slice(c * tc, (c + 1) * tc)
        for j in range(LRU_BLOCKS_PER_STEP):
            cols = slice(j * LANES, (j + 1) * LANES)
            hs = 0.0
            for (a_ref, u_ref, jj, _), state in zip(chains, entering):
                if jj == j:
                    hs = hs + (u_ref[j, pl.ds(c, seg, stride=SUBLANES), :]
                               + a_ref[j, pl.ds(c, seg, stride=SUBLANES), :] * state[c:c + 1, :])
            y = hs * _gelu_tanh(g_ref[0, rows, cols].astype(F32))
            o_ref[0, rows, cols] = y.astype(BF16)


def _lru(proj3, conv_w, conv_b, w_cat, b_cat, lam):
    b, s, _ = proj3.shape
    tc = s // SUBLANES
    kernel = functools.partial(_lru_kernel, seq=s, tc=tc)
    nb = LRU_BLOCKS_PER_STEP
    width = nb * LANES
    return pl.pallas_call(
        kernel,
        out_shape=jax.ShapeDtypeStruct((b, s, LRU_WIDTH), BF16),
        grid=(b, LRU_BLOCKS // nb),
        in_specs=[
            pl.BlockSpec((1, s, width), lambda bi, ni: (bi, 0, LRUX_COL // nb + ni)),
            pl.BlockSpec((1, s, width), lambda bi, ni: (bi, 0, LRUG_COL // nb + ni)),
            pl.BlockSpec((4, width), lambda bi, ni: (0, ni)),
            pl.BlockSpec((1, width), lambda bi, ni: (0, ni)),
            pl.BlockSpec((nb, LRU_BLOCK_DIM, 4 * LRU_BLOCK_DIM), lambda bi, ni: (ni, 0, 0)),
            pl.BlockSpec((nb, 1, 4 * LRU_BLOCK_DIM), lambda bi, ni: (ni, 0, 0)),
            pl.BlockSpec((2, width), lambda bi, ni: (0, ni)),
        ],
        out_specs=pl.BlockSpec((1, s, width), lambda bi, ni: (bi, 0, ni)),
        scratch_shapes=[pltpu.VMEM((s + 2 * SUBLANES, width), F32)]
        + [pltpu.VMEM((nb, s, LANES), F32)] * 4,
        compiler_params=pltpu.CompilerParams(
            dimension_semantics=("arbitrary", "arbitrary"), vmem_limit_bytes=VMEM_LIMIT),
        name="rglru",
    )(proj3, proj3, conv_w, conv_b, w_cat, b_cat, lam)


def _merge_kernel(x_ref, ao_ref, yl_ref, ga_ref, gl_ref, bg_ref, wa_ref, wl_ref, wo_ref,
                  n2_ref, wr_ref, br_ref, x1_ref, h2_ref, cw_ref):
    attn_o = jnp.concatenate([ao_ref[0, h] for h in range(N_HEADS)], axis=1)
    attn_d = jnp.dot(attn_o, wa_ref[...], preferred_element_type=F32)
    lru_d = jnp.dot(yl_ref[...], wl_ref[...], preferred_element_type=F32)
    bg = bg_ref[...]
    g_attn = jax.nn.sigmoid(ga_ref[...].astype(F32) + bg[:, 0:D_MODEL])
    g_lru = jax.nn.sigmoid(gl_ref[...].astype(F32) + bg[:, D_MODEL:2 * D_MODEL])
    merged = g_attn * attn_d + g_lru * lru_d
    x1 = x_ref[...] + jnp.dot(merged.astype(BF16), wo_ref[...], preferred_element_type=F32)
    x1_ref[...] = x1
    h2 = _rms(x1, n2_ref[...])
    h2_ref[...] = h2.astype(BF16)

    h2_hi = h2.astype(BF16)
    h2_lo = (h2 - h2_hi.astype(F32)).astype(BF16)
    wr = wr_ref[...]
    part = jnp.dot(h2_hi, wr, preferred_element_type=F32)
    logits = (part[:, 0:ROUTER_LANES] + part[:, ROUTER_LANES:]
              + jnp.dot(h2_lo, wr[:, 0:ROUTER_LANES], preferred_element_type=F32) + br_ref[...])
    lane = lax.broadcasted_iota(jnp.int32, logits.shape, 1)
    neg = jnp.full_like(logits, -jnp.inf)
    big = jnp.full_like(lane, ROUTER_LANES)

    def masked_max(mask):
        return jnp.max(jnp.where(mask, logits, neg), axis=-1, keepdims=True)

    def first_lane(mask, value):
        return jnp.min(jnp.where(mask & (logits == value), lane, big), axis=-1, keepdims=True)

    g_mask = lane < N_GROUPS
    g_max = masked_max(g_mask)
    g_sel = first_lane(g_mask, g_max)
    g_w = 1.0 / jnp.sum(jnp.where(g_mask, jnp.exp(logits - g_max), 0.0), axis=-1, keepdims=True)
    e_lo = EXPERT_LANE0 + g_sel * EXPERTS_PER_GROUP
    e_mask = (lane >= e_lo) & (lane < e_lo + EXPERTS_PER_GROUP)
    v1 = masked_max(e_mask)
    i1 = first_lane(e_mask, v1)
    e_mask2 = e_mask & (lane != i1)
    v2 = masked_max(e_mask2)
    i2 = first_lane(e_mask2, v2)
    t = jnp.exp(v2 - v1)
    w1 = g_w / (1.0 + t)
    w2 = g_w * t / (1.0 + t)
    cw_ref[...] = (jnp.where(lane == i1, w1, 0.0) + jnp.where(lane == i2, w2, 0.0)
                   + jnp.where(lane == SORT_CLASS_LANE, g_sel.astype(F32), 0.0))


def _merge(x2, attn_o, y_lru, proj, b_gates, wa, wl, wo, n2_gain, w_router, b_router, tm=512):
    n = x2.shape[0]
    blocks_per_seq = attn_o.shape[2] // tm
    row = lambda cols, col_blk=0: pl.BlockSpec((tm, cols), lambda i: (i, col_blk))
    const = lambda shape: pl.BlockSpec(shape, lambda i: (0, 0))
    heads = pl.BlockSpec((1, N_HEADS, tm, V_HEAD),
                         lambda i: (i // blocks_per_seq, 0, i % blocks_per_seq, 0))
    return pl.pallas_call(
        _merge_kernel,
        out_shape=(jax.ShapeDtypeStruct((n, D_MODEL), F32),
                   jax.ShapeDtypeStruct((n, D_MODEL), BF16),
                   jax.ShapeDtypeStruct((n, ROUTER_LANES), F32)),
        grid=(n // tm,),
        in_specs=[
            row(D_MODEL), heads, row(LRU_WIDTH),
            row(D_MODEL, GATE_COL_1024), row(D_MODEL, GATE_COL_1024 + 1),
            const((1, 2 * D_MODEL)),
            const((ATTN_WIDTH, D_MODEL)), const((LRU_WIDTH, D_MODEL)), const((D_MODEL, D_MODEL)),
            const((1, D_MODEL)), const((D_MODEL, 2 * ROUTER_LANES)), const((1, ROUTER_LANES)),
        ],
        out_specs=(row(D_MODEL), row(D_MODEL), row(ROUTER_LANES)),
        compiler_params=pltpu.CompilerParams(
            dimension_semantics=("arbitrary",), vmem_limit_bytes=VMEM_LIMIT),
        name="merge_router",
    )(x2, attn_o, y_lru, proj, proj, b_gates, wa, wl, wo, n2_gain, w_router, b_router)


MOE_ALIGN = 16
MOE_ALIGN_SHIFT = 4
MOE_WINDOW_ROWS = (256, 320, 384, 512)
MOE_LARGEST_SHIFT = 9
MOE_ROW_PAD = MOE_WINDOW_ROWS[0]
MOE_VMEM_LIMIT = 60 * 1024 * 1024


def _moe_kernel(h_ref, rt_ref, x1_ref, wg_ref, wu_ref, wd_ref, o_ref,
                xs_ref, ys_ref, pos_ref, rng_ref):
    g = pl.program_id(1)
    t_tile = h_ref.shape[0]

    @pl.when(g == 0)
    def _sort_rows():
        rt = rt_ref[...]
        lane = lax.broadcasted_iota(jnp.int32, rt.shape, 1)
        cls = jnp.sum(jnp.where(lane == SORT_CLASS_LANE, rt, 0.0), axis=-1, keepdims=True)
        onehot = lane == cls.astype(jnp.int32)
        r = lax.broadcasted_iota(jnp.int32, (t_tile, t_tile), 0)
        c = lax.broadcasted_iota(jnp.int32, (t_tile, t_tile), 1)
        earlier = jnp.dot((r > c).astype(BF16), onehot.astype(BF16),
                          preferred_element_type=F32)
        cnt = jnp.sum(onehot.astype(F32), axis=0, keepdims=True)
        lane1 = lax.broadcasted_iota(jnp.int32, cnt.shape, 1)
        incl = cnt
        for d in (1, 2, 4, 8, 16):
            incl = incl + jnp.where(lane1 >= d, pltpu.roll(incl, d, axis=1), 0.0)
        off = incl - cnt
        pos = jnp.sum(jnp.where(onehot, earlier + off, 0.0), axis=-1, keepdims=True)
        pos_b = jnp.broadcast_to(pos, rt.shape)
        pos_row = pos_b.T[0:1, :]
        perm = (r.astype(F32) == pos_row).astype(BF16)
        pos_ref[...] = perm
        rt_hi = rt.astype(BF16)
        rt_lo = (rt - rt_hi.astype(F32)).astype(BF16)
        cat = jnp.concatenate([h_ref[...], rt_hi, rt_lo], axis=1)
        xs_ref[0:t_tile, :] = jnp.dot(perm, cat, preferred_element_type=F32).astype(BF16)
        xs_ref[t_tile:, :] = jnp.zeros((xs_ref.shape[0] - t_tile, xs_ref.shape[1]), BF16)
        ys_ref[...] = jnp.zeros(ys_ref.shape, F32)
        for k in range(N_SORT_CLASSES + 1):
            rng_ref[k] = jnp.sum(jnp.where(lane1 == k, off, 0.0)).astype(jnp.int32)

    def window(r0, rows):
        r0 = pl.multiple_of(r0, MOE_ALIGN)
        xc = xs_ref[pl.ds(r0, rows), 0:D_MODEL]
        rs = (xs_ref[pl.ds(r0, rows), D_MODEL:D_MODEL + ROUTER_LANES].astype(F32)
              + xs_ref[pl.ds(r0, rows), D_MODEL + ROUTER_LANES:].astype(F32))
        lane = lax.broadcasted_iota(jnp.int32, rs.shape, 1)
        y = jnp.zeros((rows, D_MODEL), F32)
        for e in range(EXPERTS_PER_GROUP):
            a = jnp.dot(xc, wg_ref[e], preferred_element_type=F32)
            u = jnp.dot(xc, wu_ref[e], preferred_element_type=F32)
            ce = jnp.sum(jnp.where(lane == EXPERT_LANE0 + g * EXPERTS_PER_GROUP + e, rs, 0.0),
                         axis=-1, keepdims=True)
            hid = a * jax.nn.sigmoid(a) * u * ce
            y = y + jnp.dot(hid.astype(BF16), wd_ref[e], preferred_element_type=F32)
        ys_ref[pl.ds(r0, rows), :] += y

    largest = MOE_WINDOW_ROWS[-1]
    start = rng_ref[g * CLASSES_PER_GROUP]
    stop = rng_ref[(g + 1) * CLASSES_PER_GROUP]
    first = lax.shift_left(lax.shift_right_logical(start, MOE_ALIGN_SHIFT), MOE_ALIGN_SHIFT)
    span = jnp.where(stop > start, stop - first, 0)
    n_full = lax.shift_right_logical(span, MOE_LARGEST_SHIFT)

    def full_window(ci, carry):
        window(first + ci * largest, largest)
        return carry

    lax.fori_loop(0, n_full, full_window, 0)
    rem = span - lax.shift_left(n_full, MOE_LARGEST_SHIFT)
    rem_start = first + lax.shift_left(n_full, MOE_LARGEST_SHIFT)
    for lower, rows in zip((0,) + MOE_WINDOW_ROWS[:-1], MOE_WINDOW_ROWS):
        @pl.when((rem > lower) & (rem <= rows))
        def _remainder_window(rows=rows):
            window(rem_start, rows)

    @pl.when(g == N_GROUPS - 1)
    def _unsort_rows():
        o_ref[...] = x1_ref[...] + lax.dot_general(
            pos_ref[...], ys_ref[0:t_tile, :].astype(BF16), (((0,), (0,)), ((), ())),
            preferred_element_type=F32)


def _moe(h2, route, x1, wg, wu, wd, tm=1024):
    n = h2.shape[0]
    rows = lambda i, g: (i, 0)
    group_w = lambda r, c: pl.BlockSpec((EXPERTS_PER_GROUP, r, c), lambda i, g: (g, 0, 0))
    return pl.pallas_call(
        _moe_kernel,
        out_shape=jax.ShapeDtypeStruct((n, D_MODEL), F32),
        grid=(n // tm, N_GROUPS),
        in_specs=[
            pl.BlockSpec((tm, D_MODEL), rows),
            pl.BlockSpec((tm, ROUTER_LANES), rows),
            pl.BlockSpec((tm, D_MODEL), rows),
            group_w(D_MODEL, EXPERT_HIDDEN), group_w(D_MODEL, EXPERT_HIDDEN),
            group_w(EXPERT_HIDDEN, D_MODEL),
        ],
        out_specs=pl.BlockSpec((tm, D_MODEL), rows),
        scratch_shapes=[
            pltpu.VMEM((tm + MOE_ROW_PAD, D_MODEL + 2 * ROUTER_LANES), BF16),
            pltpu.VMEM((tm + MOE_ROW_PAD, D_MODEL), F32),
            pltpu.VMEM((tm, tm), BF16),
            pltpu.SMEM((N_SORT_CLASSES + 1,), jnp.int32),
        ],
        compiler_params=pltpu.CompilerParams(
            dimension_semantics=("arbitrary", "arbitrary"), vmem_limit_bytes=MOE_VMEM_LIMIT),
        name="moe",
    )(h2, route, x1, wg, wu, wd)


def _rope_tables(seq):
    pos = jnp.arange(seq, dtype=F32)
    inv_freq = ROPE_THETA ** (-jnp.arange(0, SUB_HEAD, 2, dtype=F32) / SUB_HEAD)
    ang = pos[:, None] * inv_freq[None, :]
    cos, sin = jnp.cos(ang), jnp.sin(ang)
    cos_full = jnp.concatenate([cos, cos, cos, cos], axis=-1)
    sin_signed = jnp.concatenate([-sin, sin, -sin, sin], axis=-1)
    return cos_full, sin_signed


def kernel(x, norm1_gain, w_in, b_gates, q_norm_gain, k_norm_gain, lambda_q1, lambda_k1, lambda_q2, lambda_k2, attn_subln_gain, w_attn_o, conv_w, conv_b, lru_wa, lru_ba, lru_wi, lru_bi, lru_lambda, w_lru_o, w_out, norm2_gain, w_group_router, b_group_router, w_expert_router, b_expert_router, w_expert_gate, w_expert_up, w_expert_down):
    b, s, d = x.shape
    n = b * s
    depth = w_in.shape[0]
    assert depth == 1 and d == D_MODEL
    cos, sin_signed = _rope_tables(s)
    x2 = x.reshape(n, d)
    l = 0

    sub_heads = ATTN_WIDTH // SUB_HEAD
    q_gain = jnp.tile(q_norm_gain[l], sub_heads)[None, :]
    k_gain = jnp.tile(k_norm_gain[l], sub_heads)[None, :]
    q_hm, kt_hm, v_hm, rest = _inproj(
        x, norm1_gain[l][None, :], w_in[l].astype(BF16), q_gain, k_gain, cos, sin_signed)
    proj3 = rest.reshape(b, s, REST_WIDTH)

    lam_params = jnp.stack([lambda_q1[l], lambda_k1[l], lambda_q2[l], lambda_k2[l]])
    attn_o = _attention(q_hm, kt_hm, v_hm, lam_params, attn_subln_gain[l][None, :],
                        q_gain, k_gain)

    w_cat = (0.5 * jnp.concatenate([lru_wa[l, 0], lru_wi[l, 0], lru_wa[l, 1], lru_wi[l, 1]],
                                   axis=-1)).astype(BF16)
    blk = lambda v: v.reshape(LRU_BLOCKS, 1, LRU_BLOCK_DIM)
    b_cat = 0.5 * jnp.concatenate([blk(lru_ba[l, 0]), blk(lru_bi[l, 0]),
                                   blk(lru_ba[l, 1]), blk(lru_bi[l, 1])], axis=-1)
    y_lru = _lru(proj3, conv_w[l], conv_b[l][None, :], w_cat, b_cat, lru_lambda[l])

    pad = ROUTER_LANES - N_GROUPS - N_EXPERTS
    w_router = jnp.concatenate([w_group_router[l], w_expert_router[l],
                                jnp.zeros((d, pad), F32)], axis=-1)
    b_router = jnp.concatenate([b_group_router[l], b_expert_router[l],
                                jnp.zeros((pad,), F32)])[None, :]
    w_router_hi = w_router.astype(BF16)
    w_router = jnp.concatenate(
        [w_router_hi, (w_router - w_router_hi.astype(F32)).astype(BF16)], axis=-1)
    x1, h2, cw = _merge(
        x2, attn_o, y_lru.reshape(n, LRU_WIDTH), rest,
        b_gates[l][None, :], w_attn_o[l].astype(BF16), w_lru_o[l].astype(BF16),
        w_out[l].astype(BF16), norm2_gain[l][None, :], w_router, b_router)

    out = _moe(h2, cw, x1, w_expert_gate[l].astype(BF16), w_expert_up[l].astype(BF16),
               w_expert_down[l].astype(BF16))
    return out.reshape(b, s, d)
```
